```python
import math
import jax, jax.numpy as jnp
from jax import lax
import numpy as np

D_MODEL = 1024
BATCH = 8
SEQ = 8192
DEPTH = 4

MIX_WIDTH = D_MODEL
GROUP_WIDTH = MIX_WIDTH // 4

S5_WIDTH = GROUP_WIDTH
S5_CH = 16
S5_GROUPS = S5_WIDTH // S5_CH
S5_STATE = 64
S5_DT_MIN = 1e-3
S5_DT_MAX = 1e-1

RWKV_HEAD = 64
RWKV_WIDTH = GROUP_WIDTH
RWKV_HEADS = RWKV_WIDTH // RWKV_HEAD
RWKV_DECAY_LORA = 64
RWKV_A_LORA = 64
RWKV_V_LORA = 32
RWKV_G_LORA = 128
RWKV_GN_EPS = 64e-5
RWKV_IN = 3 * RWKV_WIDTH + RWKV_DECAY_LORA + RWKV_A_LORA + RWKV_G_LORA

MLA_V = 64
MLA_HEADS = GROUP_WIDTH // MLA_V
MLA_NOPE = 64
MLA_ROPE = 32
MLA_Q_RANK = 256
MLA_KV_RANK = 128
MLA_WIDTH = MLA_HEADS * MLA_V
MLA_IN = MLA_Q_RANK + MLA_KV_RANK + MLA_ROPE
Q_BLOCK = 128

RET_V = 64
RET_HEADS = GROUP_WIDTH // RET_V
RET_QK = 32
RET_WIDTH = RET_HEADS * RET_V
RET_CHUNK = 128
RET_IN = 2 * RET_HEADS * RET_QK + 2 * RET_WIDTH

IN_WIDTH = S5_WIDTH + RWKV_IN + MLA_IN + RET_IN
ROPE_BASE = 10000.0
MAX_POS_OFFSET = 4096

N_EXPERTS = 32
TOP_K = 4
D_EXPERT = D_MODEL
SWIGLU_ALPHA = 1.702
SWIGLU_LIMIT = 7.0
MOE_BLOCK = 128

NORM_EPS = 1e-5
N_MOD = 6
F32 = jnp.float32

kernel_name = "hybrid_s5_rwkv7_mla_retention_moe_adaln"


def rmsnorm(x, g):
    xf = x.astype(F32)
    y = xf * lax.rsqrt(jnp.mean(xf * xf, axis=-1, keepdims=True) + NORM_EPS)
    return (y * g.astype(F32)).astype(x.dtype)


def rope(x, positions):
    d = x.shape[-1]
    inv = ROPE_BASE ** (-jnp.arange(0, d, 2, dtype=F32) / d)
    ang = positions.astype(F32)[..., None] * inv
    cos = jnp.cos(ang)[:, :, None, :]
    sin = jnp.sin(ang)[:, :, None, :]
    x1, x2 = jnp.split(x.astype(F32), 2, axis=-1)
    return jnp.concatenate([x1 * cos - x2 * sin, x1 * sin + x2 * cos], axis=-1).astype(x.dtype)


def _complex_affine_combine(e1, e2):
    a1r, a1i, b1r, b1i = e1
    a2r, a2i, b2r, b2i = e2
    return (a1r * a2r - a1i * a2i,
            a1r * a2i + a1i * a2r,
            a2r * b1r - a2i * b1i + b2r,
            a2r * b1i + a2i * b1r + b2i)


def s5_mixer(u, lam_re, lam_im, log_step, b_re, b_im, c_re, c_im, d_skip, glu_w, glu_b):
    out_dtype = u.dtype
    bsz, seq, _ = u.shape
    u = u.astype(F32)
    lam_re, lam_im, b_re, b_im, c_re, c_im = (t.astype(F32) for t in (lam_re, lam_im, b_re, b_im, c_re, c_im))
    dt = jnp.exp(log_step.astype(F32))[:, None]
    mag = jnp.exp(lam_re * dt)
    lb_re = mag * jnp.cos(lam_im * dt)
    lb_im = mag * jnp.sin(lam_im * dt)
    den = lam_re * lam_re + lam_im * lam_im
    n_re = lb_re - 1.0
    f_re = (n_re * lam_re + lb_im * lam_im) / den
    f_im = (lb_im * lam_re - n_re * lam_im) / den
    bb_re = f_re[..., None] * b_re - f_im[..., None] * b_im
    bb_im = f_re[..., None] * b_im + f_im[..., None] * b_re
    ug = u.reshape(bsz, seq, S5_GROUPS, S5_CH)
    bu_re = jnp.einsum('blgh,gph->blgp', ug, bb_re)
    bu_im = jnp.einsum('blgh,gph->blgp', ug, bb_im)
    a_re = jnp.broadcast_to(lb_re, bu_re.shape)
    a_im = jnp.broadcast_to(lb_im, bu_im.shape)
    _, _, s_re, s_im = lax.associative_scan(_complex_affine_combine, (a_re, a_im, bu_re, bu_im), axis=1)
    y = jnp.einsum('blgp,ghp->blgh', s_re, c_re) - jnp.einsum('blgp,ghp->blgh', s_im, c_im)
    y = y.reshape(bsz, seq, S5_WIDTH) + d_skip.astype(F32) * u
    y = jax.nn.gelu(y)
    y = y * jax.nn.sigmoid(y @ glu_w.astype(F32) + glu_b.astype(F32))
    return y.astype(out_dtype)


def wkv7_scan(r, w, k, v, kk, a):
    bsz, seq, nh, n = r.shape

    def step(state, inp):
        r_t, w_t, k_t, v_t, kk_t, a_t = inp
        sa = jnp.einsum('bhvk,bhk->bhv', state, -kk_t)
        state = (state * w_t[:, :, None, :]
                 + sa[..., None] * (kk_t * a_t)[:, :, None, :]
                 + v_t[..., None] * k_t[:, :, None, :])
        return state, jnp.einsum('bhvk,bhk->bhv', state, r_t)

    xs = tuple(jnp.moveaxis(t, 1, 0) for t in (r, w, k, v, kk, a))
    s0 = jnp.zeros((bsz, nh, n, n), F32)
    _, y = lax.scan(step, s0, xs)
    return jnp.moveaxis(y, 0, 1)


def rwkv7_mixer(h, v_first, mu, w0, w2, a0, a2, g2, k_k, k_a, r_k, ln_g, ln_b, vmix):
    out_dtype = h.dtype
    bsz, seq, _ = h.shape
    h = h.astype(F32)
    h_prev = jnp.pad(h, ((0, 0), (1, 0), (0, 0)))[:, :-1]
    h = h + (h_prev - h) * mu.astype(F32)
    cuts = np.cumsum([RWKV_WIDTH, RWKV_WIDTH, RWKV_WIDTH, RWKV_DECAY_LORA, RWKV_A_LORA]).tolist()
    r, k, v, wd, ad, gd = jnp.split(h, cuts, axis=-1)
    w = -jax.nn.softplus(-(w0 + jnp.tanh(wd) @ w2)) - 0.5
    decay = jnp.exp(-jnp.exp(w))
    a = jax.nn.sigmoid(a0 + ad @ a2)
    g = jax.nn.sigmoid(gd) @ g2
    if vmix is None:
        v_first = v
    else:
        v0, v1, v2 = vmix
        v = v + (v_first - v) * jax.nn.sigmoid(v0 + (v @ v1) @ v2)
    heads = lambda t: t.reshape(bsz, seq, RWKV_HEADS, RWKV_HEAD).astype(F32)
    kk = heads(k * k_k)
    kk = kk / jnp.maximum(jnp.sqrt(jnp.sum(kk * kk, axis=-1, keepdims=True)), 1e-12)
    k = k * (1.0 + (a - 1.0) * k_a)
    rh, kh, vh, ah, wh = heads(r), heads(k), heads(v), heads(a), heads(decay)
    y = wkv7_scan(rh, wh, kh, vh, kk, ah)
    mean = jnp.mean(y, axis=-1, keepdims=True)
    var = jnp.mean(jnp.square(y - mean), axis=-1, keepdims=True)
    y = ((y - mean) * lax.rsqrt(var + RWKV_GN_EPS)).reshape(bsz, seq, RWKV_WIDTH)
    y = y * ln_g.astype(F32) + ln_b.astype(F32)
    bonus = jnp.sum(rh * kh * r_k.astype(F32), axis=-1, keepdims=True) * vh
    y = (y + bonus.reshape(bsz, seq, RWKV_WIDTH)) * g
    return y.astype(out_dtype), v_first


def causal_attention(q, k, v, scale):
    bsz, seq, nh, dq = q.shape
    dv = v.shape[-1]
    nb = seq // Q_BLOCK
    qb = jnp.moveaxis(q.reshape(bsz, nb, Q_BLOCK, nh, dq), 1, 0)
    kpos = jnp.arange(seq)

    def one_block(args):
        q_blk, i = args
        s = jnp.einsum('bqhd,bkhd->bhqk', q_blk, k).astype(F32) * scale
        qpos = i * Q_BLOCK + jnp.arange(Q_BLOCK)
        s = jnp.where(kpos[None, :] <= qpos[:, None], s, -jnp.inf)
        p = jax.nn.softmax(s, axis=-1).astype(v.dtype)
        return jnp.einsum('bhqk,bkhd->bqhd', p, v)

    out = lax.map(one_block, (qb, jnp.arange(nb)))
    return jnp.moveaxis(out, 0, 1).reshape(bsz, seq, nh, dv)


def mla_mixer(h, positions, q_norm_g, kv_norm_g, w_q_up, w_kv_up):
    bsz, seq, _ = h.shape
    qc, kvc, k_pe = jnp.split(h, [MLA_Q_RANK, MLA_Q_RANK + MLA_KV_RANK], axis=-1)
    q = (rmsnorm(qc, q_norm_g) @ w_q_up).reshape(bsz, seq, MLA_HEADS, MLA_NOPE + MLA_ROPE)
    kv = (rmsnorm(kvc, kv_norm_g) @ w_kv_up).reshape(bsz, seq, MLA_HEADS, MLA_NOPE + MLA_V)
    q_nope, q_pe = jnp.split(q, [MLA_NOPE], axis=-1)
    k_nope, v = jnp.split(kv, [MLA_NOPE], axis=-1)
    q = jnp.concatenate([q_nope, rope(q_pe, positions)], axis=-1)
    k_pe = rope(k_pe[:, :, None, :], positions)
    k = jnp.concatenate([k_nope, jnp.broadcast_to(k_pe, (bsz, seq, MLA_HEADS, MLA_ROPE))], axis=-1)
    o = causal_attention(q, k, v, (MLA_NOPE + MLA_ROPE) ** -0.5)
    return o.reshape(bsz, seq, MLA_WIDTH)


def retention_mixer(h, positions):
    out_dtype = h.dtype
    bsz, seq, _ = h.shape
    nq = RET_HEADS * RET_QK
    q, k, v, g = jnp.split(h, [nq, 2 * nq, 2 * nq + RET_WIDTH], axis=-1)
    q = rope(q.reshape(bsz, seq, RET_HEADS, RET_QK), positions).astype(F32)
    k = rope(k.reshape(bsz, seq, RET_HEADS, RET_QK), positions).astype(F32) * (RET_QK ** -0.5)
    v = v.reshape(bsz, seq, RET_HEADS, RET_V).astype(F32)
    log_gamma = jnp.log1p(-jnp.exp2(-5.0 - jnp.arange(RET_HEADS, dtype=F32)))
    nc = seq // RET_CHUNK
    qc = q.reshape(bsz, nc, RET_CHUNK, RET_HEADS, RET_QK)
    kc = k.reshape(bsz, nc, RET_CHUNK, RET_HEADS, RET_QK)
    vc = v.reshape(bsz, nc, RET_CHUNK, RET_HEADS, RET_V)
    idx = jnp.arange(RET_CHUNK, dtype=F32)
    diff = idx[:, None] - idx[None, :]
    intra = jnp.where(diff >= 0, jnp.exp(jnp.maximum(diff, 0.0)[None] * log_gamma[:, None, None]), 0.0)
    scores = jnp.einsum('bnchd,bnmhd->bnhcm', qc, kc) * intra
    inner = jnp.einsum('bnhcm,bnmhe->bnche', scores, vc)
    k_w = jnp.exp((RET_CHUNK - 1.0 - idx)[:, None] * log_gamma[None, :])
    chunk_kv = jnp.einsum('bnchd,bnche,ch->nbhde', kc, vc, k_w)
    chunk_decay = jnp.exp(RET_CHUNK * log_gamma)[None, :, None, None]

    def step(state, kv):
        return state * chunk_decay + kv, state

    _, r_prev = lax.scan(step, jnp.zeros((bsz, RET_HEADS, RET_QK, RET_V), F32), chunk_kv)
    q_w = jnp.exp((idx + 1.0)[:, None] * log_gamma[None, :])
    cross = jnp.einsum('bnchd,nbhde,ch->bnche', qc, r_prev, q_w)
    o = (inner + cross).reshape(bsz, seq, RET_HEADS, RET_V)
    o = o * lax.rsqrt(jnp.mean(o * o, axis=-1, keepdims=True) + NORM_EPS)
    o = o.reshape(bsz, seq, RET_WIDTH) * jax.nn.silu(g.astype(F32))
    return o.astype(out_dtype)


def moe_ffn(h, router_w, router_b, w_gate, b_gate, w_up, b_up, w_down, b_down):
    bsz, seq, d = h.shape
    t = bsz * seq
    xt = h.reshape(t, d)
    logits = (xt @ router_w + router_b).astype(F32)
    top_val, top_idx = lax.top_k(logits, TOP_K)
    top_w = jax.nn.softmax(top_val, axis=-1)
    n_assign = t * TOP_K
    flat_e = top_idx.reshape(-1)
    flat_tok = jnp.arange(n_assign, dtype=jnp.int32) // TOP_K
    flat_w = top_w.reshape(-1)
    order = jnp.argsort(flat_e)
    sorted_e = flat_e[order]
    counts = jnp.bincount(flat_e, length=N_EXPERTS)
    padded = (counts + MOE_BLOCK - 1) // MOE_BLOCK * MOE_BLOCK
    pad_end = jnp.cumsum(padded)
    pad_start = pad_end - padded
    start = jnp.cumsum(counts) - counts
    dest = pad_start[sorted_e] + jnp.arange(n_assign) - start[sorted_e]
    p_rows = n_assign + N_EXPERTS * MOE_BLOCK
    n_blocks = p_rows // MOE_BLOCK
    buf_tok = jnp.full((p_rows,), t, jnp.int32).at[dest].set(flat_tok[order])
    buf_w = jnp.zeros((p_rows,), F32).at[dest].set(flat_w[order])
    blk_e = jnp.minimum(jnp.searchsorted(pad_end, jnp.arange(n_blocks) * MOE_BLOCK, side='right'), N_EXPERTS - 1)
    x_pad = jnp.concatenate([xt, jnp.zeros((1, d), xt.dtype)], axis=0)
    xb = x_pad[buf_tok].reshape(n_blocks, MOE_BLOCK, d)

    def expert_block(args):
        xblk, e = args
        gt = jnp.minimum(xblk @ w_gate[e] + b_gate[e], SWIGLU_LIMIT)
        up = jnp.clip(xblk @ w_up[e] + b_up[e], -SWIGLU_LIMIT, SWIGLU_LIMIT)
        act = gt * jax.nn.sigmoid(SWIGLU_ALPHA * gt) * (up + 1.0)
        return act @ w_down[e] + b_down[e]

    yb = lax.map(expert_block, (xb, blk_e)).reshape(p_rows, d)
    out = jnp.zeros((t + 1, d), F32).at[buf_tok].add(yb.astype(F32) * buf_w[:, None])
    return out[:t].reshape(bsz, seq, d).astype(h.dtype)


def setup_inputs(seed: int = 0) -> dict:
    key = jax.random.key(seed)
    ks = iter(jax.random.split(key, 64))

    def nrm(shape, scale):
        return jax.random.normal(next(ks), shape, F32) * scale

    def unif(shape, lo, hi):
        return jax.random.uniform(next(ks), shape, F32, lo, hi)

    L = DEPTH
    x = nrm((BATCH, SEQ, D_MODEL), 1.0)
    c = nrm((BATCH, D_MODEL), 1.0)
    offset = jax.random.randint(next(ks), (BATCH, 1), 0, MAX_POS_OFFSET, dtype=jnp.int32)
    positions = offset + jnp.arange(SEQ, dtype=jnp.int32)[None, :]
    inp = {
        "x": x, "c": c, "positions": positions,
        "ada_w": nrm((L, D_MODEL, N_MOD * D_MODEL), 0.5 * D_MODEL ** -0.5),
        "ada_b": nrm((L, N_MOD * D_MODEL), 0.02),
        "norm_mix_g": 1.0 + nrm((L, D_MODEL), 0.02),
        "norm_ffn_g": 1.0 + nrm((L, D_MODEL), 0.02),
        "w_in": nrm((L, D_MODEL, IN_WIDTH), D_MODEL ** -0.5),
        "w_out": nrm((L, MIX_WIDTH, D_MODEL), MIX_WIDTH ** -0.5),
        "s5_lambda_re": -0.5 + nrm((L, S5_GROUPS, S5_STATE), 0.01),
        "s5_lambda_im": math.pi * jnp.arange(S5_STATE, dtype=F32) + nrm((L, S5_GROUPS, S5_STATE), 0.01),
        "s5_log_step": unif((L, S5_GROUPS), math.log(S5_DT_MIN), math.log(S5_DT_MAX)),
        "s5_b_re": nrm((L, S5_GROUPS, S5_STATE, S5_CH), (2.0 * S5_CH) ** -0.5),
        "s5_b_im": nrm((L, S5_GROUPS, S5_STATE, S5_CH), (2.0 * S5_CH) ** -0.5),
        "s5_c_re": nrm((L, S5_GROUPS, S5_CH, S5_STATE), S5_STATE ** -0.5),
        "s5_c_im": nrm((L, S5_GROUPS, S5_CH, S5_STATE), S5_STATE ** -0.5),
        "s5_d": nrm((L, S5_WIDTH), 1.0),
        "s5_glu_w": nrm((L, S5_WIDTH, S5_WIDTH), S5_WIDTH ** -0.5),
        "s5_glu_b": nrm((L, S5_WIDTH), 0.02),
        "rw_mu": unif((L, RWKV_IN), 0.0, 1.0),
        "rw_w0": unif((L, RWKV_WIDTH), -6.0, -1.0),
        "rw_w2": nrm((L, RWKV_DECAY_LORA, RWKV_WIDTH), 0.1 * RWKV_DECAY_LORA ** -0.5),
        "rw_a0": nrm((L, RWKV_WIDTH), 0.1),
        "rw_a2": nrm((L, RWKV_A_LORA, RWKV_WIDTH), 0.1 * RWKV_A_LORA ** -0.5),
        "rw_g2": nrm((L, RWKV_G_LORA, RWKV_WIDTH), RWKV_G_LORA ** -0.5),
        "rw_k_k": 0.85 + nrm((L, RWKV_WIDTH), 0.02),
        "rw_k_a": 1.0 + nrm((L, RWKV_WIDTH), 0.02),
        "rw_r_k": nrm((L, RWKV_HEADS, RWKV_HEAD), 0.1),
        "rw_ln_g": 1.0 + nrm((L, RWKV_WIDTH), 0.02),
        "rw_ln_b": nrm((L, RWKV_WIDTH), 0.02),
        "rw_v0": 1.0 + nrm((L - 1, RWKV_WIDTH), 0.1),
        "rw_v1": nrm((L - 1, RWKV_WIDTH, RWKV_V_LORA), RWKV_WIDTH ** -0.5),
        "rw_v2": nrm((L - 1, RWKV_V_LORA, RWKV_WIDTH), 0.1 * RWKV_V_LORA ** -0.5),
        "mla_q_norm_g": 1.0 + nrm((L, MLA_Q_RANK), 0.02),
        "mla_kv_norm_g": 1.0 + nrm((L, MLA_KV_RANK), 0.02),
        "mla_w_q_up": nrm((L, MLA_Q_RANK, MLA_HEADS * (MLA_NOPE + MLA_ROPE)), MLA_Q_RANK ** -0.5),
        "mla_w_kv_up": nrm((L, MLA_KV_RANK, MLA_HEADS * (MLA_NOPE + MLA_V)), MLA_KV_RANK ** -0.5),
        "router_w": nrm((L, D_MODEL, N_EXPERTS), D_MODEL ** -0.5),
        "router_b": nrm((L, N_EXPERTS), 0.01),
        "ex_w_gate": nrm((L, N_EXPERTS, D_MODEL, D_EXPERT), D_MODEL ** -0.5),
        "ex_b_gate": nrm((L, N_EXPERTS, D_EXPERT), 0.01),
        "ex_w_up": nrm((L, N_EXPERTS, D_MODEL, D_EXPERT), D_MODEL ** -0.5),
        "ex_b_up": nrm((L, N_EXPERTS, D_EXPERT), 0.01),
        "ex_w_down": nrm((L, N_EXPERTS, D_EXPERT, D_MODEL), D_EXPERT ** -0.5),
        "ex_b_down": nrm((L, N_EXPERTS, D_MODEL), 0.01),
        "final_norm_g": 1.0 + nrm((D_MODEL,), 0.02),
    }
    return inp


def reference(x, c, positions, ada_w, ada_b, norm_mix_g, norm_ffn_g, w_in, w_out,
              s5_lambda_re, s5_lambda_im, s5_log_step, s5_b_re, s5_b_im, s5_c_re, s5_c_im,
              s5_d, s5_glu_w, s5_glu_b,
              rw_mu, rw_w0, rw_w2, rw_a0, rw_a2, rw_g2, rw_k_k, rw_k_a, rw_r_k, rw_ln_g, rw_ln_b,
              rw_v0, rw_v1, rw_v2,
              mla_q_norm_g, mla_kv_norm_g, mla_w_q_up, mla_w_kv_up,
              router_w, router_b, ex_w_gate, ex_b_gate, ex_w_up, ex_b_up, ex_w_down, ex_b_down,
              final_norm_g):
    split_pts = np.cumsum([S5_WIDTH, RWKV_IN, MLA_IN]).tolist()
    cond = jax.nn.silu(c)
    v_first = None
    for l in range(DEPTH):
        mod = (cond @ ada_w[l] + ada_b[l])[:, None, :]
        sh1, sc1, g1, sh2, sc2, g2 = jnp.split(mod, N_MOD, axis=-1)
        hn = rmsnorm(x, norm_mix_g[l]) * (1.0 + sc1) + sh1
        proj = hn @ w_in[l]
        s5_u, rw_in, mla_in, ret_in = jnp.split(proj, split_pts, axis=-1)
        y_s5 = s5_mixer(s5_u, s5_lambda_re[l], s5_lambda_im[l], s5_log_step[l], s5_b_re[l], s5_b_im[l],
                        s5_c_re[l], s5_c_im[l], s5_d[l], s5_glu_w[l], s5_glu_b[l])
        vmix = None if l == 0 else (rw_v0[l - 1], rw_v1[l - 1], rw_v2[l - 1])
        y_rw, v_first = rwkv7_mixer(rw_in, v_first, rw_mu[l], rw_w0[l], rw_w2[l], rw_a0[l], rw_a2[l],
                                    rw_g2[l], rw_k_k[l], rw_k_a[l], rw_r_k[l], rw_ln_g[l], rw_ln_b[l], vmix)
        y_mla = mla_mixer(mla_in, positions, mla_q_norm_g[l], mla_kv_norm_g[l], mla_w_q_up[l], mla_w_kv_up[l])
        y_ret = retention_mixer(ret_in, positions)
        mixed = jnp.concatenate([y_s5, y_rw.astype(x.dtype), y_mla.astype(x.dtype), y_ret], axis=-1) @ w_out[l]
        x = x + g1 * mixed
        hn = rmsnorm(x, norm_ffn_g[l]) * (1.0 + sc2) + sh2
        x = x + g2 * moe_ffn(hn, router_w[l], router_b[l], ex_w_gate[l], ex_b_gate[l], ex_w_up[l], ex_b_up[l],
                             ex_w_down[l], ex_b_down[l])
    return rmsnorm(x, final_norm_g)
```

```python
import functools
import math

import numpy as np
import jax
import jax.numpy as jnp
from jax import lax
from jax.experimental import pallas as pl
from jax.experimental.pallas import tpu as pltpu

F32 = jnp.float32
BF16 = jnp.bfloat16

D_MODEL = 1024
GROUP_WIDTH = 256
S5_CH = 16
S5_GROUPS = 16
S5_STATE = 64
S5_FLAT = S5_GROUPS * S5_STATE
RW_HEADS = 4
RW_HEAD = 64
RW_GN_EPS = 64e-5
MLA_HEADS = 4
MLA_NOPE = 64
MLA_ROPE = 32
MLA_V = 64
MLA_Q_RANK = 256
MLA_KV_RANK = 128
RET_HEADS = 4
RET_QK = 32
RET_V = 64
ROPE_BASE = 10000.0
N_EXPERTS = 32
TOP_K = 4
SWIGLU_ALPHA = 1.702
SWIGLU_LIMIT = 7.0
NORM_EPS = 1e-5
N_MOD = 6

LANES = 128
SUBLANES = 8
VMEM_LIMIT_BYTES = 56 * 1024 * 1024

IN_S5 = (0, 256)
IN_RW = (256, 1280)
IN_MLA = (1280, 1920)
IN_RET = (1920, 2944)
IN_PACKED = 2944


def _params(*sem):
    return pltpu.CompilerParams(dimension_semantics=sem, vmem_limit_bytes=VMEM_LIMIT_BYTES)


def _bdot(a, b):
    return jnp.dot(a.astype(BF16), b.astype(BF16), preferred_element_type=F32)


def _split_dot(a, b_bf16):
    hi = a.astype(BF16)
    lo = (a - hi.astype(F32)).astype(BF16)
    return (jnp.dot(hi, b_bf16, preferred_element_type=F32)
            + jnp.dot(lo, b_bf16, preferred_element_type=F32))


def _sigmoid(x):
    return 1.0 / (1.0 + jnp.exp(-x))


def _adaln_kernel(c_ref, w_ref, b_ref, o_ref):
    c = c_ref[...]
    cond = c * _sigmoid(c)
    o_ref[0] = _bdot(cond, w_ref[0]) + b_ref[0]


def adaln_mod(c, ada_w, ada_b):
    depth, d, n = ada_w.shape
    bsz = c.shape[0]
    tn = 1536
    return pl.pallas_call(
        _adaln_kernel,
        grid=(depth, n // tn),
        in_specs=[pl.BlockSpec((bsz, d), lambda l, j: (0, 0)),
                  pl.BlockSpec((1, d, tn), lambda l, j: (l, 0, j)),
                  pl.BlockSpec((1, 1, tn), lambda l, j: (l, 0, j))],
        out_specs=pl.BlockSpec((1, bsz, tn), lambda l, j: (l, 0, j)),
        out_shape=jax.ShapeDtypeStruct((depth, bsz, n), F32),
        compiler_params=_params("parallel", "parallel"),
        name="adaln_mod",
    )(c, ada_w, ada_b.reshape(depth, 1, n))


def _inproj_kernel(x_ref, sc_ref, sh_ref, g_ref, w_ref, s5_ref, rw_ref, mla_ref, ret_ref):
    x = x_ref[0]
    ms = jnp.mean(x * x, axis=-1, keepdims=True)
    hn = x * lax.rsqrt(ms + NORM_EPS) * g_ref[...]
    hn = hn * (1.0 + sc_ref[0]) + sh_ref[0]
    p = jnp.dot(hn.astype(BF16), w_ref[...], preferred_element_type=F32)
    s5_ref[0] = p[:, IN_S5[0]:IN_S5[1]]
    rw_ref[0] = p[:, IN_RW[0]:IN_RW[1]]
    mla_ref[0] = p[:, IN_MLA[0]:IN_MLA[1]]
    ret_ref[0] = p[:, IN_RET[0]:IN_RET[1]]


def in_proj(x, sc, sh, g, w_packed, tm=512):
    bsz, seq, d = x.shape
    widths = [b - a for a, b in (IN_S5, IN_RW, IN_MLA, IN_RET)]
    row = lambda b, i: (b, i, 0)
    per_b = lambda b, i: (b, 0, 0)
    return pl.pallas_call(
        _inproj_kernel,
        grid=(bsz, seq // tm),
        in_specs=[pl.BlockSpec((1, tm, d), row),
                  pl.BlockSpec((1, 1, d), per_b),
                  pl.BlockSpec((1, 1, d), per_b),
                  pl.BlockSpec((1, d), lambda b, i: (0, 0)),
                  pl.BlockSpec((d, IN_PACKED), lambda b, i: (0, 0))],
        out_specs=[pl.BlockSpec((1, tm, w), row) for w in widths],
        out_shape=[jax.ShapeDtypeStruct((bsz, seq, w), F32) for w in widths],
        compiler_params=_params("parallel", "parallel"),
        name="in_proj",
    )(x, sc, sh, g.reshape(1, d), w_packed)


def _swap_halves(cols, block):
    cols = np.asarray(cols).reshape(-1, 2, block // 2)
    return cols[:, ::-1, :].reshape(-1)


def pack_w_in(w_in_l):
    zero = w_in_l.shape[1]
    s5 = np.arange(0, 256)
    rw = np.arange(256, 1280)
    qc = np.arange(1280, 1536)
    kvc = np.arange(1536, 1664)
    kpe = np.arange(1664, 1696)
    z = lambda n: np.full((n,), zero)
    kpe_slot = np.concatenate([z(MLA_NOPE), kpe, z(LANES - MLA_NOPE - MLA_ROPE)])
    kpe_sw_slot = np.concatenate([z(MLA_NOPE), _swap_halves(kpe, MLA_ROPE), z(LANES - MLA_NOPE - MLA_ROPE)])
    rq = np.arange(1696, 1824)
    rk = np.arange(1824, 1952)
    rv = np.arange(1952, 2208)
    rg = np.arange(2208, 2464)
    idx = np.concatenate([s5, rw, qc, kvc, kpe_slot, kpe_sw_slot,
                          rq, rk, _swap_halves(rq, RET_QK), _swap_halves(rk, RET_QK), rv, rg])
    assert idx.shape[0] == IN_PACKED
    w_ext = jnp.concatenate([w_in_l, jnp.zeros((w_in_l.shape[0], 1), w_in_l.dtype)], axis=1)
    return jnp.take(w_ext, jnp.asarray(idx, jnp.int32), axis=1).astype(BF16)


def rope_tables(positions):
    inv = ROPE_BASE ** (-jnp.arange(0, MLA_ROPE, 2, dtype=F32) / MLA_ROPE)
    ang = positions.astype(F32)[..., None] * inv
    cos, sin = jnp.cos(ang), jnp.sin(ang)
    reps = LANES // MLA_ROPE
    cos_t = jnp.tile(jnp.concatenate([cos, cos], axis=-1), (1, 1, reps))
    sin_t = jnp.tile(jnp.concatenate([-sin, sin], axis=-1), (1, 1, reps))
    return cos_t, sin_t


def _s5_kernel(u_ref, bre_ref, bim_ref, cre_ref, cim_ref, lam_ref, lamq_ref, lamseg_ref,
               d_ref, gw_ref, gb_ref, o_ref, sre_ref, sim_ref, st_ref, *, nq):
    @pl.when(pl.program_id(1) == 0)
    def _():
        st_ref[...] = jnp.zeros_like(st_ref)

    u = u_ref[0]
    ub = u.astype(BF16)
    sre_ref[...] = jnp.dot(ub, bre_ref[...], preferred_element_type=F32)
    sim_ref[...] = jnp.dot(ub, bim_ref[...], preferred_element_type=F32)
    lam_re = lam_ref[0:1, :]
    lam_im = lam_ref[1:2, :]

    def scan_body(q, carry):
        cr, ci = carry
        rows = pl.ds(pl.multiple_of(q * SUBLANES, SUBLANES), SUBLANES)
        nr = lam_re * cr - lam_im * ci + sre_ref[rows, :]
        ni = lam_re * ci + lam_im * cr + sim_ref[rows, :]
        sre_ref[rows, :] = nr
        sim_ref[rows, :] = ni
        return nr, ni

    zero = jnp.zeros((SUBLANES, S5_FLAT), F32)
    end_re, end_im = lax.fori_loop(0, nq, scan_body, (zero, zero))

    seg_re = lamseg_ref[0:1, :]
    seg_im = lamseg_ref[1:2, :]
    cr, ci = st_ref[0:1, :], st_ref[1:2, :]
    in_re, in_im = [], []
    for r in range(SUBLANES):
        in_re.append(cr)
        in_im.append(ci)
        er, ei = end_re[r:r + 1, :], end_im[r:r + 1, :]
        cr, ci = seg_re * cr - seg_im * ci + er, seg_re * ci + seg_im * cr + ei
    st_ref[0:1, :] = cr
    st_ref[1:2, :] = ci
    car_re = jnp.concatenate(in_re, axis=0)
    car_im = jnp.concatenate(in_im, axis=0)

    def fix_body(q, _):
        rows = pl.ds(pl.multiple_of(q * SUBLANES, SUBLANES), SUBLANES)
        pr = lamq_ref[0, pl.ds(q, 1), :]
        pi = lamq_ref[1, pl.ds(q, 1), :]
        sre_ref[rows, :] = sre_ref[rows, :] + (pr * car_re - pi * car_im)
        sim_ref[rows, :] = sim_ref[rows, :] + (pr * car_im + pi * car_re)
        return 0

    lax.fori_loop(0, nq, fix_body, 0)

    y = (jnp.dot(sre_ref[...].astype(BF16), cre_ref[...], preferred_element_type=F32)
         - jnp.dot(sim_ref[...].astype(BF16), cim_ref[...], preferred_element_type=F32))
    y = y + d_ref[...] * u
    y = jax.nn.gelu(y)
    gate = jnp.dot(y.astype(BF16), gw_ref[...], preferred_element_type=F32) + gb_ref[...]
    o_ref[0] = y * _sigmoid(gate)


def s5_prepare(lam_re, lam_im, log_step, b_re, b_im, c_re, c_im, nq):
    dt = jnp.exp(log_step.astype(F32))[:, None]
    mag = jnp.exp(lam_re * dt)
    lb_re = mag * jnp.cos(lam_im * dt)
    lb_im = mag * jnp.sin(lam_im * dt)
    den = lam_re * lam_re + lam_im * lam_im
    n_re = lb_re - 1.0
    f_re = (n_re * lam_re + lb_im * lam_im) / den
    f_im = (lb_im * lam_re - n_re * lam_im) / den
    bb_re = f_re[..., None] * b_re - f_im[..., None] * b_im
    bb_im = f_re[..., None] * b_im + f_im[..., None] * b_re
    eye = jnp.eye(S5_GROUPS, dtype=F32)
    bd_in = lambda t: jnp.einsum('gph,gk->ghkp', t, eye).reshape(GROUP_WIDTH, S5_FLAT).astype(BF16)
    bd_out = lambda t: jnp.einsum('ghp,gk->gpkh', t, eye).reshape(S5_FLAT, GROUP_WIDTH).astype(BF16)
    lam = jnp.stack([lb_re.reshape(-1), lb_im.reshape(-1)])

    def power(n):
        n = n[:, None, None]
        m = jnp.exp(n * (lam_re * dt)[None])
        a = n * (lam_im * dt)[None]
        return jnp.stack([(m * jnp.cos(a)).reshape(-1, S5_FLAT), (m * jnp.sin(a)).reshape(-1, S5_FLAT)])

    lam_q = power(jnp.arange(1, nq + 1, dtype=F32))
    lam_seg = power(jnp.full((1,), float(nq), F32))[:, 0, :]
    return bd_in(bb_re), bd_in(bb_im), bd_out(c_re), bd_out(c_im), lam, lam_q, lam_seg


def s5_mixer(u, prep, d_skip, glu_w, glu_b, chunk=512):
    bsz, seq, w = u.shape
    nq = chunk // SUBLANES
    nchunk = seq // chunk
    bre, bim, cre, cim, lam, lam_q, lam_seg = prep
    up = u.reshape(bsz, nchunk, SUBLANES, nq, w).transpose(0, 1, 3, 2, 4).reshape(bsz, seq, w)
    const2 = lambda b, i: (0, 0)
    out = pl.pallas_call(
        functools.partial(_s5_kernel, nq=nq),
        grid=(bsz, nchunk),
        in_specs=[pl.BlockSpec((1, chunk, w), lambda b, i: (b, i, 0)),
                  pl.BlockSpec((w, S5_FLAT), const2),
                  pl.BlockSpec((w, S5_FLAT), const2),
                  pl.BlockSpec((S5_FLAT, w), const2),
                  pl.BlockSpec((S5_FLAT, w), const2),
                  pl.BlockSpec((2, S5_FLAT), const2),
                  pl.BlockSpec((2, nq, S5_FLAT), lambda b, i: (0, 0, 0)),
                  pl.BlockSpec((2, S5_FLAT), const2),
                  pl.BlockSpec((1, w), const2),
                  pl.BlockSpec((w, w), const2),
                  pl.BlockSpec((1, w), const2)],
        out_specs=pl.BlockSpec((1, chunk, w), lambda b, i: (b, i, 0)),
        out_shape=jax.ShapeDtypeStruct((bsz, seq, w), F32),
        scratch_shapes=[pltpu.VMEM((chunk, S5_FLAT), F32),
                        pltpu.VMEM((chunk, S5_FLAT), F32),
                        pltpu.VMEM((2, S5_FLAT), F32)],
        compiler_params=_params("parallel", "arbitrary"),
        name="s5_mixer",
    )(up, bre, bim, cre, cim, lam, lam_q, lam_seg,
      d_skip.reshape(1, w), glu_w.astype(BF16), glu_b.reshape(1, w))
    return out.reshape(bsz, nchunk, nq, SUBLANES, w).transpose(0, 1, 3, 2, 4).reshape(bsz, seq, w)


def _ret_kernel(h_ref, cos_ref, sin_ref, intra_ref, qw_ref, kw_ref, dec_ref, ones_ref,
                o_ref, st_ref, *, chunk):
    @pl.when(pl.program_id(1) == 0)
    def _():
        st_ref[...] = jnp.zeros_like(st_ref)

    h = h_ref[0]
    cos = cos_ref[0]
    sin = sin_ref[0]
    nqk = RET_HEADS * RET_QK
    q = h[:, 0:nqk] * cos + h[:, 2 * nqk:3 * nqk] * sin
    k = (h[:, nqk:2 * nqk] * cos + h[:, 3 * nqk:4 * nqk] * sin) * (RET_QK ** -0.5)
    v = h[:, 4 * nqk:4 * nqk + GROUP_WIDTH]
    g = h[:, 4 * nqk + GROUP_WIDTH:]
    kb = k.astype(BF16)
    lane_qk = lax.broadcasted_iota(jnp.int32, (chunk, nqk), 1) // RET_QK
    lane_v = lax.broadcasted_iota(jnp.int32, (chunk, GROUP_WIDTH), 1) // RET_V
    state = st_ref[...]
    o = _bdot(q * qw_ref[...], state)
    for hd in range(RET_HEADS):
        qh = jnp.where(lane_qk == hd, q, 0.0).astype(BF16)
        s = lax.dot_general(qh, kb, (((1,), (1,)), ((), ())), preferred_element_type=F32)
        s = s * intra_ref[hd]
        vh = jnp.where(lane_v == hd, v, 0.0).astype(BF16)
        o = o + jnp.dot(s.astype(BF16), vh, preferred_element_type=F32)
    kv = lax.dot_general((k * kw_ref[...]).astype(BF16), v.astype(BF16),
                         (((0,), (0,)), ((), ())), preferred_element_type=F32)
    dec = dec_ref[...]
    st_ref[...] = state * dec + jnp.where(dec > 0.0, kv, 0.0)
    ms = _split_dot(o * o, ones_ref[...])
    o = o * lax.rsqrt(ms + NORM_EPS)
    o_ref[0] = o * (g * _sigmoid(g))


def retention_tables(chunk):
    log_gamma = np.log1p(-np.exp2(-5.0 - np.arange(RET_HEADS, dtype=np.float64)))
    idx = np.arange(chunk, dtype=np.float64)
    diff = idx[:, None] - idx[None, :]
    intra = np.where(diff >= 0, np.exp(np.maximum(diff, 0.0)[None] * log_gamma[:, None, None]), 0.0)
    q_w = np.repeat(np.exp((idx + 1.0)[:, None] * log_gamma[None, :]), RET_QK, axis=1)
    k_w = np.repeat(np.exp((chunk - 1.0 - idx)[:, None] * log_gamma[None, :]), RET_QK, axis=1)
    head_q = np.arange(RET_HEADS * RET_QK) // RET_QK
    head_v = np.arange(GROUP_WIDTH) // RET_V
    same = head_q[:, None] == head_v[None, :]
    dec = np.where(same, np.exp(chunk * log_gamma)[head_q][:, None], 0.0)
    ones = (head_v[:, None] == head_v[None, :]).astype(np.float64) / RET_V
    f = lambda a: jnp.asarray(a, F32)
    return f(intra), f(q_w), f(k_w), f(dec), jnp.asarray(ones, BF16)


def retention_mixer(h, cos_t, sin_t, chunk=256):
    bsz, seq, wh = h.shape
    intra, q_w, k_w, dec, ones = retention_tables(chunk)
    nqk = RET_HEADS * RET_QK
    row = lambda b, i: (b, i, 0)
    c2 = lambda b, i: (0, 0)
    return pl.pallas_call(
        functools.partial(_ret_kernel, chunk=chunk),
        grid=(bsz, seq // chunk),
        in_specs=[pl.BlockSpec((1, chunk, wh), row),
                  pl.BlockSpec((1, chunk, LANES), row),
                  pl.BlockSpec((1, chunk, LANES), row),
                  pl.BlockSpec((RET_HEADS, chunk, chunk), lambda b, i: (0, 0, 0)),
                  pl.BlockSpec((chunk, nqk), c2),
                  pl.BlockSpec((chunk, nqk), c2),
                  pl.BlockSpec((nqk, GROUP_WIDTH), c2),
                  pl.BlockSpec((GROUP_WIDTH, GROUP_WIDTH), c2)],
        out_specs=pl.BlockSpec((1, chunk, GROUP_WIDTH), row),
        out_shape=jax.ShapeDtypeStruct((bsz, seq, GROUP_WIDTH), F32),
        scratch_shapes=[pltpu.VMEM((nqk, GROUP_WIDTH), F32)],
        compiler_params=_params("parallel", "arbitrary"),
        name="retention",
    )(h, cos_t, sin_t, intra, q_w, k_w, dec, ones)


def _mla_prep_kernel(h_ref, cos_ref, sin_ref, qg_ref, kvg_ref, wqa_ref, wqb_ref, wk_ref, wv_ref,
                     q_ref, k_ref, v_ref, *, scale):
    h = h_ref[0]
    tm = h.shape[0]
    lane = lax.broadcasted_iota(jnp.int32, (tm, LANES), 1)
    is_nope = lane < MLA_NOPE
    is_rope = jnp.logical_and(lane >= MLA_NOPE, lane < MLA_NOPE + MLA_ROPE)
    cm = jnp.where(is_nope, 1.0, jnp.where(is_rope, cos_ref[0], 0.0))
    sm = jnp.where(is_rope, sin_ref[0], 0.0)

    qc = h[:, 0:MLA_Q_RANK]
    qn = (qc * lax.rsqrt(jnp.mean(qc * qc, axis=-1, keepdims=True) + NORM_EPS) * qg_ref[...]).astype(BF16)
    kvc = h[:, MLA_Q_RANK:MLA_Q_RANK + MLA_KV_RANK]
    kvn = (kvc * lax.rsqrt(jnp.mean(kvc * kvc, axis=-1, keepdims=True) + NORM_EPS) * kvg_ref[...]).astype(BF16)
    off = MLA_Q_RANK + MLA_KV_RANK
    kpe = h[:, off:off + LANES] * cm + h[:, off + LANES:off + 2 * LANES] * sm

    qa = jnp.dot(qn, wqa_ref[...], preferred_element_type=F32)
    qb = jnp.dot(qn, wqb_ref[...], preferred_element_type=F32)
    kn = jnp.dot(kvn, wk_ref[...], preferred_element_type=F32)
    v_ref[0] = jnp.dot(kvn, wv_ref[...], preferred_element_type=F32).astype(BF16)
    for hd in range(MLA_HEADS):
        sl = slice(hd * LANES, (hd + 1) * LANES)
        q_ref[0, hd] = ((qa[:, sl] * cm + qb[:, sl] * sm) * scale).astype(BF16)
        k_ref[0, hd] = (kn[:, sl] + kpe).astype(BF16)


def mla_pack_weights(w_q_up, w_kv_up):
    dq = MLA_NOPE + MLA_ROPE
    zq = w_q_up.shape[1]
    zk = w_kv_up.shape[1]
    z = lambda n, zero: np.full((n,), zero)
    ia, ib, ik, iv = [], [], [], []
    for hd in range(MLA_HEADS):
        nope = np.arange(hd * dq, hd * dq + MLA_NOPE)
        pe = np.arange(hd * dq + MLA_NOPE, (hd + 1) * dq)
        pad = LANES - dq
        ia += [nope, pe, z(pad, zq)]
        ib += [z(MLA_NOPE, zq), _swap_halves(pe, MLA_ROPE), z(pad, zq)]
        kv0 = hd * (MLA_NOPE + MLA_V)
        ik += [np.arange(kv0, kv0 + MLA_NOPE), z(LANES - MLA_NOPE, zk)]
        iv += [np.arange(kv0 + MLA_NOPE, kv0 + MLA_NOPE + MLA_V)]
    ext = lambda w: jnp.concatenate([w, jnp.zeros((w.shape[0], 1), w.dtype)], axis=1)
    take = lambda w, idx: jnp.take(ext(w), jnp.asarray(np.concatenate(idx), jnp.int32), axis=1).astype(BF16)
    return take(w_q_up, ia), take(w_q_up, ib), take(w_kv_up, ik), take(w_kv_up, iv)


def mla_prep(h, cos_t, sin_t, q_norm_g, kv_norm_g, packed, tm=512):
    bsz, seq, wh = h.shape
    wqa, wqb, wk, wv = packed
    row = lambda b, i: (b, i, 0)
    c2 = lambda b, i: (0, 0)
    hrow = lambda b, i: (b, 0, i, 0)
    scale = (MLA_NOPE + MLA_ROPE) ** -0.5
    return pl.pallas_call(
        functools.partial(_mla_prep_kernel, scale=scale),
        grid=(bsz, seq // tm),
        in_specs=[pl.BlockSpec((1, tm, wh), row),
                  pl.BlockSpec((1, tm, LANES), row),
                  pl.BlockSpec((1, tm, LANES), row),
                  pl.BlockSpec((1, MLA_Q_RANK), c2),
                  pl.BlockSpec((1, MLA_KV_RANK), c2),
                  pl.BlockSpec(wqa.shape, c2),
                  pl.BlockSpec(wqb.shape, c2),
                  pl.BlockSpec(wk.shape, c2),
                  pl.BlockSpec(wv.shape, c2)],
        out_specs=[pl.BlockSpec((1, MLA_HEADS, tm, LANES), hrow),
                   pl.BlockSpec((1, MLA_HEADS, tm, LANES), hrow),
                   pl.BlockSpec((1, tm, GROUP_WIDTH), row)],
        out_shape=[jax.ShapeDtypeStruct((bsz, MLA_HEADS, seq, LANES), BF16),
                   jax.ShapeDtypeStruct((bsz, MLA_HEADS, seq, LANES), BF16),
                   jax.ShapeDtypeStruct((bsz, seq, GROUP_WIDTH), BF16)],
        compiler_params=_params("parallel", "parallel"),
        name="mla_prep",
    )(h, cos_t, sin_t, q_norm_g.reshape(1, -1), kv_norm_g.reshape(1, -1), wqa, wqb, wk, wv)


def _attn_kernel(q_ref, k_ref, v_ref, o_ref, m_ref, l_ref, acc_ref, *, blk):
    qi = pl.program_id(2)
    rep = blk // LANES

    def step(hh, j, masked):
        q = q_ref[0, hh]
        rows = pl.ds(pl.multiple_of(j * blk, blk), blk)
        ks = k_ref[0, hh, rows, :]
        vs = v_ref[0, rows, :]
        s = lax.dot_general(q, ks, (((1,), (1,)), ((), ())), preferred_element_type=F32)
        if masked:
            r = lax.broadcasted_iota(jnp.int32, (blk, blk), 0)
            c = lax.broadcasted_iota(jnp.int32, (blk, blk), 1)
            s = jnp.where(c <= r, s, -jnp.inf)
        m_prev = m_ref[hh]
        m_new = jnp.maximum(m_prev, jnp.max(s, axis=-1, keepdims=True))
        p = jnp.exp(s - pltpu.repeat(m_new, rep, axis=1))
        alpha = jnp.exp(m_prev - m_new)
        l_ref[hh] = alpha * l_ref[hh] + jnp.sum(p, axis=-1, keepdims=True)
        acc_ref[hh] = alpha * acc_ref[hh] + jnp.dot(p.astype(BF16), vs, preferred_element_type=F32)
        m_ref[hh] = m_new

    for hh in range(2):
        m_ref[hh] = jnp.full((blk, LANES), -jnp.inf, F32)
        l_ref[hh] = jnp.zeros((blk, LANES), F32)
        acc_ref[hh] = jnp.zeros((blk, LANES), F32)

        def body(j, _, hh=hh):
            step(hh, j, False)
            return 0

        lax.fori_loop(0, qi, body, 0)
        step(hh, qi, True)

    lane = lax.broadcasted_iota(jnp.int32, (blk, LANES), 1)
    o0 = acc_ref[0] / l_ref[0]
    o1 = acc_ref[1] / l_ref[1]
    o_ref[0] = jnp.where(lane < MLA_V, o0, o1)


def causal_attention(q, k, v, blk=512):
    bsz, nh, seq, dk = q.shape
    return pl.pallas_call(
        functools.partial(_attn_kernel, blk=blk),
        grid=(bsz, nh // 2, seq // blk),
        in_specs=[pl.BlockSpec((1, 2, blk, dk), lambda b, p, i: (b, p, i, 0)),
                  pl.BlockSpec((1, 2, seq, dk), lambda b, p, i: (b, p, 0, 0)),
                  pl.BlockSpec((1, seq, LANES), lambda b, p, i: (b, 0, p))],
        out_specs=pl.BlockSpec((1, blk, LANES), lambda b, p, i: (b, i, p)),
        out_shape=jax.ShapeDtypeStruct((bsz, seq, nh // 2 * LANES), F32),
        scratch_shapes=[pltpu.VMEM((2, blk, LANES), F32)] * 3,
        compiler_params=_params("parallel", "parallel", "arbitrary"),
        name="mla_attention",
    )(q, k, v)


def _rw_prep_kernel(*refs, first):
    if first:
        (h_ref, prev_ref, mu_ref, w0_ref, w2_ref, a0_ref, a2_ref, g2_ref, kk_ref, ka_ref, rk_ref, ones_ref,
         rT_ref, wT_ref, kT_ref, kkT_ref, bT_ref, v_ref, bonus_ref, g_ref) = refs
    else:
        (h_ref, prev_ref, mu_ref, w0_ref, w2_ref, a0_ref, a2_ref, g2_ref, kk_ref, ka_ref, rk_ref, ones_ref,
         vf_ref, v0_ref, v1_ref, v2_ref,
         rT_ref, wT_ref, kT_ref, kkT_ref, bT_ref, v_ref, bonus_ref, g_ref) = refs
    h = h_ref[0]
    tm = h.shape[0]
    last = prev_ref[0, SUBLANES - 1:SUBLANES, :]
    last = jnp.where(pl.program_id(1) == 0, 0.0, last)
    row = lax.broadcasted_iota(jnp.int32, h.shape, 0)
    h_prev = jnp.where(row == 0, last, pltpu.roll(h, 1, axis=0))
    h = h + (h_prev - h) * mu_ref[...]
    w = GROUP_WIDTH
    r = h[:, 0:w]
    k = h[:, w:2 * w]
    v = h[:, 2 * w:3 * w]
    wa = h[:, 3 * w:3 * w + LANES]
    gd = h[:, 3 * w + LANES:]
    z = w0_ref[...] + _bdot(jnp.tanh(wa), w2_ref[...])
    nz = -z
    softplus = jnp.maximum(nz, 0.0) + jnp.log(1.0 + jnp.exp(-jnp.abs(nz)))
    decay = jnp.exp(-jnp.exp(-softplus - 0.5))
    a = _sigmoid(a0_ref[...] + _bdot(wa, a2_ref[...]))
    g_ref[0] = _bdot(_sigmoid(gd), g2_ref[...])
    if not first:
        mix = _sigmoid(v0_ref[...] + _bdot(_bdot(v, v1_ref[...]), v2_ref[...]))
        v = v + (vf_ref[0] - v) * mix
    kk = k * kk_ref[...]
    norm = jnp.sqrt(_split_dot(kk * kk, ones_ref[...]))
    kk = kk / jnp.maximum(norm, 1e-12)
    k = k * (1.0 + (a - 1.0) * ka_ref[...])
    v_ref[0] = v
    bonus_ref[0] = _split_dot(r * k * rk_ref[...], ones_ref[...]) * v
    rT_ref[0] = r.T
    wT_ref[0] = decay.T
    kT_ref[0] = k.T
    kkT_ref[0] = kk.T
    bT_ref[0] = (kk * a).T


def rw_pack_weights(w2, a2, v1=None, v2=None):
    zeros = lambda n, m: jnp.zeros((n, m), F32)
    half = LANES // 2
    w2p = jnp.concatenate([w2, zeros(half, GROUP_WIDTH)], axis=0).astype(BF16)
    a2p = jnp.concatenate([zeros(half, GROUP_WIDTH), a2], axis=0).astype(BF16)
    if v1 is None:
        return w2p, a2p
    v1p = jnp.concatenate([v1, zeros(GROUP_WIDTH, LANES - v1.shape[1])], axis=1).astype(BF16)
    v2p = jnp.concatenate([v2, zeros(LANES - v2.shape[0], GROUP_WIDTH)], axis=0).astype(BF16)
    return w2p, a2p, v1p, v2p


def head_ones(width, head, scale):
    hd = np.arange(width) // head
    return jnp.asarray((hd[:, None] == hd[None, :]).astype(np.float32) * scale, BF16)


def rw_prep(h, mu, w0, w2p, a0, a2p, g2, k_k, k_a, r_k, vmix=None, tm=512):
    bsz, seq, wh = h.shape
    w = GROUP_WIDTH
    first = vmix is None
    row = lambda b, i: (b, i, 0)
    c2 = lambda b, i: (0, 0)
    vec = lambda t: t.reshape(1, -1)
    prev_map = lambda b, i: (b, jnp.maximum(i * (tm // SUBLANES) - 1, 0), 0)
    args = [h, h, vec(mu), vec(w0), w2p, vec(a0), a2p, g2.astype(BF16), vec(k_k), vec(k_a), vec(r_k),
            head_ones(w, RW_HEAD, 1.0)]
    specs = [pl.BlockSpec((1, tm, wh), row), pl.BlockSpec((1, SUBLANES, wh), prev_map),
             pl.BlockSpec((1, wh), c2), pl.BlockSpec((1, w), c2), pl.BlockSpec((LANES, w), c2),
             pl.BlockSpec((1, w), c2), pl.BlockSpec((LANES, w), c2), pl.BlockSpec((LANES, w), c2),
             pl.BlockSpec((1, w), c2), pl.BlockSpec((1, w), c2), pl.BlockSpec((1, w), c2),
             pl.BlockSpec((w, w), c2)]
    if not first:
        v_first, v0, v1p, v2p = vmix
        args += [v_first, vec(v0), v1p, v2p]
        specs += [pl.BlockSpec((1, tm, w), row), pl.BlockSpec((1, w), c2),
                  pl.BlockSpec((w, LANES), c2), pl.BlockSpec((LANES, w), c2)]
    tr = lambda b, i: (b, 0, i)
    t_shape = jax.ShapeDtypeStruct((bsz, w, seq), F32)
    n_shape = jax.ShapeDtypeStruct((bsz, seq, w), F32)
    return pl.pallas_call(
        functools.partial(_rw_prep_kernel, first=first),
        grid=(bsz, seq // tm),
        in_specs=specs,
        out_specs=[pl.BlockSpec((1, w, tm), tr)] * 5 + [pl.BlockSpec((1, tm, w), row)] * 3,
        out_shape=[t_shape] * 5 + [n_shape] * 3,
        compiler_params=_params("parallel", "parallel"),
        name="rwkv_prep",
    )(*args)


RW_UNROLL = 8


def _wkv_kernel(rT_ref, wT_ref, kT_ref, kkT_ref, bT_ref, v_ref, y_ref, st_ref, *, tc):
    @pl.when(pl.program_id(1) == 0)
    def _():
        st_ref[...] = jnp.zeros_like(st_ref)

    nvt = RW_HEAD // SUBLANES
    lane = lax.broadcasted_iota(jnp.int32, (SUBLANES, LANES), 1)

    def colsum(x):
        t = x[0]
        for i in range(1, nvt):
            t = t + x[i]
        for sh in (4, 2, 1):
            t = t + pltpu.roll(t, sh, axis=0)
        return t

    def group(g, states):
        base = pl.multiple_of(g * RW_UNROLL, RW_UNROLL)
        shift = tc - base
        vt = v_ref[0, pl.ds(base, RW_UNROLL), :]
        tabs = []
        for hd in range(RW_HEADS):
            rows = slice(hd * RW_HEAD, (hd + 1) * RW_HEAD)
            tabs.append([pltpu.roll(ref[0, rows, :], shift, axis=1)
                         for ref in (kkT_ref, wT_ref, bT_ref, kT_ref, rT_ref)])
        ys = [[] for _ in range(RW_HEADS)]
        states = list(states)
        for j in range(RW_UNROLL):
            for hd in range(RW_HEADS):
                col = lambda t: jnp.broadcast_to(t[:, j:j + 1], (RW_HEAD, LANES)).reshape(nvt, SUBLANES, LANES)
                kkc, wc, bc, kc, rc = (col(t) for t in tabs[hd])
                pair = hd // 2
                vrow = jnp.broadcast_to(vt[j:j + 1, pair * LANES:(pair + 1) * LANES], (SUBLANES, LANES))
                s = states[hd]
                sa = colsum(s * kkc)
                s = s * wc + (vrow[None] * kc - sa[None] * bc)
                states[hd] = s
                ys[hd].append(colsum(s * rc)[0:1, :])
        for pair in range(RW_HEADS // 2):
            y0 = jnp.concatenate(ys[2 * pair], axis=0)
            y1 = jnp.concatenate(ys[2 * pair + 1], axis=0)
            y_ref[0, pl.ds(base, RW_UNROLL), pair * LANES:(pair + 1) * LANES] = jnp.where(lane < RW_HEAD, y0, y1)
        return tuple(states)

    init = tuple(st_ref[hd].reshape(nvt, SUBLANES, LANES) for hd in range(RW_HEADS))
    final = lax.fori_loop(0, tc // RW_UNROLL, group, init)
    for hd in range(RW_HEADS):
        st_ref[hd] = final[hd].reshape(RW_HEAD, LANES)


def wkv7_scan(rT, wT, kT, kkT, bT, v, tc=128):
    bsz, w, seq = rT.shape
    tr = lambda b, i: (b, 0, i)
    row = lambda b, i: (b, i, 0)
    return pl.pallas_call(
        functools.partial(_wkv_kernel, tc=tc),
        grid=(bsz, seq // tc),
        in_specs=[pl.BlockSpec((1, w, tc), tr)] * 5 + [pl.BlockSpec((1, tc, w), row)],
        out_specs=pl.BlockSpec((1, tc, w), row),
        out_shape=jax.ShapeDtypeStruct((bsz, seq, w), F32),
        scratch_shapes=[pltpu.VMEM((RW_HEADS, RW_HEAD, LANES), F32)],
        compiler_params=_params("parallel", "arbitrary"),
        name="wkv7_scan",
    )(rT, wT, kT, kkT, bT, v)


def _outproj_kernel(x_ref, ys5_ref, yrw_ref, bonus_ref, grw_ref, ymla_ref, yret_ref, lng_ref, lnb_ref, ones_ref,
                    wout_ref, g1_ref, ng_ref, sc_ref, sh_ref, rhi_ref, rlo_ref, rb_ref,
                    xo_ref, hn_ref, lg_ref):
    y = yrw_ref[0]
    mean = _split_dot(y, ones_ref[...])
    yc = y - mean
    var = _split_dot(yc * yc, ones_ref[...])
    yrw = (yc * lax.rsqrt(var + RW_GN_EPS) * lng_ref[...] + lnb_ref[...] + bonus_ref[0]) * grw_ref[0]
    w = GROUP_WIDTH
    mixed = (_bdot(ys5_ref[0], wout_ref[0:w, :]) + _bdot(yrw, wout_ref[w:2 * w, :])
             + _bdot(ymla_ref[0], wout_ref[2 * w:3 * w, :]) + _bdot(yret_ref[0], wout_ref[3 * w:4 * w, :]))
    x = x_ref[0] + g1_ref[0] * mixed
    xo_ref[0] = x
    ms = jnp.mean(x * x, axis=-1, keepdims=True)
    hn = x * lax.rsqrt(ms + NORM_EPS) * ng_ref[...]
    hn = hn * (1.0 + sc_ref[0]) + sh_ref[0]
    hi = hn.astype(BF16)
    hn_ref[0] = hi
    lo = (hn - hi.astype(F32)).astype(BF16)
    lg_ref[0] = (jnp.dot(hi, rhi_ref[...], preferred_element_type=F32)
                 + jnp.dot(lo, rhi_ref[...], preferred_element_type=F32)
                 + jnp.dot(hi, rlo_ref[...], preferred_element_type=F32) + rb_ref[...])


def out_proj(x, y_s5, y_rw, bonus, g_rw, y_mla, y_ret, ln_g, ln_b, w_out, g1, norm_g, sc2, sh2,
             router_w, router_b, tm=512):
    bsz, seq, d = x.shape
    w = GROUP_WIDTH
    row = lambda b, i: (b, i, 0)
    per_b = lambda b, i: (b, 0, 0)
    c2 = lambda b, i: (0, 0)
    vec = lambda t: t.reshape(1, -1)
    pad = LANES - router_w.shape[1]
    rw_pad = jnp.concatenate([router_w, jnp.zeros((d, pad), F32)], axis=1)
    r_hi = rw_pad.astype(BF16)
    r_lo = (rw_pad - r_hi.astype(F32)).astype(BF16)
    rb = jnp.concatenate([router_b, jnp.zeros((pad,), F32)]).reshape(1, LANES)
    mixer = pl.BlockSpec((1, tm, w), row)
    return pl.pallas_call(
        _outproj_kernel,
        grid=(bsz, seq // tm),
        in_specs=[pl.BlockSpec((1, tm, d), row), mixer, mixer, mixer, mixer, mixer, mixer,
                  pl.BlockSpec((1, w), c2), pl.BlockSpec((1, w), c2), pl.BlockSpec((w, w), c2),
                  pl.BlockSpec((4 * w, d), c2),
                  pl.BlockSpec((1, 1, d), per_b), pl.BlockSpec((1, d), c2),
                  pl.BlockSpec((1, 1, d), per_b), pl.BlockSpec((1, 1, d), per_b),
                  pl.BlockSpec((d, LANES), c2), pl.BlockSpec((d, LANES), c2), pl.BlockSpec((1, LANES), c2)],
        out_specs=[pl.BlockSpec((1, tm, d), row), pl.BlockSpec((1, tm, d), row), pl.BlockSpec((1, tm, LANES), row)],
        out_shape=[jax.ShapeDtypeStruct((bsz, seq, d), F32), jax.ShapeDtypeStruct((bsz, seq, d), BF16),
                   jax.ShapeDtypeStruct((bsz, seq, LANES), F32)],
        compiler_params=_params("parallel", "parallel"),
        name="out_proj",
    )(x, y_s5, y_rw, bonus, g_rw, y_mla, y_ret, vec(ln_g), vec(ln_b), head_ones(w, RW_HEAD, 1.0 / RW_HEAD),
      w_out.astype(BF16), g1, vec(norm_g), sc2, sh2, r_hi, r_lo, rb)


MOE_ROWS = 512


def _moe_kernel(blk_e_ref, x_ref, wg_ref, bg_ref, wu_ref, bu_ref, wd_ref, bd_ref, o_ref):
    x = x_ref[...]
    gt = jnp.minimum(jnp.dot(x, wg_ref[0], preferred_element_type=F32) + bg_ref[0], SWIGLU_LIMIT)
    up = jnp.clip(jnp.dot(x, wu_ref[0], preferred_element_type=F32) + bu_ref[0], -SWIGLU_LIMIT, SWIGLU_LIMIT)
    act = gt * _sigmoid(SWIGLU_ALPHA * gt) * (up + 1.0)
    o_ref[...] = jnp.dot(act.astype(BF16), wd_ref[0], preferred_element_type=F32) + bd_ref[0]


def moe_experts(xb, blk_e, w_gate, b_gate, w_up, b_up, w_down, b_down):
    p_rows, d = xb.shape
    n_e, _, de = w_gate.shape
    wmap = lambda i, e: (e[i], 0, 0)
    return pl.pallas_call(
        _moe_kernel,
        grid_spec=pltpu.PrefetchScalarGridSpec(
            num_scalar_prefetch=1,
            grid=(p_rows // MOE_ROWS,),
            in_specs=[pl.BlockSpec((MOE_ROWS, d), lambda i, e: (i, 0)),
                      pl.BlockSpec((1, d, de), wmap), pl.BlockSpec((1, 1, de), wmap),
                      pl.BlockSpec((1, d, de), wmap), pl.BlockSpec((1, 1, de), wmap),
                      pl.BlockSpec((1, de, d), wmap), pl.BlockSpec((1, 1, d), wmap)],
            out_specs=pl.BlockSpec((MOE_ROWS, d), lambda i, e: (i, 0))),
        out_shape=jax.ShapeDtypeStruct((p_rows, d), F32),
        compiler_params=_params("arbitrary"),
        name="moe_experts",
    )(blk_e, xb, w_gate, b_gate.reshape(n_e, 1, de), w_up, b_up.reshape(n_e, 1, de),
      w_down, b_down.reshape(n_e, 1, d))


def moe_route(logits):
    t = logits.shape[0]
    top_val, top_idx = lax.top_k(logits, TOP_K)
    top_w = jax.nn.softmax(top_val, axis=-1)
    n_assign = t * TOP_K
    flat_e = top_idx.reshape(-1)
    order = jnp.argsort(flat_e)
    sorted_e = flat_e[order]
    counts = jnp.bincount(flat_e, length=N_EXPERTS)
    padded = (counts + MOE_ROWS - 1) // MOE_ROWS * MOE_ROWS
    pad_end = jnp.cumsum(padded)
    pad_start = pad_end - padded
    start = jnp.cumsum(counts) - counts
    dest = (pad_start[sorted_e] + jnp.arange(n_assign) - start[sorted_e]).astype(jnp.int32)
    p_rows = n_assign + N_EXPERTS * MOE_ROWS
    n_blocks = p_rows // MOE_ROWS
    buf_tok = jnp.full((p_rows,), t, jnp.int32).at[dest].set((order // TOP_K).astype(jnp.int32))
    pos = jnp.zeros((n_assign,), jnp.int32).at[order].set(dest)
    blk_e = jnp.minimum(jnp.searchsorted(pad_end, jnp.arange(n_blocks) * MOE_ROWS, side='right'),
                        N_EXPERTS - 1).astype(jnp.int32)
    return top_w, buf_tok, pos.reshape(t, TOP_K), blk_e


def moe_ffn(hn_bf16, logits, w_gate, b_gate, w_up, b_up, w_down, b_down):
    bsz, seq, d = hn_bf16.shape
    t = bsz * seq
    top_w, buf_tok, pos, blk_e = moe_route(logits.reshape(t, LANES)[:, :N_EXPERTS])
    x_pad = jnp.concatenate([hn_bf16.reshape(t, d), jnp.zeros((1, d), BF16)], axis=0)
    xb = jnp.take(x_pad, buf_tok, axis=0)
    yb = moe_experts(xb, blk_e, w_gate, b_gate, w_up, b_up, w_down, b_down)
    out = jnp.einsum('tkd,tk->td', jnp.take(yb, pos, axis=0), top_w)
    return out.reshape(bsz, seq, d)


def _residual_kernel(x_ref, y_ref, g_ref, o_ref):
    o_ref[0] = x_ref[0] + g_ref[0] * y_ref[0]


def _final_kernel(x_ref, y_ref, g_ref, ng_ref, o_ref):
    x = x_ref[0] + g_ref[0] * y_ref[0]
    o_ref[0] = x * lax.rsqrt(jnp.mean(x * x, axis=-1, keepdims=True) + NORM_EPS) * ng_ref[...]


def gated_residual(x, y, gate, final_g=None, tm=1024):
    bsz, seq, d = x.shape
    row = lambda b, i: (b, i, 0)
    specs = [pl.BlockSpec((1, tm, d), row), pl.BlockSpec((1, tm, d), row),
             pl.BlockSpec((1, 1, d), lambda b, i: (b, 0, 0))]
    args = [x, y, gate]
    body = _residual_kernel
    if final_g is not None:
        specs.append(pl.BlockSpec((1, d), lambda b, i: (0, 0)))
        args.append(final_g.reshape(1, d))
        body = _final_kernel
    return pl.pallas_call(
        body,
        grid=(bsz, seq // tm),
        in_specs=specs,
        out_specs=pl.BlockSpec((1, tm, d), row),
        out_shape=jax.ShapeDtypeStruct((bsz, seq, d), F32),
        compiler_params=_params("parallel", "parallel"),
        name="gated_residual",
    )(*args)


S5_CHUNK = 512


def kernel(x, c, positions, ada_w, ada_b, norm_mix_g, norm_ffn_g, w_in, w_out,
           s5_lambda_re, s5_lambda_im, s5_log_step, s5_b_re, s5_b_im, s5_c_re, s5_c_im,
           s5_d, s5_glu_w, s5_glu_b,
           rw_mu, rw_w0, rw_w2, rw_a0, rw_a2, rw_g2, rw_k_k, rw_k_a, rw_r_k, rw_ln_g, rw_ln_b,
           rw_v0, rw_v1, rw_v2,
           mla_q_norm_g, mla_kv_norm_g, mla_w_q_up, mla_w_kv_up,
           router_w, router_b, ex_w_gate, ex_b_gate, ex_w_up, ex_b_up, ex_w_down, ex_b_down,
           final_norm_g):
    depth = w_in.shape[0]
    mod = adaln_mod(c, ada_w, ada_b)
    cos_t, sin_t = rope_tables(positions)
    v_first = None
    for l in range(depth):
        sh1, sc1, g1, sh2, sc2, g2 = [m[:, None, :] for m in jnp.split(mod[l], N_MOD, axis=-1)]
        s5_u, rw_in, mla_in, ret_in = in_proj(x, sc1, sh1, norm_mix_g[l], pack_w_in(w_in[l]))

        prep = s5_prepare(s5_lambda_re[l], s5_lambda_im[l], s5_log_step[l], s5_b_re[l], s5_b_im[l],
                          s5_c_re[l], s5_c_im[l], S5_CHUNK // SUBLANES)
        y_s5 = s5_mixer(s5_u, prep, s5_d[l], s5_glu_w[l], s5_glu_b[l], chunk=S5_CHUNK)

        if l == 0:
            w2p, a2p = rw_pack_weights(rw_w2[l], rw_a2[l])
            vmix = None
        else:
            w2p, a2p, v1p, v2p = rw_pack_weights(rw_w2[l], rw_a2[l], rw_v1[l - 1], rw_v2[l - 1])
            vmix = (v_first, rw_v0[l - 1], v1p, v2p)
        rT, wT, kT, kkT, bT, v_rw, bonus, g_rw = rw_prep(
            rw_in, rw_mu[l], rw_w0[l], w2p, rw_a0[l], a2p, rw_g2[l], rw_k_k[l], rw_k_a[l],
            rw_r_k[l].reshape(-1), vmix)
        if l == 0:
            v_first = v_rw
        y_rw = wkv7_scan(rT, wT, kT, kkT, bT, v_rw)

        q, k, v = mla_prep(mla_in, cos_t, sin_t, mla_q_norm_g[l], mla_kv_norm_g[l],
                           mla_pack_weights(mla_w_q_up[l], mla_w_kv_up[l]))
        y_mla = causal_attention(q, k, v)

        y_ret = retention_mixer(ret_in, cos_t, sin_t)

        x, hn, logits = out_proj(x, y_s5, y_rw, bonus, g_rw, y_mla, y_ret, rw_ln_g[l], rw_ln_b[l], w_out[l],
                                 g1, norm_ffn_g[l], sc2, sh2, router_w[l], router_b[l])
        y_ffn = moe_ffn(hn, logits, ex_w_gate[l].astype(BF16), ex_b_gate[l], ex_w_up[l].astype(BF16), ex_b_up[l],
                        ex_w_down[l].astype(BF16), ex_b_down[l])
        x = gated_residual(x, y_ffn, g2, final_norm_g if l == depth - 1 else None)
    return x
```

```python
import functools
import math

import numpy as np
import jax
import jax.numpy as jnp
from jax import lax
from jax.experimental import pallas as pl
from jax.experimental.pallas import tpu as pltpu

F32 = jnp.float32
BF16 = jnp.bfloat16

D_MODEL = 1024
GROUP_WIDTH = 256
S5_CH = 16
S5_GROUPS = 16
S5_STATE = 64
S5_FLAT = S5_GROUPS * S5_STATE
RW_HEADS = 4
RW_HEAD = 64
RW_GN_EPS = 64e-5
MLA_HEADS = 4
MLA_NOPE = 64
MLA_ROPE = 32
MLA_V = 64
MLA_Q_RANK = 256
MLA_KV_RANK = 128
RET_HEADS = 4
RET_QK = 32
RET_V = 64
ROPE_BASE = 10000.0
N_EXPERTS = 32
TOP_K = 4
SWIGLU_ALPHA = 1.702
SWIGLU_LIMIT = 7.0
NORM_EPS = 1e-5
N_MOD = 6

LANES = 128
SUBLANES = 8
VMEM_LIMIT_BYTES = 56 * 1024 * 1024

IN_S5 = (0, 256)
IN_RW = (256, 1280)
IN_MLA = (1280, 1920)
IN_RET = (1920, 2944)
IN_PACKED = 2944


def _params(*sem):
    return pltpu.CompilerParams(dimension_semantics=sem, vmem_limit_bytes=VMEM_LIMIT_BYTES)


def _bdot(a, b):
    return jnp.dot(a.astype(BF16), b.astype(BF16), preferred_element_type=F32)


def _split_dot(a, b_bf16):
    hi = a.astype(BF16)
    lo = (a - hi.astype(F32)).astype(BF16)
    return (jnp.dot(hi, b_bf16, preferred_element_type=F32)
            + jnp.dot(lo, b_bf16, preferred_element_type=F32))


def _sigmoid(x):
    return 1.0 / (1.0 + jnp.exp(-x))


def _adaln_kernel(c_ref, w_ref, b_ref, o_ref):
    c = c_ref[...]
    cond = c * _sigmoid(c)
    o_ref[0] = _bdot(cond, w_ref[0]) + b_ref[0]


def adaln_mod(c, ada_w, ada_b):
    depth, d, n = ada_w.shape
    bsz = c.shape[0]
    tn = 1536
    return pl.pallas_call(
        _adaln_kernel,
        grid=(depth, n // tn),
        in_specs=[pl.BlockSpec((bsz, d), lambda l, j: (0, 0)),
                  pl.BlockSpec((1, d, tn), lambda l, j: (l, 0, j)),
                  pl.BlockSpec((1, 1, tn), lambda l, j: (l, 0, j))],
        out_specs=pl.BlockSpec((1, bsz, tn), lambda l, j: (l, 0, j)),
        out_shape=jax.ShapeDtypeStruct((depth, bsz, n), F32),
        compiler_params=_params("parallel", "parallel"),
        name="adaln_mod",
    )(c, ada_w, ada_b.reshape(depth, 1, n))


def _inproj_kernel(x_ref, sc_ref, sh_ref, g_ref, w_ref, s5_ref, rw_ref, mla_ref, ret_ref):
    x = x_ref[0]
    ms = jnp.mean(x * x, axis=-1, keepdims=True)
    hn = x * lax.rsqrt(ms + NORM_EPS) * g_ref[...]
    hn = hn * (1.0 + sc_ref[0]) + sh_ref[0]
    p = jnp.dot(hn.astype(BF16), w_ref[...], preferred_element_type=F32)
    s5_ref[0] = p[:, IN_S5[0]:IN_S5[1]]
    rw_ref[0] = p[:, IN_RW[0]:IN_RW[1]]
    mla_ref[0] = p[:, IN_MLA[0]:IN_MLA[1]]
    ret_ref[0] = p[:, IN_RET[0]:IN_RET[1]]


def in_proj(x, sc, sh, g, w_packed, tm=512):
    bsz, seq, d = x.shape
    widths = [b - a for a, b in (IN_S5, IN_RW, IN_MLA, IN_RET)]
    row = lambda b, i: (b, i, 0)
    per_b = lambda b, i: (b, 0, 0)
    return pl.pallas_call(
        _inproj_kernel,
        grid=(bsz, seq // tm),
        in_specs=[pl.BlockSpec((1, tm, d), row),
                  pl.BlockSpec((1, 1, d), per_b),
                  pl.BlockSpec((1, 1, d), per_b),
                  pl.BlockSpec((1, d), lambda b, i: (0, 0)),
                  pl.BlockSpec((d, IN_PACKED), lambda b, i: (0, 0))],
        out_specs=[pl.BlockSpec((1, tm, w), row) for w in widths],
        out_shape=[jax.ShapeDtypeStruct((bsz, seq, w), F32) for w in widths],
        compiler_params=_params("parallel", "parallel"),
        name="in_proj",
    )(x, sc, sh, g.reshape(1, d), w_packed)


def _swap_halves(cols, block):
    cols = np.asarray(cols).reshape(-1, 2, block // 2)
    return cols[:, ::-1, :].reshape(-1)


def pack_w_in(w_in_l):
    zero = w_in_l.shape[1]
    s5 = np.arange(0, 256)
    rw = np.arange(256, 1280)
    qc = np.arange(1280, 1536)
    kvc = np.arange(1536, 1664)
    kpe = np.arange(1664, 1696)
    z = lambda n: np.full((n,), zero)
    kpe_slot = np.concatenate([z(MLA_NOPE), kpe, z(LANES - MLA_NOPE - MLA_ROPE)])
    kpe_sw_slot = np.concatenate([z(MLA_NOPE), _swap_halves(kpe, MLA_ROPE), z(LANES - MLA_NOPE - MLA_ROPE)])
    rq = np.arange(1696, 1824)
    rk = np.arange(1824, 1952)
    rv = np.arange(1952, 2208)
    rg = np.arange(2208, 2464)
    idx = np.concatenate([s5, rw, qc, kvc, kpe_slot, kpe_sw_slot,
                          rq, rk, _swap_halves(rq, RET_QK), _swap_halves(rk, RET_QK), rv, rg])
    assert idx.shape[0] == IN_PACKED
    w_ext = jnp.concatenate([w_in_l, jnp.zeros((w_in_l.shape[0], 1), w_in_l.dtype)], axis=1)
    return jnp.take(w_ext, jnp.asarray(idx, jnp.int32), axis=1).astype(BF16)


def rope_tables(positions):
    inv = ROPE_BASE ** (-jnp.arange(0, MLA_ROPE, 2, dtype=F32) / MLA_ROPE)
    ang = positions.astype(F32)[..., None] * inv
    cos, sin = jnp.cos(ang), jnp.sin(ang)
    reps = LANES // MLA_ROPE
    cos_t = jnp.tile(jnp.concatenate([cos, cos], axis=-1), (1, 1, reps))
    sin_t = jnp.tile(jnp.concatenate([-sin, sin], axis=-1), (1, 1, reps))
    return cos_t, sin_t


def _s5_kernel(u_ref, bre_ref, bim_ref, cre_ref, cim_ref, lam_ref, lamq_ref, lamseg_ref,
               d_ref, gw_ref, gb_ref, o_ref, sre_ref, sim_ref, st_ref, *, nq):
    @pl.when(pl.program_id(1) == 0)
    def _():
        st_ref[...] = jnp.zeros_like(st_ref)

    u = u_ref[0]
    ub = u.astype(BF16)
    sre_ref[...] = jnp.dot(ub, bre_ref[...], preferred_element_type=F32)
    sim_ref[...] = jnp.dot(ub, bim_ref[...], preferred_element_type=F32)
    lam_re = lam_ref[0:1, :]
    lam_im = lam_ref[1:2, :]

    def scan_body(q, carry):
        cr, ci = carry
        rows = pl.ds(pl.multiple_of(q * SUBLANES, SUBLANES), SUBLANES)
        nr = lam_re * cr - lam_im * ci + sre_ref[rows, :]
        ni = lam_re * ci + lam_im * cr + sim_ref[rows, :]
        sre_ref[rows, :] = nr
        sim_ref[rows, :] = ni
        return nr, ni

    zero = jnp.zeros((SUBLANES, S5_FLAT), F32)
    end_re, end_im = lax.fori_loop(0, nq, scan_body, (zero, zero))

    seg_re = lamseg_ref[0:1, :]
    seg_im = lamseg_ref[1:2, :]
    cr, ci = st_ref[0:1, :], st_ref[1:2, :]
    in_re, in_im = [], []
    for r in range(SUBLANES):
        in_re.append(cr)
        in_im.append(ci)
        er, ei = end_re[r:r + 1, :], end_im[r:r + 1, :]
        cr, ci = seg_re * cr - seg_im * ci + er, seg_re * ci + seg_im * cr + ei
    st_ref[0:1, :] = cr
    st_ref[1:2, :] = ci
    car_re = jnp.concatenate(in_re, axis=0)
    car_im = jnp.concatenate(in_im, axis=0)

    def fix_body(q, _):
        rows = pl.ds(pl.multiple_of(q * SUBLANES, SUBLANES), SUBLANES)
        pr = lamq_ref[0, pl.ds(q, 1), :]
        pi = lamq_ref[1, pl.ds(q, 1), :]
        sre_ref[rows, :] = sre_ref[rows, :] + (pr * car_re - pi * car_im)
        sim_ref[rows, :] = sim_ref[rows, :] + (pr * car_im + pi * car_re)
        return 0

    lax.fori_loop(0, nq, fix_body, 0)

    y = (jnp.dot(sre_ref[...].astype(BF16), cre_ref[...], preferred_element_type=F32)
         - jnp.dot(sim_ref[...].astype(BF16), cim_ref[...], preferred_element_type=F32))
    y = y + d_ref[...] * u
    y = jax.nn.gelu(y)
    gate = jnp.dot(y.astype(BF16), gw_ref[...], preferred_element_type=F32) + gb_ref[...]
    o_ref[0] = y * _sigmoid(gate)


def s5_prepare(lam_re, lam_im, log_step, b_re, b_im, c_re, c_im, nq):
    dt = jnp.exp(log_step.astype(F32))[:, None]
    mag = jnp.exp(lam_re * dt)
    lb_re = mag * jnp.cos(lam_im * dt)
    lb_im = mag * jnp.sin(lam_im * dt)
    den = lam_re * lam_re + lam_im * lam_im
    n_re = lb_re - 1.0
    f_re = (n_re * lam_re + lb_im * lam_im) / den
    f_im = (lb_im * lam_re - n_re * lam_im) / den
    bb_re = f_re[..., None] * b_re - f_im[..., None] * b_im
    bb_im = f_re[..., None] * b_im + f_im[..., None] * b_re
    eye = jnp.eye(S5_GROUPS, dtype=F32)
    bd_in = lambda t: jnp.einsum('gph,gk->ghkp', t, eye).reshape(GROUP_WIDTH, S5_FLAT).astype(BF16)
    bd_out = lambda t: jnp.einsum('ghp,gk->gpkh', t, eye).reshape(S5_FLAT, GROUP_WIDTH).astype(BF16)
    lam = jnp.stack([lb_re.reshape(-1), lb_im.reshape(-1)])

    def power(n):
        n = n[:, None, None]
        m = jnp.exp(n * (lam_re * dt)[None])
        a = n * (lam_im * dt)[None]
        return jnp.stack([(m * jnp.cos(a)).reshape(-1, S5_FLAT), (m * jnp.sin(a)).reshape(-1, S5_FLAT)])

    lam_q = power(jnp.arange(1, nq + 1, dtype=F32))
    lam_seg = power(jnp.full((1,), float(nq), F32))[:, 0, :]
    return bd_in(bb_re), bd_in(bb_im), bd_out(c_re), bd_out(c_im), lam, lam_q, lam_seg


def s5_mixer(u, prep, d_skip, glu_w, glu_b, chunk=512):
    bsz, seq, w = u.shape
    nq = chunk // SUBLANES
    nchunk = seq // chunk
    bre, bim, cre, cim, lam, lam_q, lam_seg = prep
    up = u.reshape(bsz, nchunk, SUBLANES, nq, w).transpose(0, 1, 3, 2, 4).reshape(bsz, seq, w)
    const2 = lambda b, i: (0, 0)
    out = pl.pallas_call(
        functools.partial(_s5_kernel, nq=nq),
        grid=(bsz, nchunk),
        in_specs=[pl.BlockSpec((1, chunk, w), lambda b, i: (b, i, 0)),
                  pl.BlockSpec((w, S5_FLAT), const2),
                  pl.BlockSpec((w, S5_FLAT), const2),
                  pl.BlockSpec((S5_FLAT, w), const2),
                  pl.BlockSpec((S5_FLAT, w), const2),
                  pl.BlockSpec((2, S5_FLAT), const2),
                  pl.BlockSpec((2, nq, S5_FLAT), lambda b, i: (0, 0, 0)),
                  pl.BlockSpec((2, S5_FLAT), const2),
                  pl.BlockSpec((1, w), const2),
                  pl.BlockSpec((w, w), const2),
                  pl.BlockSpec((1, w), const2)],
        out_specs=pl.BlockSpec((1, chunk, w), lambda b, i: (b, i, 0)),
        out_shape=jax.ShapeDtypeStruct((bsz, seq, w), F32),
        scratch_shapes=[pltpu.VMEM((chunk, S5_FLAT), F32),
                        pltpu.VMEM((chunk, S5_FLAT), F32),
                        pltpu.VMEM((2, S5_FLAT), F32)],
        compiler_params=_params("parallel", "arbitrary"),
        name="s5_mixer",
    )(up, bre, bim, cre, cim, lam, lam_q, lam_seg,
      d_skip.reshape(1, w), glu_w.astype(BF16), glu_b.reshape(1, w))
    return out.reshape(bsz, nchunk, nq, SUBLANES, w).transpose(0, 1, 3, 2, 4).reshape(bsz, seq, w)


def _ret_kernel(h_ref, cos_ref, sin_ref, intra_ref, qw_ref, kw_ref, dec_ref, ones_ref,
                o_ref, st_ref, *, chunk):
    @pl.when(pl.program_id(1) == 0)
    def _():
        st_ref[...] = jnp.zeros_like(st_ref)

    h = h_ref[0]
    cos = cos_ref[0]
    sin = sin_ref[0]
    nqk = RET_HEADS * RET_QK
    q = h[:, 0:nqk] * cos + h[:, 2 * nqk:3 * nqk] * sin
    k = (h[:, nqk:2 * nqk] * cos + h[:, 3 * nqk:4 * nqk] * sin) * (RET_QK ** -0.5)
    v = h[:, 4 * nqk:4 * nqk + GROUP_WIDTH]
    g = h[:, 4 * nqk + GROUP_WIDTH:]
    kb = k.astype(BF16)
    lane_qk = lax.broadcasted_iota(jnp.int32, (chunk, nqk), 1) // RET_QK
    lane_v = lax.broadcasted_iota(jnp.int32, (chunk, GROUP_WIDTH), 1) // RET_V
    state = st_ref[...]
    o = _bdot(q * qw_ref[...], state)
    for hd in range(RET_HEADS):
        qh = jnp.where(lane_qk == hd, q, 0.0).astype(BF16)
        s = lax.dot_general(qh, kb, (((1,), (1,)), ((), ())), preferred_element_type=F32)
        s = s * intra_ref[hd]
        vh = jnp.where(lane_v == hd, v, 0.0).astype(BF16)
        o = o + jnp.dot(s.astype(BF16), vh, preferred_element_type=F32)
    kv = lax.dot_general((k * kw_ref[...]).astype(BF16), v.astype(BF16),
                         (((0,), (0,)), ((), ())), preferred_element_type=F32)
    dec = dec_ref[...]
    st_ref[...] = state * dec + jnp.where(dec > 0.0, kv, 0.0)
    ms = _split_dot(o * o, ones_ref[...])
    o = o * lax.rsqrt(ms + NORM_EPS)
    o_ref[0] = o * (g * _sigmoid(g))


def retention_tables(chunk):
    log_gamma = np.log1p(-np.exp2(-5.0 - np.arange(RET_HEADS, dtype=np.float64)))
    idx = np.arange(chunk, dtype=np.float64)
    diff = idx[:, None] - idx[None, :]
    intra = np.where(diff >= 0, np.exp(np.maximum(diff, 0.0)[None] * log_gamma[:, None, None]), 0.0)
    q_w = np.repeat(np.exp((idx + 1.0)[:, None] * log_gamma[None, :]), RET_QK, axis=1)
    k_w = np.repeat(np.exp((chunk - 1.0 - idx)[:, None] * log_gamma[None, :]), RET_QK, axis=1)
    head_q = np.arange(RET_HEADS * RET_QK) // RET_QK
    head_v = np.arange(GROUP_WIDTH) // RET_V
    same = head_q[:, None] == head_v[None, :]
    dec = np.where(same, np.exp(chunk * log_gamma)[head_q][:, None], 0.0)
    ones = (head_v[:, None] == head_v[None, :]).astype(np.float64) / RET_V
    f = lambda a: jnp.asarray(a, F32)
    return f(intra), f(q_w), f(k_w), f(dec), jnp.asarray(ones, BF16)


def retention_mixer(h, cos_t, sin_t, chunk=256):
    bsz, seq, wh = h.shape
    intra, q_w, k_w, dec, ones = retention_tables(chunk)
    nqk = RET_HEADS * RET_QK
    row = lambda b, i: (b, i, 0)
    c2 = lambda b, i: (0, 0)
    return pl.pallas_call(
        functools.partial(_ret_kernel, chunk=chunk),
        grid=(bsz, seq // chunk),
        in_specs=[pl.BlockSpec((1, chunk, wh), row),
                  pl.BlockSpec((1, chunk, LANES), row),
                  pl.BlockSpec((1, chunk, LANES), row),
                  pl.BlockSpec((RET_HEADS, chunk, chunk), lambda b, i: (0, 0, 0)),
                  pl.BlockSpec((chunk, nqk), c2),
                  pl.BlockSpec((chunk, nqk), c2),
                  pl.BlockSpec((nqk, GROUP_WIDTH), c2),
                  pl.BlockSpec((GROUP_WIDTH, GROUP_WIDTH), c2)],
        out_specs=pl.BlockSpec((1, chunk, GROUP_WIDTH), row),
        out_shape=jax.ShapeDtypeStruct((bsz, seq, GROUP_WIDTH), F32),
        scratch_shapes=[pltpu.VMEM((nqk, GROUP_WIDTH), F32)],
        compiler_params=_params("parallel", "arbitrary"),
        name="retention",
    )(h, cos_t, sin_t, intra, q_w, k_w, dec, ones)


def _mla_prep_kernel(h_ref, cos_ref, sin_ref, qg_ref, kvg_ref, wqa_ref, wqb_ref, wk_ref, wv_ref,
                     q_ref, k_ref, v_ref, *, scale):
    h = h_ref[0]
    tm = h.shape[0]
    lane = lax.broadcasted_iota(jnp.int32, (tm, LANES), 1)
    is_nope = lane < MLA_NOPE
    is_rope = jnp.logical_and(lane >= MLA_NOPE, lane < MLA_NOPE + MLA_ROPE)
    cm = jnp.where(is_nope, 1.0, jnp.where(is_rope, cos_ref[0], 0.0))
    sm = jnp.where(is_rope, sin_ref[0], 0.0)

    qc = h[:, 0:MLA_Q_RANK]
    qn = (qc * lax.rsqrt(jnp.mean(qc * qc, axis=-1, keepdims=True) + NORM_EPS) * qg_ref[...]).astype(BF16)
    kvc = h[:, MLA_Q_RANK:MLA_Q_RANK + MLA_KV_RANK]
    kvn = (kvc * lax.rsqrt(jnp.mean(kvc * kvc, axis=-1, keepdims=True) + NORM_EPS) * kvg_ref[...]).astype(BF16)
    off = MLA_Q_RANK + MLA_KV_RANK
    kpe = h[:, off:off + LANES] * cm + h[:, off + LANES:off + 2 * LANES] * sm

    qa = jnp.dot(qn, wqa_ref[...], preferred_element_type=F32)
    qb = jnp.dot(qn, wqb_ref[...], preferred_element_type=F32)
    kn = jnp.dot(kvn, wk_ref[...], preferred_element_type=F32)
    v_ref[0] = jnp.dot(kvn, wv_ref[...], preferred_element_type=F32).astype(BF16)
    for hd in range(MLA_HEADS):
        sl = slice(hd * LANES, (hd + 1) * LANES)
        q_ref[0, hd] = ((qa[:, sl] * cm + qb[:, sl] * sm) * scale).astype(BF16)
        k_ref[0, hd] = (kn[:, sl] + kpe).astype(BF16)


def mla_pack_weights(w_q_up, w_kv_up):
    dq = MLA_NOPE + MLA_ROPE
    zq = w_q_up.shape[1]
    zk = w_kv_up.shape[1]
    z = lambda n, zero: np.full((n,), zero)
    ia, ib, ik, iv = [], [], [], []
    for hd in range(MLA_HEADS):
        nope = np.arange(hd * dq, hd * dq + MLA_NOPE)
        pe = np.arange(hd * dq + MLA_NOPE, (hd + 1) * dq)
        pad = LANES - dq
        ia += [nope, pe, z(pad, zq)]
        ib += [z(MLA_NOPE, zq), _swap_halves(pe, MLA_ROPE), z(pad, zq)]
        kv0 = hd * (MLA_NOPE + MLA_V)
        ik += [np.arange(kv0, kv0 + MLA_NOPE), z(LANES - MLA_NOPE, zk)]
        iv += [np.arange(kv0 + MLA_NOPE, kv0 + MLA_NOPE + MLA_V)]
    ext = lambda w: jnp.concatenate([w, jnp.zeros((w.shape[0], 1), w.dtype)], axis=1)
    take = lambda w, idx: jnp.take(ext(w), jnp.asarray(np.concatenate(idx), jnp.int32), axis=1).astype(BF16)
    return take(w_q_up, ia), take(w_q_up, ib), take(w_kv_up, ik), take(w_kv_up, iv)


def mla_prep(h, cos_t, sin_t, q_norm_g, kv_norm_g, packed, tm=512):
    bsz, seq, wh = h.shape
    wqa, wqb, wk, wv = packed
    row = lambda b, i: (b, i, 0)
    c2 = lambda b, i: (0, 0)
    hrow = lambda b, i: (b, 0, i, 0)
    scale = (MLA_NOPE + MLA_ROPE) ** -0.5
    return pl.pallas_call(
        functools.partial(_mla_prep_kernel, scale=scale),
        grid=(bsz, seq // tm),
        in_specs=[pl.BlockSpec((1, tm, wh), row),
                  pl.BlockSpec((1, tm, LANES), row),
                  pl.BlockSpec((1, tm, LANES), row),
                  pl.BlockSpec((1, MLA_Q_RANK), c2),
                  pl.BlockSpec((1, MLA_KV_RANK), c2),
                  pl.BlockSpec(wqa.shape, c2),
                  pl.BlockSpec(wqb.shape, c2),
                  pl.BlockSpec(wk.shape, c2),
                  pl.BlockSpec(wv.shape, c2)],
        out_specs=[pl.BlockSpec((1, MLA_HEADS, tm, LANES), hrow),
                   pl.BlockSpec((1, MLA_HEADS, tm, LANES), hrow),
                   pl.BlockSpec((1, tm, GROUP_WIDTH), row)],
        out_shape=[jax.ShapeDtypeStruct((bsz, MLA_HEADS, seq, LANES), BF16),
                   jax.ShapeDtypeStruct((bsz, MLA_HEADS, seq, LANES), BF16),
                   jax.ShapeDtypeStruct((bsz, seq, GROUP_WIDTH), BF16)],
        compiler_params=_params("parallel", "parallel"),
        name="mla_prep",
    )(h, cos_t, sin_t, q_norm_g.reshape(1, -1), kv_norm_g.reshape(1, -1), wqa, wqb, wk, wv)


def _attn_kernel(q_ref, k_ref, v_ref, o_ref, m_ref, l_ref, acc_ref, *, blk):
    qi = pl.program_id(2)
    rep = blk // LANES

    def step(hh, j, masked):
        q = q_ref[0, hh]
        rows = pl.ds(pl.multiple_of(j * blk, blk), blk)
        ks = k_ref[0, hh, rows, :]
        vs = v_ref[0, rows, :]
        s = lax.dot_general(q, ks, (((1,), (1,)), ((), ())), preferred_element_type=F32)
        if masked:
            r = lax.broadcasted_iota(jnp.int32, (blk, blk), 0)
            c = lax.broadcasted_iota(jnp.int32, (blk, blk), 1)
            s = jnp.where(c <= r, s, -jnp.inf)
        m_prev = m_ref[hh]
        m_new = jnp.maximum(m_prev, jnp.max(s, axis=-1, keepdims=True))
        p = jnp.exp(s - pltpu.repeat(m_new, rep, axis=1))
        alpha = jnp.exp(m_prev - m_new)
        l_ref[hh] = alpha * l_ref[hh] + jnp.sum(p, axis=-1, keepdims=True)
        acc_ref[hh] = alpha * acc_ref[hh] + jnp.dot(p.astype(BF16), vs, preferred_element_type=F32)
        m_ref[hh] = m_new

    for hh in range(2):
        m_ref[hh] = jnp.full((blk, LANES), -jnp.inf, F32)
        l_ref[hh] = jnp.zeros((blk, LANES), F32)
        acc_ref[hh] = jnp.zeros((blk, LANES), F32)

        def body(j, _, hh=hh):
            step(hh, j, False)
            return 0

        lax.fori_loop(0, qi, body, 0)
        step(hh, qi, True)

    lane = lax.broadcasted_iota(jnp.int32, (blk, LANES), 1)
    o0 = acc_ref[0] / l_ref[0]
    o1 = acc_ref[1] / l_ref[1]
    o_ref[0] = jnp.where(lane < MLA_V, o0, o1)


def causal_attention(q, k, v, blk=512):
    bsz, nh, seq, dk = q.shape
    return pl.pallas_call(
        functools.partial(_attn_kernel, blk=blk),
        grid=(bsz, nh // 2, seq // blk),
        in_specs=[pl.BlockSpec((1, 2, blk, dk), lambda b, p, i: (b, p, i, 0)),
                  pl.BlockSpec((1, 2, seq, dk), lambda b, p, i: (b, p, 0, 0)),
                  pl.BlockSpec((1, seq, LANES), lambda b, p, i: (b, 0, p))],
        out_specs=pl.BlockSpec((1, blk, LANES), lambda b, p, i: (b, i, p)),
        out_shape=jax.ShapeDtypeStruct((bsz, seq, nh // 2 * LANES), F32),
        scratch_shapes=[pltpu.VMEM((2, blk, LANES), F32)] * 3,
        compiler_params=_params("parallel", "parallel", "arbitrary"),
        name="mla_attention",
    )(q, k, v)


def _rw_prep_kernel(*refs, first):
    if first:
        (h_ref, prev_ref, mu_ref, w0_ref, w2_ref, a0_ref, a2_ref, g2_ref, kk_ref, ka_ref, rk_ref, ones_ref,
         rT_ref, wT_ref, kT_ref, kkT_ref, bT_ref, v_ref, bonus_ref, g_ref) = refs
    else:
        (h_ref, prev_ref, mu_ref, w0_ref, w2_ref, a0_ref, a2_ref, g2_ref, kk_ref, ka_ref, rk_ref, ones_ref,
         vf_ref, v0_ref, v1_ref, v2_ref,
         rT_ref, wT_ref, kT_ref, kkT_ref, bT_ref, v_ref, bonus_ref, g_ref) = refs
    h = h_ref[0]
    tm = h.shape[0]
    last = prev_ref[0, SUBLANES - 1:SUBLANES, :]
    last = jnp.where(pl.program_id(1) == 0, 0.0, last)
    row = lax.broadcasted_iota(jnp.int32, h.shape, 0)
    h_prev = jnp.where(row == 0, last, pltpu.roll(h, 1, axis=0))
    h = h + (h_prev - h) * mu_ref[...]
    w = GROUP_WIDTH
    r = h[:, 0:w]
    k = h[:, w:2 * w]
    v = h[:, 2 * w:3 * w]
    wa = h[:, 3 * w:3 * w + LANES]
    gd = h[:, 3 * w + LANES:]
    z = w0_ref[...] + _bdot(jnp.tanh(wa), w2_ref[...])
    nz = -z
    softplus = jnp.maximum(nz, 0.0) + jnp.log(1.0 + jnp.exp(-jnp.abs(nz)))
    decay = jnp.exp(-jnp.exp(-softplus - 0.5))
    a = _sigmoid(a0_ref[...] + _bdot(wa, a2_ref[...]))
    g_ref[0] = _bdot(_sigmoid(gd), g2_ref[...])
    if not first:
        mix = _sigmoid(v0_ref[...] + _bdot(_bdot(v, v1_ref[...]), v2_ref[...]))
        v = v + (vf_ref[0] - v) * mix
    kk = k * kk_ref[...]
    norm = jnp.sqrt(_split_dot(kk * kk, ones_ref[...]))
    kk = kk / jnp.maximum(norm, 1e-12)
    k = k * (1.0 + (a - 1.0) * ka_ref[...])
    v_ref[0] = v
    bonus_ref[0] = _split_dot(r * k * rk_ref[...], ones_ref[...]) * v
    rT_ref[0] = r.T
    wT_ref[0] = decay.T
    kT_ref[0] = k.T
    kkT_ref[0] = kk.T
    bT_ref[0] = (kk * a).T


def rw_pack_weights(w2, a2, v1=None, v2=None):
    zeros = lambda n, m: jnp.zeros((n, m), F32)
    half = LANES // 2
    w2p = jnp.concatenate([w2, zeros(half, GROUP_WIDTH)], axis=0).astype(BF16)
    a2p = jnp.concatenate([zeros(half, GROUP_WIDTH), a2], axis=0).astype(BF16)
    if v1 is None:
        return w2p, a2p
    v1p = jnp.concatenate([v1, zeros(GROUP_WIDTH, LANES - v1.shape[1])], axis=1).astype(BF16)
    v2p = jnp.concatenate([v2, zeros(LANES - v2.shape[0], GROUP_WIDTH)], axis=0).astype(BF16)
    return w2p, a2p, v1p, v2p


def head_ones(width, head, scale):
    hd = np.arange(width) // head
    return jnp.asarray((hd[:, None] == hd[None, :]).astype(np.float32) * scale, BF16)


def rw_prep(h, mu, w0, w2p, a0, a2p, g2, k_k, k_a, r_k, vmix=None, tm=512):
    bsz, seq, wh = h.shape
    w = GROUP_WIDTH
    first = vmix is None
    row = lambda b, i: (b, i, 0)
    c2 = lambda b, i: (0, 0)
    vec = lambda t: t.reshape(1, -1)
    prev_map = lambda b, i: (b, jnp.maximum(i * (tm // SUBLANES) - 1, 0), 0)
    args = [h, h, vec(mu), vec(w0), w2p, vec(a0), a2p, g2.astype(BF16), vec(k_k), vec(k_a), vec(r_k),
            head_ones(w, RW_HEAD, 1.0)]
    specs = [pl.BlockSpec((1, tm, wh), row), pl.BlockSpec((1, SUBLANES, wh), prev_map),
             pl.BlockSpec((1, wh), c2), pl.BlockSpec((1, w), c2), pl.BlockSpec((LANES, w), c2),
             pl.BlockSpec((1, w), c2), pl.BlockSpec((LANES, w), c2), pl.BlockSpec((LANES, w), c2),
             pl.BlockSpec((1, w), c2), pl.BlockSpec((1, w), c2), pl.BlockSpec((1, w), c2),
             pl.BlockSpec((w, w), c2)]
    if not first:
        v_first, v0, v1p, v2p = vmix
        args += [v_first, vec(v0), v1p, v2p]
        specs += [pl.BlockSpec((1, tm, w), row), pl.BlockSpec((1, w), c2),
                  pl.BlockSpec((w, LANES), c2), pl.BlockSpec((LANES, w), c2)]
    tr = lambda b, i: (b, 0, i)
    t_shape = jax.ShapeDtypeStruct((bsz, w, seq), F32)
    n_shape = jax.ShapeDtypeStruct((bsz, seq, w), F32)
    return pl.pallas_call(
        functools.partial(_rw_prep_kernel, first=first),
        grid=(bsz, seq // tm),
        in_specs=specs,
        out_specs=[pl.BlockSpec((1, w, tm), tr)] * 5 + [pl.BlockSpec((1, tm, w), row)] * 3,
        out_shape=[t_shape] * 5 + [n_shape] * 3,
        compiler_params=_params("parallel", "parallel"),
        name="rwkv_prep",
    )(*args)


RW_UNROLL = 8


def _wkv_kernel(rT_ref, wT_ref, kT_ref, kkT_ref, bT_ref, v_ref, y_ref, st_ref, *, tc):
    @pl.when(pl.program_id(1) == 0)
    def _():
        st_ref[...] = jnp.zeros_like(st_ref)

    nvt = RW_HEAD // SUBLANES
    lane = lax.broadcasted_iota(jnp.int32, (SUBLANES, LANES), 1)

    def colsum(x):
        t = x[0]
        for i in range(1, nvt):
            t = t + x[i]
        for sh in (4, 2, 1):
            t = t + pltpu.roll(t, sh, axis=0)
        return t

    def group(g, states):
        base = pl.multiple_of(g * RW_UNROLL, RW_UNROLL)
        shift = tc - base
        vt = v_ref[0, pl.ds(base, RW_UNROLL), :]
        tabs = []
        for hd in range(RW_HEADS):
            rows = slice(hd * RW_HEAD, (hd + 1) * RW_HEAD)
            tabs.append([pltpu.roll(ref[0, rows, :], shift, axis=1)
                         for ref in (kkT_ref, wT_ref, bT_ref, kT_ref, rT_ref)])
        ys = [[] for _ in range(RW_HEADS)]
        states = list(states)
        for j in range(RW_UNROLL):
            for hd in range(RW_HEADS):
                col = lambda t: jnp.broadcast_to(t[:, j:j + 1], (RW_HEAD, LANES)).reshape(nvt, SUBLANES, LANES)
                kkc, wc, bc, kc, rc = (col(t) for t in tabs[hd])
                pair = hd // 2
                vrow = jnp.broadcast_to(vt[j:j + 1, pair * LANES:(pair + 1) * LANES], (SUBLANES, LANES))
                s = states[hd]
                sa = colsum(s * kkc)
                s = s * wc + (vrow[None] * kc - sa[None] * bc)
                states[hd] = s
                ys[hd].append(colsum(s * rc)[0:1, :])
        for pair in range(RW_HEADS // 2):
            y0 = jnp.concatenate(ys[2 * pair], axis=0)
            y1 = jnp.concatenate(ys[2 * pair + 1], axis=0)
            y_ref[0, pl.ds(base, RW_UNROLL), pair * LANES:(pair + 1) * LANES] = jnp.where(lane < RW_HEAD, y0, y1)
        return tuple(states)

    init = tuple(st_ref[hd].reshape(nvt, SUBLANES, LANES) for hd in range(RW_HEADS))
    final = lax.fori_loop(0, tc // RW_UNROLL, group, init)
    for hd in range(RW_HEADS):
        st_ref[hd] = final[hd].reshape(RW_HEAD, LANES)


def wkv7_scan(rT, wT, kT, kkT, bT, v, tc=128):
    bsz, w, seq = rT.shape
    tr = lambda b, i: (b, 0, i)
    row = lambda b, i: (b, i, 0)
    return pl.pallas_call(
        functools.partial(_wkv_kernel, tc=tc),
        grid=(bsz, seq // tc),
        in_specs=[pl.BlockSpec((1, w, tc), tr)] * 5 + [pl.BlockSpec((1, tc, w), row)],
        out_specs=pl.BlockSpec((1, tc, w), row),
        out_shape=jax.ShapeDtypeStruct((bsz, seq, w), F32),
        scratch_shapes=[pltpu.VMEM((RW_HEADS, RW_HEAD, LANES), F32)],
        compiler_params=_params("parallel", "arbitrary"),
        name="wkv7_scan",
    )(rT, wT, kT, kkT, bT, v)


def _outproj_kernel(x_ref, ys5_ref, yrw_ref, bonus_ref, grw_ref, ymla_ref, yret_ref, lng_ref, lnb_ref, ones_ref,
                    wout_ref, g1_ref, ng_ref, sc_ref, sh_ref, rhi_ref, rlo_ref, rb_ref,
                    xo_ref, hn_ref, lg_ref):
    y = yrw_ref[0]
    mean = _split_dot(y, ones_ref[...])
    yc = y - mean
    var = _split_dot(yc * yc, ones_ref[...])
    yrw = (yc * lax.rsqrt(var + RW_GN_EPS) * lng_ref[...] + lnb_ref[...] + bonus_ref[0]) * grw_ref[0]
    w = GROUP_WIDTH
    mixed = (_bdot(ys5_ref[0], wout_ref[0:w, :]) + _bdot(yrw, wout_ref[w:2 * w, :])
             + _bdot(ymla_ref[0], wout_ref[2 * w:3 * w, :]) + _bdot(yret_ref[0], wout_ref[3 * w:4 * w, :]))
    x = x_ref[0] + g1_ref[0] * mixed
    xo_ref[0] = x
    ms = jnp.mean(x * x, axis=-1, keepdims=True)
    hn = x * lax.rsqrt(ms + NORM_EPS) * ng_ref[...]
    hn = hn * (1.0 + sc_ref[0]) + sh_ref[0]
    hi = hn.astype(BF16)
    hn_ref[0] = hi
    lo = (hn - hi.astype(F32)).astype(BF16)
    lg_ref[0] = (jnp.dot(hi, rhi_ref[...], preferred_element_type=F32)
                 + jnp.dot(lo, rhi_ref[...], preferred_element_type=F32)
                 + jnp.dot(hi, rlo_ref[...], preferred_element_type=F32) + rb_ref[...])


def out_proj(x, y_s5, y_rw, bonus, g_rw, y_mla, y_ret, ln_g, ln_b, w_out, g1, norm_g, sc2, sh2,
             router_w, router_b, tm=512):
    bsz, seq, d = x.shape
    w = GROUP_WIDTH
    row = lambda b, i: (b, i, 0)
    per_b = lambda b, i: (b, 0, 0)
    c2 = lambda b, i: (0, 0)
    vec = lambda t: t.reshape(1, -1)
    pad = LANES - router_w.shape[1]
    rw_pad = jnp.concatenate([router_w, jnp.zeros((d, pad), F32)], axis=1)
    r_hi = rw_pad.astype(BF16)
    r_lo = (rw_pad - r_hi.astype(F32)).astype(BF16)
    rb = jnp.concatenate([router_b, jnp.zeros((pad,), F32)]).reshape(1, LANES)
    mixer = pl.BlockSpec((1, tm, w), row)
    return pl.pallas_call(
        _outproj_kernel,
        grid=(bsz, seq // tm),
        in_specs=[pl.BlockSpec((1, tm, d), row), mixer, mixer, mixer, mixer, mixer, mixer,
                  pl.BlockSpec((1, w), c2), pl.BlockSpec((1, w), c2), pl.BlockSpec((w, w), c2),
                  pl.BlockSpec((4 * w, d), c2),
                  pl.BlockSpec((1, 1, d), per_b), pl.BlockSpec((1, d), c2),
                  pl.BlockSpec((1, 1, d), per_b), pl.BlockSpec((1, 1, d), per_b),
                  pl.BlockSpec((d, LANES), c2), pl.BlockSpec((d, LANES), c2), pl.BlockSpec((1, LANES), c2)],
        out_specs=[pl.BlockSpec((1, tm, d), row), pl.BlockSpec((1, tm, d), row), pl.BlockSpec((1, tm, LANES), row)],
        out_shape=[jax.ShapeDtypeStruct((bsz, seq, d), F32), jax.ShapeDtypeStruct((bsz, seq, d), BF16),
                   jax.ShapeDtypeStruct((bsz, seq, LANES), F32)],
        compiler_params=_params("parallel", "parallel"),
        name="out_proj",
    )(x, y_s5, y_rw, bonus, g_rw, y_mla, y_ret, vec(ln_g), vec(ln_b), head_ones(w, RW_HEAD, 1.0 / RW_HEAD),
      w_out.astype(BF16), g1, vec(norm_g), sc2, sh2, r_hi, r_lo, rb)


MOE_ROWS = 512


def _moe_kernel(blk_e_ref, blk_on_ref, x_ref, wg_ref, bg_ref, wu_ref, bu_ref, wd_ref, bd_ref, o_ref, wb_ref):
    i = pl.program_id(0)
    changed = jnp.logical_or(i == 0, blk_e_ref[i] != blk_e_ref[jnp.maximum(i - 1, 0)])

    @pl.when(changed)
    def _():
        wb_ref[0] = wg_ref[0].astype(BF16)
        wb_ref[1] = wu_ref[0].astype(BF16)
        wb_ref[2] = wd_ref[0].astype(BF16)

    @pl.when(blk_on_ref[i] > 0)
    def _():
        x = x_ref[...]
        gt = jnp.minimum(jnp.dot(x, wb_ref[0], preferred_element_type=F32) + bg_ref[0], SWIGLU_LIMIT)
        up = jnp.clip(jnp.dot(x, wb_ref[1], preferred_element_type=F32) + bu_ref[0], -SWIGLU_LIMIT, SWIGLU_LIMIT)
        act = gt * _sigmoid(SWIGLU_ALPHA * gt) * (up + 1.0)
        o_ref[...] = jnp.dot(act.astype(BF16), wb_ref[2], preferred_element_type=F32) + bd_ref[0]


def moe_experts(xb, blk_e, blk_on, w_gate, b_gate, w_up, b_up, w_down, b_down):
    p_rows, d = xb.shape
    n_e, _, de = w_gate.shape
    wmap = lambda i, e, on: (e[i], 0, 0)
    rows = lambda i, e, on: (i, 0)
    return pl.pallas_call(
        _moe_kernel,
        grid_spec=pltpu.PrefetchScalarGridSpec(
            num_scalar_prefetch=2,
            grid=(p_rows // MOE_ROWS,),
            in_specs=[pl.BlockSpec((MOE_ROWS, d), rows),
                      pl.BlockSpec((1, d, de), wmap), pl.BlockSpec((1, 1, de), wmap),
                      pl.BlockSpec((1, d, de), wmap), pl.BlockSpec((1, 1, de), wmap),
                      pl.BlockSpec((1, de, d), wmap), pl.BlockSpec((1, 1, d), wmap)],
            out_specs=pl.BlockSpec((MOE_ROWS, d), rows),
            scratch_shapes=[pltpu.VMEM((3, d, de), BF16)]),
        out_shape=jax.ShapeDtypeStruct((p_rows, d), F32),
        compiler_params=_params("arbitrary"),
        name="moe_experts",
    )(blk_e, blk_on, xb, w_gate, b_gate.reshape(n_e, 1, de), w_up, b_up.reshape(n_e, 1, de),
      w_down, b_down.reshape(n_e, 1, d))


def moe_route(logits):
    t = logits.shape[0]
    top_val, top_idx = lax.top_k(logits, TOP_K)
    top_w = jax.nn.softmax(top_val, axis=-1)
    n_assign = t * TOP_K
    flat_e = top_idx.reshape(-1).astype(jnp.int32)
    iota = jnp.arange(n_assign, dtype=jnp.int32)
    sorted_e, order = lax.sort((flat_e, iota), num_keys=1)
    _, rank = lax.sort((order, iota), num_keys=1)
    experts = jnp.arange(N_EXPERTS, dtype=jnp.int32)
    start = jnp.sum(sorted_e[None, :] < experts[:, None], axis=1, dtype=jnp.int32)
    counts = jnp.sum(sorted_e[None, :] == experts[:, None], axis=1, dtype=jnp.int32)
    padded = (counts + MOE_ROWS - 1) // MOE_ROWS * MOE_ROWS
    pad_end = jnp.cumsum(padded)
    pad_start = pad_end - padded
    pos = rank + (pad_start - start)[flat_e]
    p_rows = n_assign + N_EXPERTS * MOE_ROWS
    n_blocks = p_rows // MOE_ROWS
    blk_first = jnp.arange(n_blocks, dtype=jnp.int32) * MOE_ROWS
    blk_e = jnp.minimum(jnp.sum(pad_end[None, :] <= blk_first[:, None], axis=1, dtype=jnp.int32), N_EXPERTS - 1)
    blk_on = (blk_first < pad_start[blk_e] + counts[blk_e]).astype(jnp.int32)
    row = jnp.arange(p_rows, dtype=jnp.int32)
    row_e = jnp.repeat(blk_e, MOE_ROWS)
    within = row - pad_start[row_e]
    src = jnp.clip(start[row_e] + within, 0, n_assign - 1)
    buf_tok = jnp.where(within < counts[row_e], order[src] // TOP_K, 0)
    return top_w, buf_tok, pos.reshape(t, TOP_K), blk_e, blk_on


def moe_ffn(hn_bf16, logits, w_gate, b_gate, w_up, b_up, w_down, b_down):
    bsz, seq, d = hn_bf16.shape
    t = bsz * seq
    top_w, buf_tok, pos, blk_e, blk_on = moe_route(logits.reshape(t, LANES)[:, :N_EXPERTS])
    xb = hn_bf16.reshape(t, d).at[buf_tok].get(mode='promise_in_bounds')
    yb = moe_experts(xb, blk_e, blk_on, w_gate, b_gate, w_up, b_up, w_down, b_down)
    yg = yb.at[pos.T.reshape(-1)].get(mode='promise_in_bounds')
    return yg.reshape(TOP_K, bsz, seq, d), top_w.reshape(bsz, seq, TOP_K)


def _combine_kernel(*refs, final):
    if final:
        x_ref, y_ref, w_ref, g_ref, ng_ref, o_ref = refs
    else:
        x_ref, y_ref, w_ref, g_ref, o_ref = refs
    w = w_ref[0]
    y = y_ref[0, 0] * w[:, 0:1]
    for j in range(1, TOP_K):
        y = y + y_ref[j, 0] * w[:, j:j + 1]
    x = x_ref[0] + g_ref[0] * y
    if final:
        x = x * lax.rsqrt(jnp.mean(x * x, axis=-1, keepdims=True) + NORM_EPS) * ng_ref[...]
    o_ref[0] = x


def moe_combine(x, yg, top_w, gate, final_g=None, tm=512):
    bsz, seq, d = x.shape
    row = lambda b, i: (b, i, 0)
    specs = [pl.BlockSpec((1, tm, d), row),
             pl.BlockSpec((TOP_K, 1, tm, d), lambda b, i: (0, b, i, 0)),
             pl.BlockSpec((1, tm, TOP_K), row),
             pl.BlockSpec((1, 1, d), lambda b, i: (b, 0, 0))]
    args = [x, yg, top_w, gate]
    if final_g is not None:
        specs.append(pl.BlockSpec((1, d), lambda b, i: (0, 0)))
        args.append(final_g.reshape(1, d))
    return pl.pallas_call(
        functools.partial(_combine_kernel, final=final_g is not None),
        grid=(bsz, seq // tm),
        in_specs=specs,
        out_specs=pl.BlockSpec((1, tm, d), row),
        out_shape=jax.ShapeDtypeStruct((bsz, seq, d), F32),
        compiler_params=_params("parallel", "parallel"),
        name="moe_combine",
    )(*args)


S5_CHUNK = 512


def kernel(x, c, positions, ada_w, ada_b, norm_mix_g, norm_ffn_g, w_in, w_out,
           s5_lambda_re, s5_lambda_im, s5_log_step, s5_b_re, s5_b_im, s5_c_re, s5_c_im,
           s5_d, s5_glu_w, s5_glu_b,
           rw_mu, rw_w0, rw_w2, rw_a0, rw_a2, rw_g2, rw_k_k, rw_k_a, rw_r_k, rw_ln_g, rw_ln_b,
           rw_v0, rw_v1, rw_v2,
           mla_q_norm_g, mla_kv_norm_g, mla_w_q_up, mla_w_kv_up,
           router_w, router_b, ex_w_gate, ex_b_gate, ex_w_up, ex_b_up, ex_w_down, ex_b_down,
           final_norm_g):
    depth = w_in.shape[0]
    mod = adaln_mod(c, ada_w, ada_b)
    cos_t, sin_t = rope_tables(positions)
    v_first = None
    for l in range(depth):
        sh1, sc1, g1, sh2, sc2, g2 = [m[:, None, :] for m in jnp.split(mod[l], N_MOD, axis=-1)]
        s5_u, rw_in, mla_in, ret_in = in_proj(x, sc1, sh1, norm_mix_g[l], pack_w_in(w_in[l]))

        prep = s5_prepare(s5_lambda_re[l], s5_lambda_im[l], s5_log_step[l], s5_b_re[l], s5_b_im[l],
                          s5_c_re[l], s5_c_im[l], S5_CHUNK // SUBLANES)
        y_s5 = s5_mixer(s5_u, prep, s5_d[l], s5_glu_w[l], s5_glu_b[l], chunk=S5_CHUNK)

        if l == 0:
            w2p, a2p = rw_pack_weights(rw_w2[l], rw_a2[l])
            vmix = None
        else:
            w2p, a2p, v1p, v2p = rw_pack_weights(rw_w2[l], rw_a2[l], rw_v1[l - 1], rw_v2[l - 1])
            vmix = (v_first, rw_v0[l - 1], v1p, v2p)
        rT, wT, kT, kkT, bT, v_rw, bonus, g_rw = rw_prep(
            rw_in, rw_mu[l], rw_w0[l], w2p, rw_a0[l], a2p, rw_g2[l], rw_k_k[l], rw_k_a[l],
            rw_r_k[l].reshape(-1), vmix)
        if l == 0:
            v_first = v_rw
        y_rw = wkv7_scan(rT, wT, kT, kkT, bT, v_rw)

        q, k, v = mla_prep(mla_in, cos_t, sin_t, mla_q_norm_g[l], mla_kv_norm_g[l],
                           mla_pack_weights(mla_w_q_up[l], mla_w_kv_up[l]))
        y_mla = causal_attention(q, k, v)

        y_ret = retention_mixer(ret_in, cos_t, sin_t)

        x, hn, logits = out_proj(x, y_s5, y_rw, bonus, g_rw, y_mla, y_ret, rw_ln_g[l], rw_ln_b[l], w_out[l],
                                 g1, norm_ffn_g[l], sc2, sh2, router_w[l], router_b[l])
        yg, top_w = moe_ffn(hn, logits, ex_w_gate[l], ex_b_gate[l], ex_w_up[l], ex_b_up[l],
                            ex_w_down[l], ex_b_down[l])
        x = moe_combine(x, yg, top_w, g2, final_norm_g if l == depth - 1 else None)
    return x
```

```python
import functools
import math

import numpy as np
import jax
import jax.numpy as jnp
from jax import lax
from jax.experimental import pallas as pl
from jax.experimental.pallas import tpu as pltpu

F32 = jnp.float32
BF16 = jnp.bfloat16

D_MODEL = 1024
GROUP_WIDTH = 256
S5_CH = 16
S5_GROUPS = 16
S5_STATE = 64
S5_FLAT = S5_GROUPS * S5_STATE
RW_HEADS = 4
RW_HEAD = 64
RW_GN_EPS = 64e-5
MLA_HEADS = 4
MLA_NOPE = 64
MLA_ROPE = 32
MLA_V = 64
MLA_Q_RANK = 256
MLA_KV_RANK = 128
RET_HEADS = 4
RET_QK = 32
RET_V = 64
ROPE_BASE = 10000.0
N_EXPERTS = 32
TOP_K = 4
SWIGLU_ALPHA = 1.702
SWIGLU_LIMIT = 7.0
NORM_EPS = 1e-5
N_MOD = 6

LANES = 128
SUBLANES = 8
VMEM_LIMIT_BYTES = 56 * 1024 * 1024

IN_S5 = (0, 256)
IN_RW = (256, 1280)
IN_MLA = (1280, 1920)
IN_RET = (1920, 2944)
IN_PACKED = 2944


def _params(*sem):
    return pltpu.CompilerParams(dimension_semantics=sem, vmem_limit_bytes=VMEM_LIMIT_BYTES)


def _bdot(a, b):
    return jnp.dot(a.astype(BF16), b.astype(BF16), preferred_element_type=F32)


def _split_dot(a, b_bf16):
    hi = a.astype(BF16)
    lo = (a - hi.astype(F32)).astype(BF16)
    return (jnp.dot(hi, b_bf16, preferred_element_type=F32)
            + jnp.dot(lo, b_bf16, preferred_element_type=F32))


def _sigmoid(x):
    return 1.0 / (1.0 + jnp.exp(-x))


def _adaln_kernel(c_ref, w_ref, b_ref, o_ref):
    c = c_ref[...]
    cond = c * _sigmoid(c)
    o_ref[0] = _bdot(cond, w_ref[0]) + b_ref[0]


def adaln_mod(c, ada_w, ada_b):
    depth, d, n = ada_w.shape
    bsz = c.shape[0]
    tn = 1536
    return pl.pallas_call(
        _adaln_kernel,
        grid=(depth, n // tn),
        in_specs=[pl.BlockSpec((bsz, d), lambda l, j: (0, 0)),
                  pl.BlockSpec((1, d, tn), lambda l, j: (l, 0, j)),
                  pl.BlockSpec((1, 1, tn), lambda l, j: (l, 0, j))],
        out_specs=pl.BlockSpec((1, bsz, tn), lambda l, j: (l, 0, j)),
        out_shape=jax.ShapeDtypeStruct((depth, bsz, n), F32),
        compiler_params=_params("parallel", "parallel"),
        name="adaln_mod",
    )(c, ada_w, ada_b.reshape(depth, 1, n))


def _inproj_kernel(x_ref, sc_ref, sh_ref, g_ref, w_ref, s5_ref, rw_ref, mla_ref, ret_ref):
    x = x_ref[0]
    ms = jnp.mean(x * x, axis=-1, keepdims=True)
    hn = x * lax.rsqrt(ms + NORM_EPS) * g_ref[...]
    hn = hn * (1.0 + sc_ref[0]) + sh_ref[0]
    p = jnp.dot(hn.astype(BF16), w_ref[...], preferred_element_type=F32)
    s5_ref[0] = p[:, IN_S5[0]:IN_S5[1]]
    rw_ref[0] = p[:, IN_RW[0]:IN_RW[1]]
    mla_ref[0] = p[:, IN_MLA[0]:IN_MLA[1]]
    ret_ref[0] = p[:, IN_RET[0]:IN_RET[1]]


def in_proj(x, sc, sh, g, w_packed, tm=512):
    bsz, seq, d = x.shape
    widths = [b - a for a, b in (IN_S5, IN_RW, IN_MLA, IN_RET)]
    row = lambda b, i: (b, i, 0)
    per_b = lambda b, i: (b, 0, 0)
    return pl.pallas_call(
        _inproj_kernel,
        grid=(bsz, seq // tm),
        in_specs=[pl.BlockSpec((1, tm, d), row),
                  pl.BlockSpec((1, 1, d), per_b),
                  pl.BlockSpec((1, 1, d), per_b),
                  pl.BlockSpec((1, d), lambda b, i: (0, 0)),
                  pl.BlockSpec((d, IN_PACKED), lambda b, i: (0, 0))],
        out_specs=[pl.BlockSpec((1, tm, w), row) for w in widths],
        out_shape=[jax.ShapeDtypeStruct((bsz, seq, w), F32) for w in widths],
        compiler_params=_params("parallel", "parallel"),
        name="in_proj",
    )(x, sc, sh, g.reshape(1, d), w_packed)


def _swap_halves(cols, block):
    cols = np.asarray(cols).reshape(-1, 2, block // 2)
    return cols[:, ::-1, :].reshape(-1)


def pack_w_in(w_in_l):
    zero = w_in_l.shape[1]
    s5 = np.arange(0, 256)
    rw = np.arange(256, 1280)
    qc = np.arange(1280, 1536)
    kvc = np.arange(1536, 1664)
    kpe = np.arange(1664, 1696)
    z = lambda n: np.full((n,), zero)
    kpe_slot = np.concatenate([z(MLA_NOPE), kpe, z(LANES - MLA_NOPE - MLA_ROPE)])
    kpe_sw_slot = np.concatenate([z(MLA_NOPE), _swap_halves(kpe, MLA_ROPE), z(LANES - MLA_NOPE - MLA_ROPE)])
    rq = np.arange(1696, 1824)
    rk = np.arange(1824, 1952)
    rv = np.arange(1952, 2208)
    rg = np.arange(2208, 2464)
    idx = np.concatenate([s5, rw, qc, kvc, kpe_slot, kpe_sw_slot,
                          rq, rk, _swap_halves(rq, RET_QK), _swap_halves(rk, RET_QK), rv, rg])
    assert idx.shape[0] == IN_PACKED
    w_ext = jnp.concatenate([w_in_l, jnp.zeros((w_in_l.shape[0], 1), w_in_l.dtype)], axis=1)
    return jnp.take(w_ext, jnp.asarray(idx, jnp.int32), axis=1).astype(BF16)


def rope_tables(positions):
    inv = ROPE_BASE ** (-jnp.arange(0, MLA_ROPE, 2, dtype=F32) / MLA_ROPE)
    ang = positions.astype(F32)[..., None] * inv
    cos, sin = jnp.cos(ang), jnp.sin(ang)
    reps = LANES // MLA_ROPE
    cos_t = jnp.tile(jnp.concatenate([cos, cos], axis=-1), (1, 1, reps))
    sin_t = jnp.tile(jnp.concatenate([-sin, sin], axis=-1), (1, 1, reps))
    return cos_t, sin_t


def _s5_kernel(u_ref, bre_ref, bim_ref, cre_ref, cim_ref, lam_ref, lamq_ref, lamseg_ref,
               d_ref, gw_ref, gb_ref, o_ref, sre_ref, sim_ref, st_ref, *, nq):
    @pl.when(pl.program_id(1) == 0)
    def _():
        st_ref[...] = jnp.zeros_like(st_ref)

    u = u_ref[0]
    ub = u.astype(BF16)
    sre_ref[...] = jnp.dot(ub, bre_ref[...], preferred_element_type=F32)
    sim_ref[...] = jnp.dot(ub, bim_ref[...], preferred_element_type=F32)
    lam_re = lam_ref[0:1, :]
    lam_im = lam_ref[1:2, :]

    def scan_body(q, carry):
        cr, ci = carry
        rows = pl.ds(pl.multiple_of(q * SUBLANES, SUBLANES), SUBLANES)
        nr = lam_re * cr - lam_im * ci + sre_ref[rows, :]
        ni = lam_re * ci + lam_im * cr + sim_ref[rows, :]
        sre_ref[rows, :] = nr
        sim_ref[rows, :] = ni
        return nr, ni

    zero = jnp.zeros((SUBLANES, S5_FLAT), F32)
    end_re, end_im = lax.fori_loop(0, nq, scan_body, (zero, zero))

    seg_re = lamseg_ref[0:1, :]
    seg_im = lamseg_ref[1:2, :]
    cr, ci = st_ref[0:1, :], st_ref[1:2, :]
    in_re, in_im = [], []
    for r in range(SUBLANES):
        in_re.append(cr)
        in_im.append(ci)
        er, ei = end_re[r:r + 1, :], end_im[r:r + 1, :]
        cr, ci = seg_re * cr - seg_im * ci + er, seg_re * ci + seg_im * cr + ei
    st_ref[0:1, :] = cr
    st_ref[1:2, :] = ci
    car_re = jnp.concatenate(in_re, axis=0)
    car_im = jnp.concatenate(in_im, axis=0)

    def fix_body(q, _):
        rows = pl.ds(pl.multiple_of(q * SUBLANES, SUBLANES), SUBLANES)
        pr = lamq_ref[0, pl.ds(q, 1), :]
        pi = lamq_ref[1, pl.ds(q, 1), :]
        sre_ref[rows, :] = sre_ref[rows, :] + (pr * car_re - pi * car_im)
        sim_ref[rows, :] = sim_ref[rows, :] + (pr * car_im + pi * car_re)
        return 0

    lax.fori_loop(0, nq, fix_body, 0)

    y = (jnp.dot(sre_ref[...].astype(BF16), cre_ref[...], preferred_element_type=F32)
         - jnp.dot(sim_ref[...].astype(BF16), cim_ref[...], preferred_element_type=F32))
    y = y + d_ref[...] * u
    y = jax.nn.gelu(y)
    gate = jnp.dot(y.astype(BF16), gw_ref[...], preferred_element_type=F32) + gb_ref[...]
    o_ref[0] = y * _sigmoid(gate)


def s5_prepare(lam_re, lam_im, log_step, b_re, b_im, c_re, c_im, nq):
    dt = jnp.exp(log_step.astype(F32))[:, None]
    mag = jnp.exp(lam_re * dt)
    lb_re = mag * jnp.cos(lam_im * dt)
    lb_im = mag * jnp.sin(lam_im * dt)
    den = lam_re * lam_re + lam_im * lam_im
    n_re = lb_re - 1.0
    f_re = (n_re * lam_re + lb_im * lam_im) / den
    f_im = (lb_im * lam_re - n_re * lam_im) / den
    bb_re = f_re[..., None] * b_re - f_im[..., None] * b_im
    bb_im = f_re[..., None] * b_im + f_im[..., None] * b_re
    eye = jnp.eye(S5_GROUPS, dtype=F32)
    bd_in = lambda t: jnp.einsum('gph,gk->ghkp', t, eye).reshape(GROUP_WIDTH, S5_FLAT).astype(BF16)
    bd_out = lambda t: jnp.einsum('ghp,gk->gpkh', t, eye).reshape(S5_FLAT, GROUP_WIDTH).astype(BF16)
    lam = jnp.stack([lb_re.reshape(-1), lb_im.reshape(-1)])

    def power(n):
        n = n[:, None, None]
        m = jnp.exp(n * (lam_re * dt)[None])
        a = n * (lam_im * dt)[None]
        return jnp.stack([(m * jnp.cos(a)).reshape(-1, S5_FLAT), (m * jnp.sin(a)).reshape(-1, S5_FLAT)])

    lam_q = power(jnp.arange(1, nq + 1, dtype=F32))
    lam_seg = power(jnp.full((1,), float(nq), F32))[:, 0, :]
    return bd_in(bb_re), bd_in(bb_im), bd_out(c_re), bd_out(c_im), lam, lam_q, lam_seg


def s5_mixer(u, prep, d_skip, glu_w, glu_b, chunk=512):
    bsz, seq, w = u.shape
    nq = chunk // SUBLANES
    nchunk = seq // chunk
    bre, bim, cre, cim, lam, lam_q, lam_seg = prep
    up = u.reshape(bsz, nchunk, SUBLANES, nq, w).transpose(0, 1, 3, 2, 4).reshape(bsz, seq, w)
    const2 = lambda b, i: (0, 0)
    out = pl.pallas_call(
        functools.partial(_s5_kernel, nq=nq),
        grid=(bsz, nchunk),
        in_specs=[pl.BlockSpec((1, chunk, w), lambda b, i: (b, i, 0)),
                  pl.BlockSpec((w, S5_FLAT), const2),
                  pl.BlockSpec((w, S5_FLAT), const2),
                  pl.BlockSpec((S5_FLAT, w), const2),
                  pl.BlockSpec((S5_FLAT, w), const2),
                  pl.BlockSpec((2, S5_FLAT), const2),
                  pl.BlockSpec((2, nq, S5_FLAT), lambda b, i: (0, 0, 0)),
                  pl.BlockSpec((2, S5_FLAT), const2),
                  pl.BlockSpec((1, w), const2),
                  pl.BlockSpec((w, w), const2),
                  pl.BlockSpec((1, w), const2)],
        out_specs=pl.BlockSpec((1, chunk, w), lambda b, i: (b, i, 0)),
        out_shape=jax.ShapeDtypeStruct((bsz, seq, w), F32),
        scratch_shapes=[pltpu.VMEM((chunk, S5_FLAT), F32),
                        pltpu.VMEM((chunk, S5_FLAT), F32),
                        pltpu.VMEM((2, S5_FLAT), F32)],
        compiler_params=_params("parallel", "arbitrary"),
        name="s5_mixer",
    )(up, bre, bim, cre, cim, lam, lam_q, lam_seg,
      d_skip.reshape(1, w), glu_w.astype(BF16), glu_b.reshape(1, w))
    return out.reshape(bsz, nchunk, nq, SUBLANES, w).transpose(0, 1, 3, 2, 4).reshape(bsz, seq, w)


def _ret_kernel(h_ref, cos_ref, sin_ref, intra_ref, qw_ref, kw_ref, dec_ref, ones_ref,
                o_ref, st_ref, *, chunk):
    @pl.when(pl.program_id(1) == 0)
    def _():
        st_ref[...] = jnp.zeros_like(st_ref)

    h = h_ref[0]
    cos = cos_ref[0]
    sin = sin_ref[0]
    nqk = RET_HEADS * RET_QK
    q = h[:, 0:nqk] * cos + h[:, 2 * nqk:3 * nqk] * sin
    k = (h[:, nqk:2 * nqk] * cos + h[:, 3 * nqk:4 * nqk] * sin) * (RET_QK ** -0.5)
    v = h[:, 4 * nqk:4 * nqk + GROUP_WIDTH]
    g = h[:, 4 * nqk + GROUP_WIDTH:]
    kb = k.astype(BF16)
    lane_qk = lax.broadcasted_iota(jnp.int32, (chunk, nqk), 1) // RET_QK
    lane_v = lax.broadcasted_iota(jnp.int32, (chunk, GROUP_WIDTH), 1) // RET_V
    state = st_ref[...]
    o = _bdot(q * qw_ref[...], state)
    for hd in range(RET_HEADS):
        qh = jnp.where(lane_qk == hd, q, 0.0).astype(BF16)
        s = lax.dot_general(qh, kb, (((1,), (1,)), ((), ())), preferred_element_type=F32)
        s = s * intra_ref[hd]
        vh = jnp.where(lane_v == hd, v, 0.0).astype(BF16)
        o = o + jnp.dot(s.astype(BF16), vh, preferred_element_type=F32)
    kv = lax.dot_general((k * kw_ref[...]).astype(BF16), v.astype(BF16),
                         (((0,), (0,)), ((), ())), preferred_element_type=F32)
    dec = dec_ref[...]
    st_ref[...] = state * dec + jnp.where(dec > 0.0, kv, 0.0)
    ms = _split_dot(o * o, ones_ref[...])
    o = o * lax.rsqrt(ms + NORM_EPS)
    o_ref[0] = o * (g * _sigmoid(g))


def retention_tables(chunk):
    log_gamma = np.log1p(-np.exp2(-5.0 - np.arange(RET_HEADS, dtype=np.float64)))
    idx = np.arange(chunk, dtype=np.float64)
    diff = idx[:, None] - idx[None, :]
    intra = np.where(diff >= 0, np.exp(np.maximum(diff, 0.0)[None] * log_gamma[:, None, None]), 0.0)
    q_w = np.repeat(np.exp((idx + 1.0)[:, None] * log_gamma[None, :]), RET_QK, axis=1)
    k_w = np.repeat(np.exp((chunk - 1.0 - idx)[:, None] * log_gamma[None, :]), RET_QK, axis=1)
    head_q = np.arange(RET_HEADS * RET_QK) // RET_QK
    head_v = np.arange(GROUP_WIDTH) // RET_V
    same = head_q[:, None] == head_v[None, :]
    dec = np.where(same, np.exp(chunk * log_gamma)[head_q][:, None], 0.0)
    ones = (head_v[:, None] == head_v[None, :]).astype(np.float64) / RET_V
    f = lambda a: jnp.asarray(a, F32)
    return f(intra), f(q_w), f(k_w), f(dec), jnp.asarray(ones, BF16)


def retention_mixer(h, cos_t, sin_t, chunk=256):
    bsz, seq, wh = h.shape
    intra, q_w, k_w, dec, ones = retention_tables(chunk)
    nqk = RET_HEADS * RET_QK
    row = lambda b, i: (b, i, 0)
    c2 = lambda b, i: (0, 0)
    return pl.pallas_call(
        functools.partial(_ret_kernel, chunk=chunk),
        grid=(bsz, seq // chunk),
        in_specs=[pl.BlockSpec((1, chunk, wh), row),
                  pl.BlockSpec((1, chunk, LANES), row),
                  pl.BlockSpec((1, chunk, LANES), row),
                  pl.BlockSpec((RET_HEADS, chunk, chunk), lambda b, i: (0, 0, 0)),
                  pl.BlockSpec((chunk, nqk), c2),
                  pl.BlockSpec((chunk, nqk), c2),
                  pl.BlockSpec((nqk, GROUP_WIDTH), c2),
                  pl.BlockSpec((GROUP_WIDTH, GROUP_WIDTH), c2)],
        out_specs=pl.BlockSpec((1, chunk, GROUP_WIDTH), row),
        out_shape=jax.ShapeDtypeStruct((bsz, seq, GROUP_WIDTH), F32),
        scratch_shapes=[pltpu.VMEM((nqk, GROUP_WIDTH), F32)],
        compiler_params=_params("parallel", "arbitrary"),
        name="retention",
    )(h, cos_t, sin_t, intra, q_w, k_w, dec, ones)


def _mla_prep_kernel(h_ref, cos_ref, sin_ref, qg_ref, kvg_ref, wqa_ref, wqb_ref, wk_ref, wv_ref,
                     q_ref, k_ref, v_ref, *, scale):
    h = h_ref[0]
    tm = h.shape[0]
    lane = lax.broadcasted_iota(jnp.int32, (tm, LANES), 1)
    is_nope = lane < MLA_NOPE
    is_rope = jnp.logical_and(lane >= MLA_NOPE, lane < MLA_NOPE + MLA_ROPE)
    cm = jnp.where(is_nope, 1.0, jnp.where(is_rope, cos_ref[0], 0.0))
    sm = jnp.where(is_rope, sin_ref[0], 0.0)

    qc = h[:, 0:MLA_Q_RANK]
    qn = (qc * lax.rsqrt(jnp.mean(qc * qc, axis=-1, keepdims=True) + NORM_EPS) * qg_ref[...]).astype(BF16)
    kvc = h[:, MLA_Q_RANK:MLA_Q_RANK + MLA_KV_RANK]
    kvn = (kvc * lax.rsqrt(jnp.mean(kvc * kvc, axis=-1, keepdims=True) + NORM_EPS) * kvg_ref[...]).astype(BF16)
    off = MLA_Q_RANK + MLA_KV_RANK
    kpe = h[:, off:off + LANES] * cm + h[:, off + LANES:off + 2 * LANES] * sm

    qa = jnp.dot(qn, wqa_ref[...], preferred_element_type=F32)
    qb = jnp.dot(qn, wqb_ref[...], preferred_element_type=F32)
    kn = jnp.dot(kvn, wk_ref[...], preferred_element_type=F32)
    v_ref[0] = jnp.dot(kvn, wv_ref[...], preferred_element_type=F32).astype(BF16)
    for hd in range(MLA_HEADS):
        sl = slice(hd * LANES, (hd + 1) * LANES)
        q_ref[0, hd] = ((qa[:, sl] * cm + qb[:, sl] * sm) * scale).astype(BF16)
        k_ref[0, hd] = (kn[:, sl] + kpe).astype(BF16)


def mla_pack_weights(w_q_up, w_kv_up):
    dq = MLA_NOPE + MLA_ROPE
    zq = w_q_up.shape[1]
    zk = w_kv_up.shape[1]
    z = lambda n, zero: np.full((n,), zero)
    ia, ib, ik, iv = [], [], [], []
    for hd in range(MLA_HEADS):
        nope = np.arange(hd * dq, hd * dq + MLA_NOPE)
        pe = np.arange(hd * dq + MLA_NOPE, (hd + 1) * dq)
        pad = LANES - dq
        ia += [nope, pe, z(pad, zq)]
        ib += [z(MLA_NOPE, zq), _swap_halves(pe, MLA_ROPE), z(pad, zq)]
        kv0 = hd * (MLA_NOPE + MLA_V)
        ik += [np.arange(kv0, kv0 + MLA_NOPE), z(LANES - MLA_NOPE, zk)]
        iv += [np.arange(kv0 + MLA_NOPE, kv0 + MLA_NOPE + MLA_V)]
    ext = lambda w: jnp.concatenate([w, jnp.zeros((w.shape[0], 1), w.dtype)], axis=1)
    take = lambda w, idx: jnp.take(ext(w), jnp.asarray(np.concatenate(idx), jnp.int32), axis=1).astype(BF16)
    return take(w_q_up, ia), take(w_q_up, ib), take(w_kv_up, ik), take(w_kv_up, iv)


def mla_prep(h, cos_t, sin_t, q_norm_g, kv_norm_g, packed, tm=512):
    bsz, seq, wh = h.shape
    wqa, wqb, wk, wv = packed
    row = lambda b, i: (b, i, 0)
    c2 = lambda b, i: (0, 0)
    hrow = lambda b, i: (b, 0, i, 0)
    scale = (MLA_NOPE + MLA_ROPE) ** -0.5
    return pl.pallas_call(
        functools.partial(_mla_prep_kernel, scale=scale),
        grid=(bsz, seq // tm),
        in_specs=[pl.BlockSpec((1, tm, wh), row),
                  pl.BlockSpec((1, tm, LANES), row),
                  pl.BlockSpec((1, tm, LANES), row),
                  pl.BlockSpec((1, MLA_Q_RANK), c2),
                  pl.BlockSpec((1, MLA_KV_RANK), c2),
                  pl.BlockSpec(wqa.shape, c2),
                  pl.BlockSpec(wqb.shape, c2),
                  pl.BlockSpec(wk.shape, c2),
                  pl.BlockSpec(wv.shape, c2)],
        out_specs=[pl.BlockSpec((1, MLA_HEADS, tm, LANES), hrow),
                   pl.BlockSpec((1, MLA_HEADS, tm, LANES), hrow),
                   pl.BlockSpec((1, tm, GROUP_WIDTH), row)],
        out_shape=[jax.ShapeDtypeStruct((bsz, MLA_HEADS, seq, LANES), BF16),
                   jax.ShapeDtypeStruct((bsz, MLA_HEADS, seq, LANES), BF16),
                   jax.ShapeDtypeStruct((bsz, seq, GROUP_WIDTH), BF16)],
        compiler_params=_params("parallel", "parallel"),
        name="mla_prep",
    )(h, cos_t, sin_t, q_norm_g.reshape(1, -1), kv_norm_g.reshape(1, -1), wqa, wqb, wk, wv)


def _attn_kernel(q_ref, k_ref, v_ref, o_ref, m_ref, l_ref, acc_ref, *, blk):
    qi = pl.program_id(2)
    rep = blk // LANES

    heads = range(2)

    def step(j, masked):
        rows = pl.ds(pl.multiple_of(j * blk, blk), blk)
        vs = v_ref[0, rows, :]
        s = [lax.dot_general(q_ref[0, hh], k_ref[0, hh, rows, :], (((1,), (1,)), ((), ())),
                             preferred_element_type=F32) for hh in heads]
        if masked:
            r = lax.broadcasted_iota(jnp.int32, (blk, blk), 0)
            c = lax.broadcasted_iota(jnp.int32, (blk, blk), 1)
            s = [jnp.where(c <= r, t, -jnp.inf) for t in s]
        m_prev = [m_ref[hh] for hh in heads]
        m_new = [jnp.maximum(m_prev[hh], jnp.max(s[hh], axis=-1, keepdims=True)) for hh in heads]
        p = [jnp.exp(s[hh] - jnp.concatenate([m_new[hh]] * rep, axis=1)) for hh in heads]
        alpha = [jnp.exp(m_prev[hh] - m_new[hh]) for hh in heads]
        for hh in heads:
            l_ref[hh] = alpha[hh] * l_ref[hh] + jnp.sum(p[hh], axis=-1, keepdims=True)
            acc_ref[hh] = alpha[hh] * acc_ref[hh] + jnp.dot(p[hh].astype(BF16), vs, preferred_element_type=F32)
            m_ref[hh] = m_new[hh]

    for hh in heads:
        m_ref[hh] = jnp.full((blk, LANES), -jnp.inf, F32)
        l_ref[hh] = jnp.zeros((blk, LANES), F32)
        acc_ref[hh] = jnp.zeros((blk, LANES), F32)

    def body(j, _):
        step(j, False)
        return 0

    lax.fori_loop(0, qi, body, 0)
    step(qi, True)

    lane = lax.broadcasted_iota(jnp.int32, (blk, LANES), 1)
    o0 = acc_ref[0] / l_ref[0]
    o1 = acc_ref[1] / l_ref[1]
    o_ref[0] = jnp.where(lane < MLA_V, o0, o1)


def causal_attention(q, k, v, blk=512):
    bsz, nh, seq, dk = q.shape
    return pl.pallas_call(
        functools.partial(_attn_kernel, blk=blk),
        grid=(bsz, nh // 2, seq // blk),
        in_specs=[pl.BlockSpec((1, 2, blk, dk), lambda b, p, i: (b, p, i, 0)),
                  pl.BlockSpec((1, 2, seq, dk), lambda b, p, i: (b, p, 0, 0)),
                  pl.BlockSpec((1, seq, LANES), lambda b, p, i: (b, 0, p))],
        out_specs=pl.BlockSpec((1, blk, LANES), lambda b, p, i: (b, i, p)),
        out_shape=jax.ShapeDtypeStruct((bsz, seq, nh // 2 * LANES), F32),
        scratch_shapes=[pltpu.VMEM((2, blk, LANES), F32)] * 3,
        compiler_params=_params("parallel", "parallel", "arbitrary"),
        name="mla_attention",
    )(q, k, v)


def _rw_prep_kernel(*refs, first):
    if first:
        (h_ref, prev_ref, mu_ref, w0_ref, w2_ref, a0_ref, a2_ref, g2_ref, kk_ref, ka_ref, rk_ref, ones_ref,
         r_out, lw_out, k_out, kk_out, b_out, v_ref, bonus_ref, g_ref) = refs
    else:
        (h_ref, prev_ref, mu_ref, w0_ref, w2_ref, a0_ref, a2_ref, g2_ref, kk_ref, ka_ref, rk_ref, ones_ref,
         vf_ref, v0_ref, v1_ref, v2_ref,
         r_out, lw_out, k_out, kk_out, b_out, v_ref, bonus_ref, g_ref) = refs
    h = h_ref[0]
    tm = h.shape[0]
    last = prev_ref[0, SUBLANES - 1:SUBLANES, :]
    last = jnp.where(pl.program_id(1) == 0, 0.0, last)
    row = lax.broadcasted_iota(jnp.int32, h.shape, 0)
    h_prev = jnp.where(row == 0, last, pltpu.roll(h, 1, axis=0))
    h = h + (h_prev - h) * mu_ref[...]
    w = GROUP_WIDTH
    r = h[:, 0:w]
    k = h[:, w:2 * w]
    v = h[:, 2 * w:3 * w]
    wa = h[:, 3 * w:3 * w + LANES]
    gd = h[:, 3 * w + LANES:]
    z = w0_ref[...] + _bdot(jnp.tanh(wa), w2_ref[...])
    nz = -z
    softplus = jnp.maximum(nz, 0.0) + jnp.log(1.0 + jnp.exp(-jnp.abs(nz)))
    log_decay = -jnp.exp(-softplus - 0.5)
    a = _sigmoid(a0_ref[...] + _bdot(wa, a2_ref[...]))
    g_ref[0] = _bdot(_sigmoid(gd), g2_ref[...])
    if not first:
        mix = _sigmoid(v0_ref[...] + _bdot(_bdot(v, v1_ref[...]), v2_ref[...]))
        v = v + (vf_ref[0] - v) * mix
    kk = k * kk_ref[...]
    norm = jnp.sqrt(_split_dot(kk * kk, ones_ref[...]))
    kk = kk / jnp.maximum(norm, 1e-12)
    k = k * (1.0 + (a - 1.0) * ka_ref[...])
    v_ref[0] = v
    bonus_ref[0] = _split_dot(r * k * rk_ref[...], ones_ref[...]) * v
    r_out[0] = r
    lw_out[0] = log_decay
    k_out[0] = k
    kk_out[0] = kk
    b_out[0] = kk * a


def rw_pack_weights(w2, a2, v1=None, v2=None):
    zeros = lambda n, m: jnp.zeros((n, m), F32)
    half = LANES // 2
    w2p = jnp.concatenate([w2, zeros(half, GROUP_WIDTH)], axis=0).astype(BF16)
    a2p = jnp.concatenate([zeros(half, GROUP_WIDTH), a2], axis=0).astype(BF16)
    if v1 is None:
        return w2p, a2p
    v1p = jnp.concatenate([v1, zeros(GROUP_WIDTH, LANES - v1.shape[1])], axis=1).astype(BF16)
    v2p = jnp.concatenate([v2, zeros(LANES - v2.shape[0], GROUP_WIDTH)], axis=0).astype(BF16)
    return w2p, a2p, v1p, v2p


def head_ones(width, head, scale):
    hd = np.arange(width) // head
    return jnp.asarray((hd[:, None] == hd[None, :]).astype(np.float32) * scale, BF16)


def rw_prep(h, mu, w0, w2p, a0, a2p, g2, k_k, k_a, r_k, vmix=None, tm=512):
    bsz, seq, wh = h.shape
    w = GROUP_WIDTH
    first = vmix is None
    row = lambda b, i: (b, i, 0)
    c2 = lambda b, i: (0, 0)
    vec = lambda t: t.reshape(1, -1)
    prev_map = lambda b, i: (b, jnp.maximum(i * (tm // SUBLANES) - 1, 0), 0)
    args = [h, h, vec(mu), vec(w0), w2p, vec(a0), a2p, g2.astype(BF16), vec(k_k), vec(k_a), vec(r_k),
            head_ones(w, RW_HEAD, 1.0)]
    specs = [pl.BlockSpec((1, tm, wh), row), pl.BlockSpec((1, SUBLANES, wh), prev_map),
             pl.BlockSpec((1, wh), c2), pl.BlockSpec((1, w), c2), pl.BlockSpec((LANES, w), c2),
             pl.BlockSpec((1, w), c2), pl.BlockSpec((LANES, w), c2), pl.BlockSpec((LANES, w), c2),
             pl.BlockSpec((1, w), c2), pl.BlockSpec((1, w), c2), pl.BlockSpec((1, w), c2),
             pl.BlockSpec((w, w), c2)]
    if not first:
        v_first, v0, v1p, v2p = vmix
        args += [v_first, vec(v0), v1p, v2p]
        specs += [pl.BlockSpec((1, tm, w), row), pl.BlockSpec((1, w), c2),
                  pl.BlockSpec((w, LANES), c2), pl.BlockSpec((LANES, w), c2)]
    return pl.pallas_call(
        functools.partial(_rw_prep_kernel, first=first),
        grid=(bsz, seq // tm),
        in_specs=specs,
        out_specs=[pl.BlockSpec((1, tm, w), row)] * 8,
        out_shape=[jax.ShapeDtypeStruct((bsz, seq, w), F32)] * 8,
        compiler_params=_params("parallel", "parallel"),
        name="rwkv_prep",
    )(*args)


WKV_CHUNK = 64
_NN = (((1,), (0,)), ((), ()))
_NT = (((1,), (1,)), ((), ()))
_TN = (((0,), (0,)), ((), ()))


def _split(a):
    hi = a.astype(BF16)
    return hi, (a - hi.astype(F32)).astype(BF16)


def _mm(a, b, dims=_NN, passes=1):
    if passes == 1:
        return lax.dot_general(a.astype(BF16), b.astype(BF16), dims, preferred_element_type=F32)
    ah, al = _split(a)
    bh, bl = _split(b)
    d = lambda x, y: lax.dot_general(x, y, dims, preferred_element_type=F32)
    return d(ah, bh) + (d(ah, bl) + d(al, bh))


def _wkv_pre_kernel(r_ref, lw_ref, k_ref, kk_ref, b_ref, v_ref, tri_ref,
                    q_ref, y0_ref, phi_ref, dlt_ref, *, nchunk):
    c = WKV_CHUNK
    n2 = 2 * c
    lane = lax.broadcasted_iota(jnp.int32, (c, LANES), 1)
    head0 = lane < RW_HEAD
    row = lax.broadcasted_iota(jnp.int32, (n2, n2), 0)
    col = lax.broadcasted_iota(jnp.int32, (n2, n2), 1)
    strict = col < row
    incl = col <= row
    eye = (row == col).astype(F32)
    stack = lambda x: jnp.concatenate([jnp.where(head0, x, 0.0), jnp.where(head0, 0.0, x)], axis=0)
    fold = lambda x: x[:c] + x[c:]
    tri = tri_ref[...]
    chains = [(ci, p) for ci in range(nchunk) for p in range(GROUP_WIDTH // LANES)]
    st = []
    for ci, p in chains:
        sl = slice(p * LANES, (p + 1) * LANES)
        rows = slice(ci * c, (ci + 1) * c)
        lw = lw_ref[0, rows, sl]
        lw_hi, lw_lo = _split(lw)
        cum = (jnp.dot(tri, lw_hi, preferred_element_type=F32)
               + jnp.dot(tri, lw_lo, preferred_element_type=F32))
        last = cum[c - 1:c, :]
        e_in = jnp.exp(cum)
        e_ex = jnp.exp(cum - lw)
        e_neg = jnp.exp(-cum)
        e_end = jnp.exp(last - cum)
        kk, r, k, b, v = kk_ref[0, rows, sl], r_ref[0, rows, sl], k_ref[0, rows, sl], b_ref[0, rows, sl], v_ref[0, rows, sl]
        kt2 = stack(kk * e_ex)
        rt2 = stack(r * e_in)
        lhs = jnp.concatenate([kt2, rt2], axis=0)
        gb = _mm(lhs, stack(b * e_neg), _NT)
        gk = _mm(lhs, stack(k * e_neg), _NT)
        x = -jnp.where(strict, gb[:n2], 0.0)
        st.append(dict(sl=sl, rows=rows, kt2=kt2, rt2=rt2, kp2=stack(k * e_end), bp2=stack(b * e_end), v2=stack(v),
                       g2=jnp.where(strict, gk[:n2], 0.0), rb2=jnp.where(incl, gb[n2:], 0.0),
                       rk2=jnp.where(incl, gk[n2:], 0.0), last=last, x=x, t=eye + x))
    for _ in range(int(np.log2(c)) - 1):
        for d in st:
            d["x"] = _mm(d["x"], d["x"])
        for d in st:
            d["t"] = d["t"] + _mm(d["t"], d["x"])
    for d in st:
        d["gv2"] = _mm(d["g2"], d["v2"])
    for d in st:
        d["tku"] = _mm(d["t"], jnp.concatenate([d["kt2"], d["gv2"]], axis=1))
    for d in st:
        d["rbz"] = _mm(d["rb2"], d["tku"])
    for (ci, p), d in zip(chains, st):
        sl, rows, tku, rbz = d["sl"], d["rows"], d["tku"], d["rbz"]
        q_ref[0, rows, sl] = fold(d["rt2"] - rbz[:, :LANES])
        y0_ref[0, rows, sl] = fold(_mm(d["rk2"], d["v2"]) - rbz[:, LANES:])
        decay = jnp.where(row == col, jnp.broadcast_to(jnp.exp(d["last"]), (n2, LANES)), 0.0)
        phi_ref[0, ci, p] = decay - _mm(d["bp2"], tku[:, :LANES], _TN)
        dlt_ref[0, ci, p] = _mm(d["kp2"], d["v2"], _TN) - _mm(d["bp2"], tku[:, LANES:], _TN)


def wkv_precompute(r, lw, k, kk, b, v, block=256):
    bsz, seq, w = r.shape
    c = WKV_CHUNK
    nchunk = block // c
    npair = w // LANES
    tri = jnp.asarray(np.tril(np.ones((c, c), np.float32)), BF16)
    row = lambda bb, i: (bb, i, 0)
    blk = pl.BlockSpec((1, block, w), row)
    mat = pl.BlockSpec((1, nchunk, npair, LANES, LANES), lambda bb, i: (bb, i, 0, 0, 0))
    mshape = jax.ShapeDtypeStruct((bsz, seq // c, npair, LANES, LANES), F32)
    return pl.pallas_call(
        functools.partial(_wkv_pre_kernel, nchunk=nchunk),
        grid=(bsz, seq // block),
        in_specs=[blk] * 6 + [pl.BlockSpec((c, c), lambda bb, i: (0, 0))],
        out_specs=[blk, blk, mat, mat],
        out_shape=[jax.ShapeDtypeStruct((bsz, seq, w), F32)] * 2 + [mshape, mshape],
        compiler_params=_params("parallel", "parallel"),
        name="wkv7_chunk_prep",
    )(r, lw, k, kk, b, v, tri)


def _wkv_seq_kernel(q_ref, y0_ref, phi_ref, dlt_ref, y_ref, st_ref, *, nchunk):
    @pl.when(pl.program_id(1) == 0)
    def _():
        st_ref[...] = jnp.zeros_like(st_ref)

    c = WKV_CHUNK
    npair = GROUP_WIDTH // LANES
    m = [st_ref[p] for p in range(npair)]
    for i in range(nchunk):
        rows = slice(i * c, (i + 1) * c)
        for p in range(npair):
            sl = slice(p * LANES, (p + 1) * LANES)
            y_ref[0, rows, sl] = _mm(q_ref[0, rows, sl], m[p], _NN, 3) + y0_ref[0, rows, sl]
        m = [_mm(phi_ref[0, i, p], m[p], _NN, 3) + dlt_ref[0, i, p] for p in range(npair)]
    for p in range(npair):
        st_ref[p] = m[p]


def wkv_sequential(q, y0, phi, dlt, block=512):
    bsz, seq, w = q.shape
    c = WKV_CHUNK
    nchunk = block // c
    npair = w // LANES
    row = lambda bb, i: (bb, i, 0)
    blk = pl.BlockSpec((1, block, w), row)
    mat = pl.BlockSpec((1, nchunk, npair, LANES, LANES), lambda bb, i: (bb, i, 0, 0, 0))
    return pl.pallas_call(
        functools.partial(_wkv_seq_kernel, nchunk=nchunk),
        grid=(bsz, seq // block),
        in_specs=[blk, blk, mat, mat],
        out_specs=blk,
        out_shape=jax.ShapeDtypeStruct((bsz, seq, w), F32),
        scratch_shapes=[pltpu.VMEM((npair, LANES, LANES), F32)],
        compiler_params=_params("parallel", "arbitrary"),
        name="wkv7_chunk_scan",
    )(q, y0, phi, dlt)


def _outproj_kernel(x_ref, ys5_ref, yrw_ref, bonus_ref, grw_ref, ymla_ref, yret_ref, lng_ref, lnb_ref, ones_ref,
                    wout_ref, g1_ref, ng_ref, sc_ref, sh_ref, rhi_ref, rlo_ref, rb_ref,
                    xo_ref, hn_ref, lg_ref):
    y = yrw_ref[0]
    mean = _split_dot(y, ones_ref[...])
    yc = y - mean
    var = _split_dot(yc * yc, ones_ref[...])
    yrw = (yc * lax.rsqrt(var + RW_GN_EPS) * lng_ref[...] + lnb_ref[...] + bonus_ref[0]) * grw_ref[0]
    w = GROUP_WIDTH
    mixed = (_bdot(ys5_ref[0], wout_ref[0:w, :]) + _bdot(yrw, wout_ref[w:2 * w, :])
             + _bdot(ymla_ref[0], wout_ref[2 * w:3 * w, :]) + _bdot(yret_ref[0], wout_ref[3 * w:4 * w, :]))
    x = x_ref[0] + g1_ref[0] * mixed
    xo_ref[0] = x
    ms = jnp.mean(x * x, axis=-1, keepdims=True)
    hn = x * lax.rsqrt(ms + NORM_EPS) * ng_ref[...]
    hn = hn * (1.0 + sc_ref[0]) + sh_ref[0]
    hi = hn.astype(BF16)
    hn_ref[0] = hi
    lo = (hn - hi.astype(F32)).astype(BF16)
    lg_ref[0] = (jnp.dot(hi, rhi_ref[...], preferred_element_type=F32)
                 + jnp.dot(lo, rhi_ref[...], preferred_element_type=F32)
                 + jnp.dot(hi, rlo_ref[...], preferred_element_type=F32) + rb_ref[...])


def out_proj(x, y_s5, y_rw, bonus, g_rw, y_mla, y_ret, ln_g, ln_b, w_out, g1, norm_g, sc2, sh2,
             router_w, router_b, tm=512):
    bsz, seq, d = x.shape
    w = GROUP_WIDTH
    row = lambda b, i: (b, i, 0)
    per_b = lambda b, i: (b, 0, 0)
    c2 = lambda b, i: (0, 0)
    vec = lambda t: t.reshape(1, -1)
    pad = LANES - router_w.shape[1]
    rw_pad = jnp.concatenate([router_w, jnp.zeros((d, pad), F32)], axis=1)
    r_hi = rw_pad.astype(BF16)
    r_lo = (rw_pad - r_hi.astype(F32)).astype(BF16)
    rb = jnp.concatenate([router_b, jnp.zeros((pad,), F32)]).reshape(1, LANES)
    mixer = pl.BlockSpec((1, tm, w), row)
    return pl.pallas_call(
        _outproj_kernel,
        grid=(bsz, seq // tm),
        in_specs=[pl.BlockSpec((1, tm, d), row), mixer, mixer, mixer, mixer, mixer, mixer,
                  pl.BlockSpec((1, w), c2), pl.BlockSpec((1, w), c2), pl.BlockSpec((w, w), c2),
                  pl.BlockSpec((4 * w, d), c2),
                  pl.BlockSpec((1, 1, d), per_b), pl.BlockSpec((1, d), c2),
                  pl.BlockSpec((1, 1, d), per_b), pl.BlockSpec((1, 1, d), per_b),
                  pl.BlockSpec((d, LANES), c2), pl.BlockSpec((d, LANES), c2), pl.BlockSpec((1, LANES), c2)],
        out_specs=[pl.BlockSpec((1, tm, d), row), pl.BlockSpec((1, tm, d), row), pl.BlockSpec((1, tm, LANES), row)],
        out_shape=[jax.ShapeDtypeStruct((bsz, seq, d), F32), jax.ShapeDtypeStruct((bsz, seq, d), BF16),
                   jax.ShapeDtypeStruct((bsz, seq, LANES), F32)],
        compiler_params=_params("parallel", "parallel"),
        name="out_proj",
    )(x, y_s5, y_rw, bonus, g_rw, y_mla, y_ret, vec(ln_g), vec(ln_b), head_ones(w, RW_HEAD, 1.0 / RW_HEAD),
      w_out.astype(BF16), g1, vec(norm_g), sc2, sh2, r_hi, r_lo, rb)


MOE_ROWS = 512


def _moe_kernel(blk_e_ref, blk_on_ref, x_ref, wg_ref, bg_ref, wu_ref, bu_ref, wd_ref, bd_ref, o_ref, wb_ref):
    i = pl.program_id(0)
    changed = jnp.logical_or(i == 0, blk_e_ref[i] != blk_e_ref[jnp.maximum(i - 1, 0)])

    @pl.when(changed)
    def _():
        wb_ref[0] = wg_ref[0].astype(BF16)
        wb_ref[1] = wu_ref[0].astype(BF16)
        wb_ref[2] = wd_ref[0].astype(BF16)

    @pl.when(blk_on_ref[i] > 0)
    def _():
        x = x_ref[...]
        gt = jnp.minimum(jnp.dot(x, wb_ref[0], preferred_element_type=F32) + bg_ref[0], SWIGLU_LIMIT)
        up = jnp.clip(jnp.dot(x, wb_ref[1], preferred_element_type=F32) + bu_ref[0], -SWIGLU_LIMIT, SWIGLU_LIMIT)
        act = gt * _sigmoid(SWIGLU_ALPHA * gt) * (up + 1.0)
        o_ref[...] = (jnp.dot(act.astype(BF16), wb_ref[2], preferred_element_type=F32) + bd_ref[0]).astype(BF16)


def moe_experts(xb, blk_e, blk_on, w_gate, b_gate, w_up, b_up, w_down, b_down):
    p_rows, d = xb.shape
    n_e, _, de = w_gate.shape
    wmap = lambda i, e, on: (e[i], 0, 0)
    rows = lambda i, e, on: (i, 0)
    return pl.pallas_call(
        _moe_kernel,
        grid_spec=pltpu.PrefetchScalarGridSpec(
            num_scalar_prefetch=2,
            grid=(p_rows // MOE_ROWS,),
            in_specs=[pl.BlockSpec((MOE_ROWS, d), rows),
                      pl.BlockSpec((1, d, de), wmap), pl.BlockSpec((1, 1, de), wmap),
                      pl.BlockSpec((1, d, de), wmap), pl.BlockSpec((1, 1, de), wmap),
                      pl.BlockSpec((1, de, d), wmap), pl.BlockSpec((1, 1, d), wmap)],
            out_specs=pl.BlockSpec((MOE_ROWS, d), rows),
            scratch_shapes=[pltpu.VMEM((3, d, de), BF16)]),
        out_shape=jax.ShapeDtypeStruct((p_rows, d), BF16),
        compiler_params=_params("arbitrary"),
        name="moe_experts",
    )(blk_e, blk_on, xb, w_gate, b_gate.reshape(n_e, 1, de), w_up, b_up.reshape(n_e, 1, de),
      w_down, b_down.reshape(n_e, 1, d))


def moe_route(logits):
    t = logits.shape[0]
    top_val, top_idx = lax.top_k(logits, TOP_K)
    top_w = jax.nn.softmax(top_val, axis=-1)
    n_assign = t * TOP_K
    flat_e = top_idx.reshape(-1).astype(jnp.int32)
    iota = jnp.arange(n_assign, dtype=jnp.int32)
    sorted_e, order = lax.sort((flat_e, iota), num_keys=1)
    _, rank = lax.sort((order, iota), num_keys=1)
    experts = jnp.arange(N_EXPERTS, dtype=jnp.int32)
    start = jnp.sum(sorted_e[None, :] < experts[:, None], axis=1, dtype=jnp.int32)
    counts = jnp.sum(sorted_e[None, :] == experts[:, None], axis=1, dtype=jnp.int32)
    padded = (counts + MOE_ROWS - 1) // MOE_ROWS * MOE_ROWS
    pad_end = jnp.cumsum(padded)
    pad_start = pad_end - padded
    pos = rank + (pad_start - start)[flat_e]
    p_rows = n_assign + N_EXPERTS * MOE_ROWS
    n_blocks = p_rows // MOE_ROWS
    blk_first = jnp.arange(n_blocks, dtype=jnp.int32) * MOE_ROWS
    blk_e = jnp.minimum(jnp.sum(pad_end[None, :] <= blk_first[:, None], axis=1, dtype=jnp.int32), N_EXPERTS - 1)
    blk_on = (blk_first < pad_start[blk_e] + counts[blk_e]).astype(jnp.int32)
    row = jnp.arange(p_rows, dtype=jnp.int32)
    row_e = jnp.repeat(blk_e, MOE_ROWS)
    within = row - pad_start[row_e]
    src = jnp.clip(start[row_e] + within, 0, n_assign - 1)
    buf_tok = jnp.where(within < counts[row_e], order[src] // TOP_K, 0)
    return top_w, buf_tok, pos.reshape(t, TOP_K), blk_e, blk_on


def moe_ffn(hn_bf16, logits, w_gate, b_gate, w_up, b_up, w_down, b_down):
    bsz, seq, d = hn_bf16.shape
    t = bsz * seq
    top_w, buf_tok, pos, blk_e, blk_on = moe_route(logits.reshape(t, LANES)[:, :N_EXPERTS])
    xb = hn_bf16.reshape(t, d).at[buf_tok].get(mode='promise_in_bounds')
    yb = moe_experts(xb, blk_e, blk_on, w_gate, b_gate, w_up, b_up, w_down, b_down)
    yg = yb.at[pos.T.reshape(-1)].get(mode='promise_in_bounds')
    return yg.reshape(TOP_K, bsz, seq, d), top_w.reshape(bsz, seq, TOP_K)


def _combine_kernel(*refs, final):
    if final:
        x_ref, y_ref, w_ref, g_ref, ng_ref, o_ref = refs
    else:
        x_ref, y_ref, w_ref, g_ref, o_ref = refs
    w = w_ref[0]
    y = y_ref[0, 0].astype(F32) * w[:, 0:1]
    for j in range(1, TOP_K):
        y = y + y_ref[j, 0].astype(F32) * w[:, j:j + 1]
    x = x_ref[0] + g_ref[0] * y
    if final:
        x = x * lax.rsqrt(jnp.mean(x * x, axis=-1, keepdims=True) + NORM_EPS) * ng_ref[...]
    o_ref[0] = x


def moe_combine(x, yg, top_w, gate, final_g=None, tm=512):
    bsz, seq, d = x.shape
    row = lambda b, i: (b, i, 0)
    specs = [pl.BlockSpec((1, tm, d), row),
             pl.BlockSpec((TOP_K, 1, tm, d), lambda b, i: (0, b, i, 0)),
             pl.BlockSpec((1, tm, TOP_K), row),
             pl.BlockSpec((1, 1, d), lambda b, i: (b, 0, 0))]
    args = [x, yg, top_w, gate]
    if final_g is not None:
        specs.append(pl.BlockSpec((1, d), lambda b, i: (0, 0)))
        args.append(final_g.reshape(1, d))
    return pl.pallas_call(
        functools.partial(_combine_kernel, final=final_g is not None),
        grid=(bsz, seq // tm),
        in_specs=specs,
        out_specs=pl.BlockSpec((1, tm, d), row),
        out_shape=jax.ShapeDtypeStruct((bsz, seq, d), F32),
        compiler_params=_params("parallel", "parallel"),
        name="moe_combine",
    )(*args)


S5_CHUNK = 512


def kernel(x, c, positions, ada_w, ada_b, norm_mix_g, norm_ffn_g, w_in, w_out,
           s5_lambda_re, s5_lambda_im, s5_log_step, s5_b_re, s5_b_im, s5_c_re, s5_c_im,
           s5_d, s5_glu_w, s5_glu_b,
           rw_mu, rw_w0, rw_w2, rw_a0, rw_a2, rw_g2, rw_k_k, rw_k_a, rw_r_k, rw_ln_g, rw_ln_b,
           rw_v0, rw_v1, rw_v2,
           mla_q_norm_g, mla_kv_norm_g, mla_w_q_up, mla_w_kv_up,
           router_w, router_b, ex_w_gate, ex_b_gate, ex_w_up, ex_b_up, ex_w_down, ex_b_down,
           final_norm_g):
    depth = w_in.shape[0]
    mod = adaln_mod(c, ada_w, ada_b)
    cos_t, sin_t = rope_tables(positions)
    v_first = None
    for l in range(depth):
        sh1, sc1, g1, sh2, sc2, g2 = [m[:, None, :] for m in jnp.split(mod[l], N_MOD, axis=-1)]
        s5_u, rw_in, mla_in, ret_in = in_proj(x, sc1, sh1, norm_mix_g[l], pack_w_in(w_in[l]))

        prep = s5_prepare(s5_lambda_re[l], s5_lambda_im[l], s5_log_step[l], s5_b_re[l], s5_b_im[l],
                          s5_c_re[l], s5_c_im[l], S5_CHUNK // SUBLANES)
        y_s5 = s5_mixer(s5_u, prep, s5_d[l], s5_glu_w[l], s5_glu_b[l], chunk=S5_CHUNK)

        if l == 0:
            w2p, a2p = rw_pack_weights(rw_w2[l], rw_a2[l])
            vmix = None
        else:
            w2p, a2p, v1p, v2p = rw_pack_weights(rw_w2[l], rw_a2[l], rw_v1[l - 1], rw_v2[l - 1])
            vmix = (v_first, rw_v0[l - 1], v1p, v2p)
        r_rw, lw_rw, k_rw, kk_rw, b_rw, v_rw, bonus, g_rw = rw_prep(
            rw_in, rw_mu[l], rw_w0[l], w2p, rw_a0[l], a2p, rw_g2[l], rw_k_k[l], rw_k_a[l],
            rw_r_k[l].reshape(-1), vmix)
        if l == 0:
            v_first = v_rw
        y_rw = wkv_sequential(*wkv_precompute(r_rw, lw_rw, k_rw, kk_rw, b_rw, v_rw))

        q, k, v = mla_prep(mla_in, cos_t, sin_t, mla_q_norm_g[l], mla_kv_norm_g[l],
                           mla_pack_weights(mla_w_q_up[l], mla_w_kv_up[l]))
        y_mla = causal_attention(q, k, v)

        y_ret = retention_mixer(ret_in, cos_t, sin_t)

        x, hn, logits = out_proj(x, y_s5, y_rw, bonus, g_rw, y_mla, y_ret, rw_ln_g[l], rw_ln_b[l], w_out[l],
                                 g1, norm_ffn_g[l], sc2, sh2, router_w[l], router_b[l])
        yg, top_w = moe_ffn(hn, logits, ex_w_gate[l], ex_b_gate[l], ex_w_up[l], ex_b_up[l],
                            ex_w_down[l], ex_b_down[l])
        x = moe_combine(x, yg, top_w, g2, final_norm_g if l == depth - 1 else None)
    return x
```

```python
import functools
import math

import numpy as np
import jax
import jax.numpy as jnp
from jax import lax
from jax.experimental import pallas as pl
from jax.experimental.pallas import tpu as pltpu

F32 = jnp.float32
BF16 = jnp.bfloat16

D_MODEL = 1024
GROUP_WIDTH = 256
S5_CH = 16
S5_GROUPS = 16
S5_STATE = 64
S5_FLAT = S5_GROUPS * S5_STATE
RW_HEADS = 4
RW_HEAD = 64
RW_GN_EPS = 64e-5
MLA_HEADS = 4
MLA_NOPE = 64
MLA_ROPE = 32
MLA_V = 64
MLA_Q_RANK = 256
MLA_KV_RANK = 128
RET_HEADS = 4
RET_QK = 32
RET_V = 64
ROPE_BASE = 10000.0
N_EXPERTS = 32
TOP_K = 4
SWIGLU_ALPHA = 1.702
SWIGLU_LIMIT = 7.0
NORM_EPS = 1e-5
N_MOD = 6

LANES = 128
SUBLANES = 8
VMEM_LIMIT_BYTES = 56 * 1024 * 1024

IN_S5 = (0, 256)
IN_RW = (256, 1280)
IN_MLA = (1280, 1920)
IN_RET = (1920, 2944)
IN_PACKED = 2944


def _params(*sem):
    return pltpu.CompilerParams(dimension_semantics=sem, vmem_limit_bytes=VMEM_LIMIT_BYTES)


def _bdot(a, b):
    return jnp.dot(a.astype(BF16), b.astype(BF16), preferred_element_type=F32)


def _split_dot(a, b_bf16):
    hi = a.astype(BF16)
    lo = (a - hi.astype(F32)).astype(BF16)
    return (jnp.dot(hi, b_bf16, preferred_element_type=F32)
            + jnp.dot(lo, b_bf16, preferred_element_type=F32))


def _sigmoid(x):
    return 1.0 / (1.0 + jnp.exp(-x))


def _adaln_kernel(c_ref, w_ref, b_ref, o_ref):
    c = c_ref[...]
    cond = c * _sigmoid(c)
    o_ref[0] = _bdot(cond, w_ref[0]) + b_ref[0]


def adaln_mod(c, ada_w, ada_b):
    depth, d, n = ada_w.shape
    bsz = c.shape[0]
    tn = 1536
    return pl.pallas_call(
        _adaln_kernel,
        grid=(depth, n // tn),
        in_specs=[pl.BlockSpec((bsz, d), lambda l, j: (0, 0)),
                  pl.BlockSpec((1, d, tn), lambda l, j: (l, 0, j)),
                  pl.BlockSpec((1, 1, tn), lambda l, j: (l, 0, j))],
        out_specs=pl.BlockSpec((1, bsz, tn), lambda l, j: (l, 0, j)),
        out_shape=jax.ShapeDtypeStruct((depth, bsz, n), F32),
        compiler_params=_params("parallel", "parallel"),
        name="adaln_mod",
    )(c, ada_w, ada_b.reshape(depth, 1, n))


def _inproj_kernel(x_ref, sc_ref, sh_ref, g_ref, w_ref, s5_ref, rw_ref, mla_ref, ret_ref):
    x = x_ref[0]
    ms = jnp.mean(x * x, axis=-1, keepdims=True)
    hn = x * lax.rsqrt(ms + NORM_EPS) * g_ref[...]
    hn = hn * (1.0 + sc_ref[0]) + sh_ref[0]
    p = jnp.dot(hn.astype(BF16), w_ref[...], preferred_element_type=F32)
    s5_ref[0] = p[:, IN_S5[0]:IN_S5[1]]
    rw_ref[0] = p[:, IN_RW[0]:IN_RW[1]]
    mla_ref[0] = p[:, IN_MLA[0]:IN_MLA[1]]
    ret_ref[0] = p[:, IN_RET[0]:IN_RET[1]]


def in_proj(x, sc, sh, g, w_packed, tm=512):
    bsz, seq, d = x.shape
    widths = [b - a for a, b in (IN_S5, IN_RW, IN_MLA, IN_RET)]
    row = lambda b, i: (b, i, 0)
    per_b = lambda b, i: (b, 0, 0)
    return pl.pallas_call(
        _inproj_kernel,
        grid=(bsz, seq // tm),
        in_specs=[pl.BlockSpec((1, tm, d), row),
                  pl.BlockSpec((1, 1, d), per_b),
                  pl.BlockSpec((1, 1, d), per_b),
                  pl.BlockSpec((1, d), lambda b, i: (0, 0)),
                  pl.BlockSpec((d, IN_PACKED), lambda b, i: (0, 0))],
        out_specs=[pl.BlockSpec((1, tm, w), row) for w in widths],
        out_shape=[jax.ShapeDtypeStruct((bsz, seq, w), F32) for w in widths],
        compiler_params=_params("parallel", "parallel"),
        name="in_proj",
    )(x, sc, sh, g.reshape(1, d), w_packed)


def _swap_halves(cols, block):
    cols = np.asarray(cols).reshape(-1, 2, block // 2)
    return cols[:, ::-1, :].reshape(-1)


def pack_w_in(w_in_l):
    zero = w_in_l.shape[1]
    s5 = np.arange(0, 256)
    rw = np.arange(256, 1280)
    qc = np.arange(1280, 1536)
    kvc = np.arange(1536, 1664)
    kpe = np.arange(1664, 1696)
    z = lambda n: np.full((n,), zero)
    kpe_slot = np.concatenate([z(MLA_NOPE), kpe, z(LANES - MLA_NOPE - MLA_ROPE)])
    kpe_sw_slot = np.concatenate([z(MLA_NOPE), _swap_halves(kpe, MLA_ROPE), z(LANES - MLA_NOPE - MLA_ROPE)])
    rq = np.arange(1696, 1824)
    rk = np.arange(1824, 1952)
    rv = np.arange(1952, 2208)
    rg = np.arange(2208, 2464)
    idx = np.concatenate([s5, rw, qc, kvc, kpe_slot, kpe_sw_slot,
                          rq, rk, _swap_halves(rq, RET_QK), _swap_halves(rk, RET_QK), rv, rg])
    assert idx.shape[0] == IN_PACKED
    w_ext = jnp.concatenate([w_in_l, jnp.zeros((w_in_l.shape[0], 1), w_in_l.dtype)], axis=1)
    return jnp.take(w_ext, jnp.asarray(idx, jnp.int32), axis=1).astype(BF16)


def rope_tables(positions):
    inv = ROPE_BASE ** (-jnp.arange(0, MLA_ROPE, 2, dtype=F32) / MLA_ROPE)
    ang = positions.astype(F32)[..., None] * inv
    cos, sin = jnp.cos(ang), jnp.sin(ang)
    reps = LANES // MLA_ROPE
    cos_t = jnp.tile(jnp.concatenate([cos, cos], axis=-1), (1, 1, reps))
    sin_t = jnp.tile(jnp.concatenate([-sin, sin], axis=-1), (1, 1, reps))
    return cos_t, sin_t


def _s5_kernel(u_ref, bre_ref, bim_ref, cre_ref, cim_ref, lam_ref, lamq_ref, lamseg_ref,
               d_ref, gw_ref, gb_ref, o_ref, sre_ref, sim_ref, st_ref, *, nq):
    @pl.when(pl.program_id(1) == 0)
    def _():
        st_ref[...] = jnp.zeros_like(st_ref)

    u = u_ref[0]
    ub = u.astype(BF16)
    sre_ref[...] = jnp.dot(ub, bre_ref[...], preferred_element_type=F32)
    sim_ref[...] = jnp.dot(ub, bim_ref[...], preferred_element_type=F32)
    lam_re = lam_ref[0:1, :]
    lam_im = lam_ref[1:2, :]

    def scan_body(q, carry):
        cr, ci = carry
        rows = pl.ds(pl.multiple_of(q * SUBLANES, SUBLANES), SUBLANES)
        nr = lam_re * cr - lam_im * ci + sre_ref[rows, :]
        ni = lam_re * ci + lam_im * cr + sim_ref[rows, :]
        sre_ref[rows, :] = nr
        sim_ref[rows, :] = ni
        return nr, ni

    zero = jnp.zeros((SUBLANES, S5_FLAT), F32)
    end_re, end_im = lax.fori_loop(0, nq, scan_body, (zero, zero))

    seg_re = lamseg_ref[0:1, :]
    seg_im = lamseg_ref[1:2, :]
    cr, ci = st_ref[0:1, :], st_ref[1:2, :]
    in_re, in_im = [], []
    for r in range(SUBLANES):
        in_re.append(cr)
        in_im.append(ci)
        er, ei = end_re[r:r + 1, :], end_im[r:r + 1, :]
        cr, ci = seg_re * cr - seg_im * ci + er, seg_re * ci + seg_im * cr + ei
    st_ref[0:1, :] = cr
    st_ref[1:2, :] = ci
    car_re = jnp.concatenate(in_re, axis=0)
    car_im = jnp.concatenate(in_im, axis=0)

    def fix_body(q, _):
        rows = pl.ds(pl.multiple_of(q * SUBLANES, SUBLANES), SUBLANES)
        pr = lamq_ref[0, pl.ds(q, 1), :]
        pi = lamq_ref[1, pl.ds(q, 1), :]
        sre_ref[rows, :] = sre_ref[rows, :] + (pr * car_re - pi * car_im)
        sim_ref[rows, :] = sim_ref[rows, :] + (pr * car_im + pi * car_re)
        return 0

    lax.fori_loop(0, nq, fix_body, 0)

    y = (jnp.dot(sre_ref[...].astype(BF16), cre_ref[...], preferred_element_type=F32)
         - jnp.dot(sim_ref[...].astype(BF16), cim_ref[...], preferred_element_type=F32))
    y = y + d_ref[...] * u
    y = jax.nn.gelu(y)
    gate = jnp.dot(y.astype(BF16), gw_ref[...], preferred_element_type=F32) + gb_ref[...]
    o_ref[0] = y * _sigmoid(gate)


def s5_prepare(lam_re, lam_im, log_step, b_re, b_im, c_re, c_im, nq):
    dt = jnp.exp(log_step.astype(F32))[:, None]
    mag = jnp.exp(lam_re * dt)
    lb_re = mag * jnp.cos(lam_im * dt)
    lb_im = mag * jnp.sin(lam_im * dt)
    den = lam_re * lam_re + lam_im * lam_im
    n_re = lb_re - 1.0
    f_re = (n_re * lam_re + lb_im * lam_im) / den
    f_im = (lb_im * lam_re - n_re * lam_im) / den
    bb_re = f_re[..., None] * b_re - f_im[..., None] * b_im
    bb_im = f_re[..., None] * b_im + f_im[..., None] * b_re
    eye = jnp.eye(S5_GROUPS, dtype=F32)
    bd_in = lambda t: jnp.einsum('gph,gk->ghkp', t, eye).reshape(GROUP_WIDTH, S5_FLAT).astype(BF16)
    bd_out = lambda t: jnp.einsum('ghp,gk->gpkh', t, eye).reshape(S5_FLAT, GROUP_WIDTH).astype(BF16)
    lam = jnp.stack([lb_re.reshape(-1), lb_im.reshape(-1)])

    def power(n):
        n = n[:, None, None]
        m = jnp.exp(n * (lam_re * dt)[None])
        a = n * (lam_im * dt)[None]
        return jnp.stack([(m * jnp.cos(a)).reshape(-1, S5_FLAT), (m * jnp.sin(a)).reshape(-1, S5_FLAT)])

    lam_q = power(jnp.arange(1, nq + 1, dtype=F32))
    lam_seg = power(jnp.full((1,), float(nq), F32))[:, 0, :]
    return bd_in(bb_re), bd_in(bb_im), bd_out(c_re), bd_out(c_im), lam, lam_q, lam_seg


def s5_mixer(u, prep, d_skip, glu_w, glu_b, chunk=512):
    bsz, seq, w = u.shape
    nq = chunk // SUBLANES
    nchunk = seq // chunk
    bre, bim, cre, cim, lam, lam_q, lam_seg = prep
    up = u.reshape(bsz, nchunk, SUBLANES, nq, w).transpose(0, 1, 3, 2, 4).reshape(bsz, seq, w)
    const2 = lambda b, i: (0, 0)
    out = pl.pallas_call(
        functools.partial(_s5_kernel, nq=nq),
        grid=(bsz, nchunk),
        in_specs=[pl.BlockSpec((1, chunk, w), lambda b, i: (b, i, 0)),
                  pl.BlockSpec((w, S5_FLAT), const2),
                  pl.BlockSpec((w, S5_FLAT), const2),
                  pl.BlockSpec((S5_FLAT, w), const2),
                  pl.BlockSpec((S5_FLAT, w), const2),
                  pl.BlockSpec((2, S5_FLAT), const2),
                  pl.BlockSpec((2, nq, S5_FLAT), lambda b, i: (0, 0, 0)),
                  pl.BlockSpec((2, S5_FLAT), const2),
                  pl.BlockSpec((1, w), const2),
                  pl.BlockSpec((w, w), const2),
                  pl.BlockSpec((1, w), const2)],
        out_specs=pl.BlockSpec((1, chunk, w), lambda b, i: (b, i, 0)),
        out_shape=jax.ShapeDtypeStruct((bsz, seq, w), F32),
        scratch_shapes=[pltpu.VMEM((chunk, S5_FLAT), F32),
                        pltpu.VMEM((chunk, S5_FLAT), F32),
                        pltpu.VMEM((2, S5_FLAT), F32)],
        compiler_params=_params("parallel", "arbitrary"),
        name="s5_mixer",
    )(up, bre, bim, cre, cim, lam, lam_q, lam_seg,
      d_skip.reshape(1, w), glu_w.astype(BF16), glu_b.reshape(1, w))
    return out.reshape(bsz, nchunk, nq, SUBLANES, w).transpose(0, 1, 3, 2, 4).reshape(bsz, seq, w)


def _ret_kernel(h_ref, cos_ref, sin_ref, intra_ref, qw_ref, kw_ref, dec_ref, ones_ref,
                o_ref, st_ref, *, chunk):
    @pl.when(pl.program_id(1) == 0)
    def _():
        st_ref[...] = jnp.zeros_like(st_ref)

    h = h_ref[0]
    cos = cos_ref[0]
    sin = sin_ref[0]
    nqk = RET_HEADS * RET_QK
    q = h[:, 0:nqk] * cos + h[:, 2 * nqk:3 * nqk] * sin
    k = (h[:, nqk:2 * nqk] * cos + h[:, 3 * nqk:4 * nqk] * sin) * (RET_QK ** -0.5)
    v = h[:, 4 * nqk:4 * nqk + GROUP_WIDTH]
    g = h[:, 4 * nqk + GROUP_WIDTH:]
    kb = k.astype(BF16)
    lane_qk = lax.broadcasted_iota(jnp.int32, (chunk, nqk), 1) // RET_QK
    lane_v = lax.broadcasted_iota(jnp.int32, (chunk, GROUP_WIDTH), 1) // RET_V
    state = st_ref[...]
    o = _bdot(q * qw_ref[...], state)
    for hd in range(RET_HEADS):
        qh = jnp.where(lane_qk == hd, q, 0.0).astype(BF16)
        s = lax.dot_general(qh, kb, (((1,), (1,)), ((), ())), preferred_element_type=F32)
        s = s * intra_ref[hd]
        vh = jnp.where(lane_v == hd, v, 0.0).astype(BF16)
        o = o + jnp.dot(s.astype(BF16), vh, preferred_element_type=F32)
    kv = lax.dot_general((k * kw_ref[...]).astype(BF16), v.astype(BF16),
                         (((0,), (0,)), ((), ())), preferred_element_type=F32)
    dec = dec_ref[...]
    st_ref[...] = state * dec + jnp.where(dec > 0.0, kv, 0.0)
    ms = _split_dot(o * o, ones_ref[...])
    o = o * lax.rsqrt(ms + NORM_EPS)
    o_ref[0] = o * (g * _sigmoid(g))


def retention_tables(chunk):
    log_gamma = np.log1p(-np.exp2(-5.0 - np.arange(RET_HEADS, dtype=np.float64)))
    idx = np.arange(chunk, dtype=np.float64)
    diff = idx[:, None] - idx[None, :]
    intra = np.where(diff >= 0, np.exp(np.maximum(diff, 0.0)[None] * log_gamma[:, None, None]), 0.0)
    q_w = np.repeat(np.exp((idx + 1.0)[:, None] * log_gamma[None, :]), RET_QK, axis=1)
    k_w = np.repeat(np.exp((chunk - 1.0 - idx)[:, None] * log_gamma[None, :]), RET_QK, axis=1)
    head_q = np.arange(RET_HEADS * RET_QK) // RET_QK
    head_v = np.arange(GROUP_WIDTH) // RET_V
    same = head_q[:, None] == head_v[None, :]
    dec = np.where(same, np.exp(chunk * log_gamma)[head_q][:, None], 0.0)
    ones = (head_v[:, None] == head_v[None, :]).astype(np.float64) / RET_V
    f = lambda a: jnp.asarray(a, F32)
    return f(intra), f(q_w), f(k_w), f(dec), jnp.asarray(ones, BF16)


def retention_mixer(h, cos_t, sin_t, chunk=256):
    bsz, seq, wh = h.shape
    intra, q_w, k_w, dec, ones = retention_tables(chunk)
    nqk = RET_HEADS * RET_QK
    row = lambda b, i: (b, i, 0)
    c2 = lambda b, i: (0, 0)
    return pl.pallas_call(
        functools.partial(_ret_kernel, chunk=chunk),
        grid=(bsz, seq // chunk),
        in_specs=[pl.BlockSpec((1, chunk, wh), row),
                  pl.BlockSpec((1, chunk, LANES), row),
                  pl.BlockSpec((1, chunk, LANES), row),
                  pl.BlockSpec((RET_HEADS, chunk, chunk), lambda b, i: (0, 0, 0)),
                  pl.BlockSpec((chunk, nqk), c2),
                  pl.BlockSpec((chunk, nqk), c2),
                  pl.BlockSpec((nqk, GROUP_WIDTH), c2),
                  pl.BlockSpec((GROUP_WIDTH, GROUP_WIDTH), c2)],
        out_specs=pl.BlockSpec((1, chunk, GROUP_WIDTH), row),
        out_shape=jax.ShapeDtypeStruct((bsz, seq, GROUP_WIDTH), F32),
        scratch_shapes=[pltpu.VMEM((nqk, GROUP_WIDTH), F32)],
        compiler_params=_params("parallel", "arbitrary"),
        name="retention",
    )(h, cos_t, sin_t, intra, q_w, k_w, dec, ones)


def _mla_prep_kernel(h_ref, cos_ref, sin_ref, qg_ref, kvg_ref, wqa_ref, wqb_ref, wk_ref, wv_ref,
                     q_ref, k_ref, v_ref, *, scale):
    h = h_ref[0]
    tm = h.shape[0]
    lane = lax.broadcasted_iota(jnp.int32, (tm, LANES), 1)
    is_nope = lane < MLA_NOPE
    is_rope = jnp.logical_and(lane >= MLA_NOPE, lane < MLA_NOPE + MLA_ROPE)
    cm = jnp.where(is_nope, 1.0, jnp.where(is_rope, cos_ref[0], 0.0))
    sm = jnp.where(is_rope, sin_ref[0], 0.0)

    qc = h[:, 0:MLA_Q_RANK]
    qn = (qc * lax.rsqrt(jnp.mean(qc * qc, axis=-1, keepdims=True) + NORM_EPS) * qg_ref[...]).astype(BF16)
    kvc = h[:, MLA_Q_RANK:MLA_Q_RANK + MLA_KV_RANK]
    kvn = (kvc * lax.rsqrt(jnp.mean(kvc * kvc, axis=-1, keepdims=True) + NORM_EPS) * kvg_ref[...]).astype(BF16)
    off = MLA_Q_RANK + MLA_KV_RANK
    kpe = h[:, off:off + LANES] * cm + h[:, off + LANES:off + 2 * LANES] * sm

    qa = jnp.dot(qn, wqa_ref[...], preferred_element_type=F32)
    qb = jnp.dot(qn, wqb_ref[...], preferred_element_type=F32)
    kn = jnp.dot(kvn, wk_ref[...], preferred_element_type=F32)
    v_ref[0] = jnp.dot(kvn, wv_ref[...], preferred_element_type=F32).astype(BF16)
    for hd in range(MLA_HEADS):
        sl = slice(hd * LANES, (hd + 1) * LANES)
        q_ref[0, hd] = ((qa[:, sl] * cm + qb[:, sl] * sm) * scale).astype(BF16)
        k_ref[0, hd] = (kn[:, sl] + kpe).astype(BF16)


def mla_pack_weights(w_q_up, w_kv_up):
    dq = MLA_NOPE + MLA_ROPE
    zq = w_q_up.shape[1]
    zk = w_kv_up.shape[1]
    z = lambda n, zero: np.full((n,), zero)
    ia, ib, ik, iv = [], [], [], []
    for hd in range(MLA_HEADS):
        nope = np.arange(hd * dq, hd * dq + MLA_NOPE)
        pe = np.arange(hd * dq + MLA_NOPE, (hd + 1) * dq)
        pad = LANES - dq
        ia += [nope, pe, z(pad, zq)]
        ib += [z(MLA_NOPE, zq), _swap_halves(pe, MLA_ROPE), z(pad, zq)]
        kv0 = hd * (MLA_NOPE + MLA_V)
        ik += [np.arange(kv0, kv0 + MLA_NOPE), z(LANES - MLA_NOPE, zk)]
        iv += [np.arange(kv0 + MLA_NOPE, kv0 + MLA_NOPE + MLA_V)]
    ext = lambda w: jnp.concatenate([w, jnp.zeros((w.shape[0], 1), w.dtype)], axis=1)
    take = lambda w, idx: jnp.take(ext(w), jnp.asarray(np.concatenate(idx), jnp.int32), axis=1).astype(BF16)
    return take(w_q_up, ia), take(w_q_up, ib), take(w_kv_up, ik), take(w_kv_up, iv)


def mla_prep(h, cos_t, sin_t, q_norm_g, kv_norm_g, packed, tm=512):
    bsz, seq, wh = h.shape
    wqa, wqb, wk, wv = packed
    row = lambda b, i: (b, i, 0)
    c2 = lambda b, i: (0, 0)
    hrow = lambda b, i: (b, 0, i, 0)
    scale = (MLA_NOPE + MLA_ROPE) ** -0.5
    return pl.pallas_call(
        functools.partial(_mla_prep_kernel, scale=scale),
        grid=(bsz, seq // tm),
        in_specs=[pl.BlockSpec((1, tm, wh), row),
                  pl.BlockSpec((1, tm, LANES), row),
                  pl.BlockSpec((1, tm, LANES), row),
                  pl.BlockSpec((1, MLA_Q_RANK), c2),
                  pl.BlockSpec((1, MLA_KV_RANK), c2),
                  pl.BlockSpec(wqa.shape, c2),
                  pl.BlockSpec(wqb.shape, c2),
                  pl.BlockSpec(wk.shape, c2),
                  pl.BlockSpec(wv.shape, c2)],
        out_specs=[pl.BlockSpec((1, MLA_HEADS, tm, LANES), hrow),
                   pl.BlockSpec((1, MLA_HEADS, tm, LANES), hrow),
                   pl.BlockSpec((1, tm, GROUP_WIDTH), row)],
        out_shape=[jax.ShapeDtypeStruct((bsz, MLA_HEADS, seq, LANES), BF16),
                   jax.ShapeDtypeStruct((bsz, MLA_HEADS, seq, LANES), BF16),
                   jax.ShapeDtypeStruct((bsz, seq, GROUP_WIDTH), BF16)],
        compiler_params=_params("parallel", "parallel"),
        name="mla_prep",
    )(h, cos_t, sin_t, q_norm_g.reshape(1, -1), kv_norm_g.reshape(1, -1), wqa, wqb, wk, wv)


def _attn_kernel(q_ref, k_ref, v_ref, o_ref, m_ref, l_ref, acc_ref, *, blk):
    qi = pl.program_id(2)
    rep = blk // LANES

    heads = range(2)

    def step(j, masked):
        rows = pl.ds(pl.multiple_of(j * blk, blk), blk)
        vs = v_ref[0, rows, :]
        s = [lax.dot_general(q_ref[0, hh], k_ref[0, hh, rows, :], (((1,), (1,)), ((), ())),
                             preferred_element_type=F32) for hh in heads]
        if masked:
            r = lax.broadcasted_iota(jnp.int32, (blk, blk), 0)
            c = lax.broadcasted_iota(jnp.int32, (blk, blk), 1)
            s = [jnp.where(c <= r, t, -jnp.inf) for t in s]
        m_prev = [m_ref[hh] for hh in heads]
        m_new = [jnp.maximum(m_prev[hh], jnp.max(s[hh], axis=-1, keepdims=True)) for hh in heads]
        p = [jnp.exp(s[hh] - jnp.concatenate([m_new[hh]] * rep, axis=1)) for hh in heads]
        alpha = [jnp.exp(m_prev[hh] - m_new[hh]) for hh in heads]
        for hh in heads:
            l_ref[hh] = alpha[hh] * l_ref[hh] + jnp.sum(p[hh], axis=-1, keepdims=True)
            acc_ref[hh] = alpha[hh] * acc_ref[hh] + jnp.dot(p[hh].astype(BF16), vs, preferred_element_type=F32)
            m_ref[hh] = m_new[hh]

    for hh in heads:
        m_ref[hh] = jnp.full((blk, LANES), -jnp.inf, F32)
        l_ref[hh] = jnp.zeros((blk, LANES), F32)
        acc_ref[hh] = jnp.zeros((blk, LANES), F32)

    def body(j, _):
        step(j, False)
        return 0

    lax.fori_loop(0, qi, body, 0)
    step(qi, True)

    lane = lax.broadcasted_iota(jnp.int32, (blk, LANES), 1)
    o0 = acc_ref[0] / l_ref[0]
    o1 = acc_ref[1] / l_ref[1]
    o_ref[0] = jnp.where(lane < MLA_V, o0, o1)


def causal_attention(q, k, v, blk=512):
    bsz, nh, seq, dk = q.shape
    return pl.pallas_call(
        functools.partial(_attn_kernel, blk=blk),
        grid=(bsz, nh // 2, seq // blk),
        in_specs=[pl.BlockSpec((1, 2, blk, dk), lambda b, p, i: (b, p, i, 0)),
                  pl.BlockSpec((1, 2, seq, dk), lambda b, p, i: (b, p, 0, 0)),
                  pl.BlockSpec((1, seq, LANES), lambda b, p, i: (b, 0, p))],
        out_specs=pl.BlockSpec((1, blk, LANES), lambda b, p, i: (b, i, p)),
        out_shape=jax.ShapeDtypeStruct((bsz, seq, nh // 2 * LANES), F32),
        scratch_shapes=[pltpu.VMEM((2, blk, LANES), F32)] * 3,
        compiler_params=_params("parallel", "parallel", "arbitrary"),
        name="mla_attention",
    )(q, k, v)


def _rw_prep_kernel(*refs, first):
    if first:
        (h_ref, prev_ref, mu_ref, w0_ref, w2_ref, a0_ref, a2_ref, g2_ref, kk_ref, ka_ref, rk_ref, ones_ref,
         r_out, lw_out, k_out, kk_out, b_out, v_ref, bonus_ref, g_ref) = refs
    else:
        (h_ref, prev_ref, mu_ref, w0_ref, w2_ref, a0_ref, a2_ref, g2_ref, kk_ref, ka_ref, rk_ref, ones_ref,
         vf_ref, v0_ref, v1_ref, v2_ref,
         r_out, lw_out, k_out, kk_out, b_out, v_ref, bonus_ref, g_ref) = refs
    h = h_ref[0]
    tm = h.shape[0]
    last = prev_ref[0, SUBLANES - 1:SUBLANES, :]
    last = jnp.where(pl.program_id(1) == 0, 0.0, last)
    row = lax.broadcasted_iota(jnp.int32, h.shape, 0)
    h_prev = jnp.where(row == 0, last, pltpu.roll(h, 1, axis=0))
    h = h + (h_prev - h) * mu_ref[...]
    w = GROUP_WIDTH
    r = h[:, 0:w]
    k = h[:, w:2 * w]
    v = h[:, 2 * w:3 * w]
    wa = h[:, 3 * w:3 * w + LANES]
    gd = h[:, 3 * w + LANES:]
    z = w0_ref[...] + _bdot(jnp.tanh(wa), w2_ref[...])
    nz = -z
    softplus = jnp.maximum(nz, 0.0) + jnp.log(1.0 + jnp.exp(-jnp.abs(nz)))
    log_decay = -jnp.exp(-softplus - 0.5)
    a = _sigmoid(a0_ref[...] + _bdot(wa, a2_ref[...]))
    g_ref[0] = _bdot(_sigmoid(gd), g2_ref[...])
    if not first:
        mix = _sigmoid(v0_ref[...] + _bdot(_bdot(v, v1_ref[...]), v2_ref[...]))
        v = v + (vf_ref[0] - v) * mix
    kk = k * kk_ref[...]
    norm = jnp.sqrt(_split_dot(kk * kk, ones_ref[...]))
    kk = kk / jnp.maximum(norm, 1e-12)
    k = k * (1.0 + (a - 1.0) * ka_ref[...])
    v_ref[0] = v
    bonus_ref[0] = _split_dot(r * k * rk_ref[...], ones_ref[...]) * v
    r_out[0] = r
    lw_out[0] = log_decay
    k_out[0] = k
    kk_out[0] = kk
    b_out[0] = kk * a


def rw_pack_weights(w2, a2, v1=None, v2=None):
    zeros = lambda n, m: jnp.zeros((n, m), F32)
    half = LANES // 2
    w2p = jnp.concatenate([w2, zeros(half, GROUP_WIDTH)], axis=0).astype(BF16)
    a2p = jnp.concatenate([zeros(half, GROUP_WIDTH), a2], axis=0).astype(BF16)
    if v1 is None:
        return w2p, a2p
    v1p = jnp.concatenate([v1, zeros(GROUP_WIDTH, LANES - v1.shape[1])], axis=1).astype(BF16)
    v2p = jnp.concatenate([v2, zeros(LANES - v2.shape[0], GROUP_WIDTH)], axis=0).astype(BF16)
    return w2p, a2p, v1p, v2p


def head_ones(width, head, scale):
    hd = np.arange(width) // head
    return jnp.asarray((hd[:, None] == hd[None, :]).astype(np.float32) * scale, BF16)


def rw_prep(h, mu, w0, w2p, a0, a2p, g2, k_k, k_a, r_k, vmix=None, tm=512):
    bsz, seq, wh = h.shape
    w = GROUP_WIDTH
    first = vmix is None
    row = lambda b, i: (b, i, 0)
    c2 = lambda b, i: (0, 0)
    vec = lambda t: t.reshape(1, -1)
    prev_map = lambda b, i: (b, jnp.maximum(i * (tm // SUBLANES) - 1, 0), 0)
    args = [h, h, vec(mu), vec(w0), w2p, vec(a0), a2p, g2.astype(BF16), vec(k_k), vec(k_a), vec(r_k),
            head_ones(w, RW_HEAD, 1.0)]
    specs = [pl.BlockSpec((1, tm, wh), row), pl.BlockSpec((1, SUBLANES, wh), prev_map),
             pl.BlockSpec((1, wh), c2), pl.BlockSpec((1, w), c2), pl.BlockSpec((LANES, w), c2),
             pl.BlockSpec((1, w), c2), pl.BlockSpec((LANES, w), c2), pl.BlockSpec((LANES, w), c2),
             pl.BlockSpec((1, w), c2), pl.BlockSpec((1, w), c2), pl.BlockSpec((1, w), c2),
             pl.BlockSpec((w, w), c2)]
    if not first:
        v_first, v0, v1p, v2p = vmix
        args += [v_first, vec(v0), v1p, v2p]
        specs += [pl.BlockSpec((1, tm, w), row), pl.BlockSpec((1, w), c2),
                  pl.BlockSpec((w, LANES), c2), pl.BlockSpec((LANES, w), c2)]
    return pl.pallas_call(
        functools.partial(_rw_prep_kernel, first=first),
        grid=(bsz, seq // tm),
        in_specs=specs,
        out_specs=[pl.BlockSpec((1, tm, w), row)] * 8,
        out_shape=[jax.ShapeDtypeStruct((bsz, seq, w), F32)] * 8,
        compiler_params=_params("parallel", "parallel"),
        name="rwkv_prep",
    )(*args)


WKV_CHUNK = 64
_NN = (((1,), (0,)), ((), ()))
_NT = (((1,), (1,)), ((), ()))
_TN = (((0,), (0,)), ((), ()))


def _split(a):
    hi = a.astype(BF16)
    return hi, (a - hi.astype(F32)).astype(BF16)


def _mm(a, b, dims=_NN, passes=1):
    if passes == 1:
        return lax.dot_general(a.astype(BF16), b.astype(BF16), dims, preferred_element_type=F32)
    ah, al = _split(a)
    bh, bl = _split(b)
    d = lambda x, y: lax.dot_general(x, y, dims, preferred_element_type=F32)
    return d(ah, bh) + (d(ah, bl) + d(al, bh))


def _wkv_pre_kernel(r_ref, lw_ref, k_ref, kk_ref, b_ref, v_ref, tri_ref,
                    q_ref, y0_ref, phi_ref, dlt_ref, *, nchunk):
    c = WKV_CHUNK
    n2 = 2 * c
    lane = lax.broadcasted_iota(jnp.int32, (c, LANES), 1)
    head0 = lane < RW_HEAD
    row = lax.broadcasted_iota(jnp.int32, (n2, n2), 0)
    col = lax.broadcasted_iota(jnp.int32, (n2, n2), 1)
    strict = col < row
    incl = col <= row
    eye = (row == col).astype(F32)
    stack = lambda x: jnp.concatenate([jnp.where(head0, x, 0.0), jnp.where(head0, 0.0, x)], axis=0)
    fold = lambda x: x[:c] + x[c:]
    tri = tri_ref[...]
    chains = [(ci, p) for ci in range(nchunk) for p in range(GROUP_WIDTH // LANES)]
    st = []
    for ci, p in chains:
        sl = slice(p * LANES, (p + 1) * LANES)
        rows = slice(ci * c, (ci + 1) * c)
        lw = lw_ref[0, rows, sl]
        lw_hi, lw_lo = _split(lw)
        cum = (jnp.dot(tri, lw_hi, preferred_element_type=F32)
               + jnp.dot(tri, lw_lo, preferred_element_type=F32))
        last = cum[c - 1:c, :]
        e_in = jnp.exp(cum)
        e_ex = jnp.exp(cum - lw)
        e_neg = jnp.exp(-cum)
        e_end = jnp.exp(last - cum)
        kk, r, k, b, v = kk_ref[0, rows, sl], r_ref[0, rows, sl], k_ref[0, rows, sl], b_ref[0, rows, sl], v_ref[0, rows, sl]
        kt2 = stack(kk * e_ex)
        rt2 = stack(r * e_in)
        lhs = jnp.concatenate([kt2, rt2], axis=0)
        gb = _mm(lhs, stack(b * e_neg), _NT)
        gk = _mm(lhs, stack(k * e_neg), _NT)
        x = -jnp.where(strict, gb[:n2], 0.0)
        st.append(dict(sl=sl, rows=rows, kt2=kt2, rt2=rt2, kp2=stack(k * e_end), bp2=stack(b * e_end), v2=stack(v),
                       g2=jnp.where(strict, gk[:n2], 0.0), rb2=jnp.where(incl, gb[n2:], 0.0),
                       rk2=jnp.where(incl, gk[n2:], 0.0), last=last, x=x, t=eye + x))
    for _ in range(int(np.log2(c)) - 1):
        for d in st:
            d["x"] = _mm(d["x"], d["x"])
        for d in st:
            d["t"] = d["t"] + _mm(d["t"], d["x"])
    for d in st:
        d["gv2"] = _mm(d["g2"], d["v2"])
    for d in st:
        d["tku"] = _mm(d["t"], jnp.concatenate([d["kt2"], d["gv2"]], axis=1))
    for d in st:
        d["rbz"] = _mm(d["rb2"], d["tku"])
    for (ci, p), d in zip(chains, st):
        sl, rows, tku, rbz = d["sl"], d["rows"], d["tku"], d["rbz"]
        q_ref[0, rows, sl] = fold(d["rt2"] - rbz[:, :LANES])
        y0_ref[0, rows, sl] = fold(_mm(d["rk2"], d["v2"]) - rbz[:, LANES:])
        decay = jnp.where(row == col, jnp.broadcast_to(jnp.exp(d["last"]), (n2, LANES)), 0.0)
        phi_ref[0, ci, p] = decay - _mm(d["bp2"], tku[:, :LANES], _TN)
        dlt_ref[0, ci, p] = _mm(d["kp2"], d["v2"], _TN) - _mm(d["bp2"], tku[:, LANES:], _TN)


def wkv_precompute(r, lw, k, kk, b, v, block=256):
    bsz, seq, w = r.shape
    c = WKV_CHUNK
    nchunk = block // c
    npair = w // LANES
    tri = jnp.asarray(np.tril(np.ones((c, c), np.float32)), BF16)
    row = lambda bb, i: (bb, i, 0)
    blk = pl.BlockSpec((1, block, w), row)
    mat = pl.BlockSpec((1, nchunk, npair, LANES, LANES), lambda bb, i: (bb, i, 0, 0, 0))
    mshape = jax.ShapeDtypeStruct((bsz, seq // c, npair, LANES, LANES), F32)
    return pl.pallas_call(
        functools.partial(_wkv_pre_kernel, nchunk=nchunk),
        grid=(bsz, seq // block),
        in_specs=[blk] * 6 + [pl.BlockSpec((c, c), lambda bb, i: (0, 0))],
        out_specs=[blk, blk, mat, mat],
        out_shape=[jax.ShapeDtypeStruct((bsz, seq, w), F32)] * 2 + [mshape, mshape],
        compiler_params=_params("parallel", "parallel"),
        name="wkv7_chunk_prep",
    )(r, lw, k, kk, b, v, tri)


def _wkv_seq_kernel(q_ref, y0_ref, phi_ref, dlt_ref, y_ref, st_ref, *, nchunk):
    @pl.when(pl.program_id(1) == 0)
    def _():
        st_ref[...] = jnp.zeros_like(st_ref)

    c = WKV_CHUNK
    npair = GROUP_WIDTH // LANES
    m = [st_ref[p] for p in range(npair)]
    for i in range(nchunk):
        rows = slice(i * c, (i + 1) * c)
        for p in range(npair):
            sl = slice(p * LANES, (p + 1) * LANES)
            y_ref[0, rows, sl] = _mm(q_ref[0, rows, sl], m[p], _NN, 3) + y0_ref[0, rows, sl]
        m = [_mm(phi_ref[0, i, p], m[p], _NN, 3) + dlt_ref[0, i, p] for p in range(npair)]
    for p in range(npair):
        st_ref[p] = m[p]


def wkv_sequential(q, y0, phi, dlt, block=512):
    bsz, seq, w = q.shape
    c = WKV_CHUNK
    nchunk = block // c
    npair = w // LANES
    row = lambda bb, i: (bb, i, 0)
    blk = pl.BlockSpec((1, block, w), row)
    mat = pl.BlockSpec((1, nchunk, npair, LANES, LANES), lambda bb, i: (bb, i, 0, 0, 0))
    return pl.pallas_call(
        functools.partial(_wkv_seq_kernel, nchunk=nchunk),
        grid=(bsz, seq // block),
        in_specs=[blk, blk, mat, mat],
        out_specs=blk,
        out_shape=jax.ShapeDtypeStruct((bsz, seq, w), F32),
        scratch_shapes=[pltpu.VMEM((npair, LANES, LANES), F32)],
        compiler_params=_params("parallel", "arbitrary"),
        name="wkv7_chunk_scan",
    )(q, y0, phi, dlt)


def _outproj_kernel(x_ref, ys5_ref, yrw_ref, bonus_ref, grw_ref, ymla_ref, yret_ref, lng_ref, lnb_ref, ones_ref,
                    wout_ref, g1_ref, ng_ref, sc_ref, sh_ref, rhi_ref, rlo_ref, rb_ref,
                    xo_ref, hn_ref, idx_ref, tw_ref, cnt_ref):
    y = yrw_ref[0]
    mean = _split_dot(y, ones_ref[...])
    yc = y - mean
    var = _split_dot(yc * yc, ones_ref[...])
    yrw = (yc * lax.rsqrt(var + RW_GN_EPS) * lng_ref[...] + lnb_ref[...] + bonus_ref[0]) * grw_ref[0]
    w = GROUP_WIDTH
    mixed = (_bdot(ys5_ref[0], wout_ref[0:w, :]) + _bdot(yrw, wout_ref[w:2 * w, :])
             + _bdot(ymla_ref[0], wout_ref[2 * w:3 * w, :]) + _bdot(yret_ref[0], wout_ref[3 * w:4 * w, :]))
    x = x_ref[0] + g1_ref[0] * mixed
    xo_ref[0] = x
    ms = jnp.mean(x * x, axis=-1, keepdims=True)
    hn = x * lax.rsqrt(ms + NORM_EPS) * ng_ref[...]
    hn = hn * (1.0 + sc_ref[0]) + sh_ref[0]
    hi = hn.astype(BF16)
    hn_ref[0] = hi
    lo = (hn - hi.astype(F32)).astype(BF16)
    logits = (jnp.dot(hi, rhi_ref[...], preferred_element_type=F32)
              + jnp.dot(lo, rhi_ref[...], preferred_element_type=F32)
              + jnp.dot(hi, rlo_ref[...], preferred_element_type=F32) + rb_ref[...])
    lane = lax.broadcasted_iota(jnp.int32, logits.shape, 1)
    lane_f = lane.astype(F32)
    cur = jnp.where(lane < N_EXPERTS, logits, -jnp.inf)
    idx_out = jnp.zeros(logits.shape, F32)
    val_out = jnp.zeros(logits.shape, F32)
    picked = jnp.zeros(logits.shape, F32)
    top = None
    denom = None
    for j in range(TOP_K):
        m = jnp.max(cur, axis=-1, keepdims=True)
        sel = jnp.min(jnp.where(cur == m, lane_f, float(LANES)), axis=-1, keepdims=True)
        hit = lane_f == sel
        cur = jnp.where(hit, -jnp.inf, cur)
        picked = picked + jnp.where(hit, 1.0, 0.0)
        top = m if top is None else top
        e = jnp.exp(m - top)
        denom = e if denom is None else denom + e
        idx_out = jnp.where(lane == j, sel, idx_out)
        val_out = jnp.where(lane == j, e, val_out)
    idx_ref[0] = idx_out.astype(jnp.int32)
    tw_ref[0] = val_out / denom
    cnt_ref[0, 0] = jnp.broadcast_to(jnp.sum(picked, axis=0, keepdims=True), (SUBLANES, LANES))


def out_proj(x, y_s5, y_rw, bonus, g_rw, y_mla, y_ret, ln_g, ln_b, w_out, g1, norm_g, sc2, sh2,
             router_w, router_b, tm=512):
    bsz, seq, d = x.shape
    w = GROUP_WIDTH
    row = lambda b, i: (b, i, 0)
    per_b = lambda b, i: (b, 0, 0)
    c2 = lambda b, i: (0, 0)
    vec = lambda t: t.reshape(1, -1)
    pad = LANES - router_w.shape[1]
    rw_pad = jnp.concatenate([router_w, jnp.zeros((d, pad), F32)], axis=1)
    r_hi = rw_pad.astype(BF16)
    r_lo = (rw_pad - r_hi.astype(F32)).astype(BF16)
    rb = jnp.concatenate([router_b, jnp.zeros((pad,), F32)]).reshape(1, LANES)
    mixer = pl.BlockSpec((1, tm, w), row)
    return pl.pallas_call(
        _outproj_kernel,
        grid=(bsz, seq // tm),
        in_specs=[pl.BlockSpec((1, tm, d), row), mixer, mixer, mixer, mixer, mixer, mixer,
                  pl.BlockSpec((1, w), c2), pl.BlockSpec((1, w), c2), pl.BlockSpec((w, w), c2),
                  pl.BlockSpec((4 * w, d), c2),
                  pl.BlockSpec((1, 1, d), per_b), pl.BlockSpec((1, d), c2),
                  pl.BlockSpec((1, 1, d), per_b), pl.BlockSpec((1, 1, d), per_b),
                  pl.BlockSpec((d, LANES), c2), pl.BlockSpec((d, LANES), c2), pl.BlockSpec((1, LANES), c2)],
        out_specs=[pl.BlockSpec((1, tm, d), row), pl.BlockSpec((1, tm, d), row), pl.BlockSpec((1, tm, LANES), row),
                   pl.BlockSpec((1, tm, LANES), row), pl.BlockSpec((1, 1, SUBLANES, LANES), lambda b, i: (b, i, 0, 0))],
        out_shape=[jax.ShapeDtypeStruct((bsz, seq, d), F32), jax.ShapeDtypeStruct((bsz, seq, d), BF16),
                   jax.ShapeDtypeStruct((bsz, seq, LANES), jnp.int32), jax.ShapeDtypeStruct((bsz, seq, LANES), F32),
                   jax.ShapeDtypeStruct((bsz, seq // tm, SUBLANES, LANES), F32)],
        compiler_params=_params("parallel", "parallel"),
        name="out_proj",
    )(x, y_s5, y_rw, bonus, g_rw, y_mla, y_ret, vec(ln_g), vec(ln_b), head_ones(w, RW_HEAD, 1.0 / RW_HEAD),
      w_out.astype(BF16), g1, vec(norm_g), sc2, sh2, r_hi, r_lo, rb)


MOE_ROWS = 512
COMBINE_ROWS = 256


def _moe_kernel(blk_e_ref, blk_on_ref, x_ref, rw_ref, wg_ref, bg_ref, wu_ref, bu_ref, wd_ref, bd_ref, o_ref, wb_ref):
    i = pl.program_id(0)
    changed = jnp.logical_or(i == 0, blk_e_ref[i] != blk_e_ref[jnp.maximum(i - 1, 0)])

    @pl.when(changed)
    def _():
        wb_ref[0] = wg_ref[0, 0].astype(BF16)
        wb_ref[1] = wu_ref[0, 0].astype(BF16)
        wb_ref[2] = wd_ref[0, 0].astype(BF16)

    @pl.when(blk_on_ref[i] > 0)
    def _():
        x = x_ref[...]
        gt = jnp.minimum(jnp.dot(x, wb_ref[0], preferred_element_type=F32) + bg_ref[0, 0], SWIGLU_LIMIT)
        up = jnp.clip(jnp.dot(x, wb_ref[1], preferred_element_type=F32) + bu_ref[0, 0], -SWIGLU_LIMIT, SWIGLU_LIMIT)
        act = gt * _sigmoid(SWIGLU_ALPHA * gt) * (up + 1.0)
        y = jnp.dot(act.astype(BF16), wb_ref[2], preferred_element_type=F32) + bd_ref[0, 0]
        o_ref[...] = (y * rw_ref[...]).astype(BF16)


def moe_experts(xb, row_w, blk_e, blk_on, layer, w_gate, b_gate, w_up, b_up, w_down, b_down):
    p_rows, d = xb.shape
    depth, n_e, _, de = w_gate.shape
    wmap = lambda i, e, on: (layer, e[i], 0, 0)
    rows = lambda i, e, on: (i, 0)
    return pl.pallas_call(
        _moe_kernel,
        grid_spec=pltpu.PrefetchScalarGridSpec(
            num_scalar_prefetch=2,
            grid=(p_rows // MOE_ROWS,),
            in_specs=[pl.BlockSpec((MOE_ROWS, d), rows), pl.BlockSpec((MOE_ROWS, 1), rows),
                      pl.BlockSpec((1, 1, d, de), wmap), pl.BlockSpec((1, 1, 1, de), wmap),
                      pl.BlockSpec((1, 1, d, de), wmap), pl.BlockSpec((1, 1, 1, de), wmap),
                      pl.BlockSpec((1, 1, de, d), wmap), pl.BlockSpec((1, 1, 1, d), wmap)],
            out_specs=pl.BlockSpec((MOE_ROWS, d), rows),
            scratch_shapes=[pltpu.VMEM((3, d, de), BF16)]),
        out_shape=jax.ShapeDtypeStruct((p_rows, d), BF16),
        compiler_params=_params("arbitrary"),
        name="moe_experts",
    )(blk_e, blk_on, xb, row_w, w_gate, b_gate.reshape(depth, n_e, 1, de), w_up, b_up.reshape(depth, n_e, 1, de),
      w_down, b_down.reshape(depth, n_e, 1, d))


def moe_route(top_idx, top_w, counts):
    t = top_idx.shape[0]
    n_assign = t * TOP_K
    flat_e = top_idx.reshape(-1)
    iota = jnp.arange(n_assign, dtype=jnp.int32)
    sorted_e, order, w_sorted = lax.sort((flat_e, iota, top_w.reshape(-1)), num_keys=1)
    start = jnp.cumsum(counts) - counts
    padded = (counts + MOE_ROWS - 1) // MOE_ROWS * MOE_ROWS
    pad_end = jnp.cumsum(padded)
    pad_start = pad_end - padded
    pos_sorted = iota + (pad_start - start)[sorted_e]
    p_rows = n_assign + N_EXPERTS * MOE_ROWS
    n_blocks = p_rows // MOE_ROWS
    blk_first = jnp.arange(n_blocks, dtype=jnp.int32) * MOE_ROWS
    blk_e = jnp.minimum(jnp.sum(pad_end[None, :] <= blk_first[:, None], axis=1, dtype=jnp.int32), N_EXPERTS - 1)
    blk_on = (blk_first < pad_start[blk_e] + counts[blk_e]).astype(jnp.int32)
    row = jnp.arange(p_rows, dtype=jnp.int32)
    row_e = jnp.repeat(blk_e, MOE_ROWS)
    within = row - pad_start[row_e]
    valid = within < counts[row_e]
    src = jnp.clip(start[row_e] + within, 0, n_assign - 1)
    buf_tok = jnp.where(valid, order[src] // TOP_K, 0)
    row_w = jnp.where(valid, w_sorted[src], 0.0).reshape(p_rows, 1)
    tok = order // TOP_K
    key = (tok // COMBINE_ROWS) * N_EXPERTS + sorted_e
    _, gather_rows, tok_local = lax.sort((key, pos_sorted, tok % COMBINE_ROWS), num_keys=1)
    return buf_tok, row_w, blk_e, blk_on, gather_rows, tok_local


def moe_ffn(hn_bf16, top_idx, top_w, counts, layer, w_gate, b_gate, w_up, b_up, w_down, b_down):
    bsz, seq, d = hn_bf16.shape
    t = bsz * seq
    buf_tok, row_w, blk_e, blk_on, gather_rows, tok_local = moe_route(
        top_idx.reshape(t, LANES)[:, :TOP_K], top_w.reshape(t, LANES)[:, :TOP_K], counts)
    xb = hn_bf16.reshape(t, d).at[buf_tok].get(mode='promise_in_bounds')
    yb = moe_experts(xb, row_w, blk_e, blk_on, layer, w_gate, b_gate, w_up, b_up, w_down, b_down)
    yg = yb.at[gather_rows].get(mode='promise_in_bounds')
    n_tiles = t // COMBINE_ROWS
    return (yg.reshape(n_tiles, TOP_K * COMBINE_ROWS, d), tok_local.reshape(n_tiles, 1, TOP_K * COMBINE_ROWS))


def _combine_kernel(*refs, final):
    if final:
        x_ref, y_ref, tl_ref, g_ref, ng_ref, o_ref = refs
    else:
        x_ref, y_ref, tl_ref, g_ref, o_ref = refs
    tm = x_ref.shape[1]
    tok = lax.broadcasted_iota(jnp.int32, (tm, TOP_K * tm), 0)
    sel = jnp.where(tl_ref[0] == tok, 1.0, 0.0).astype(BF16)
    y = jnp.dot(sel, y_ref[0], preferred_element_type=F32)
    x = x_ref[0] + g_ref[0] * y
    if final:
        x = x * lax.rsqrt(jnp.mean(x * x, axis=-1, keepdims=True) + NORM_EPS) * ng_ref[...]
    o_ref[0] = x


def moe_combine(x, yg, tok_local, gate, final_g=None):
    bsz, seq, d = x.shape
    tm = COMBINE_ROWS
    per_b = seq // tm
    row = lambda b, i: (b, i, 0)
    tile = lambda b, i: (b * per_b + i, 0, 0)
    specs = [pl.BlockSpec((1, tm, d), row),
             pl.BlockSpec((1, TOP_K * tm, d), tile),
             pl.BlockSpec((1, 1, TOP_K * tm), tile),
             pl.BlockSpec((1, 1, d), lambda b, i: (b, 0, 0))]
    args = [x, yg, tok_local, gate]
    if final_g is not None:
        specs.append(pl.BlockSpec((1, d), lambda b, i: (0, 0)))
        args.append(final_g.reshape(1, d))
    return pl.pallas_call(
        functools.partial(_combine_kernel, final=final_g is not None),
        grid=(bsz, per_b),
        in_specs=specs,
        out_specs=pl.BlockSpec((1, tm, d), row),
        out_shape=jax.ShapeDtypeStruct((bsz, seq, d), F32),
        compiler_params=_params("parallel", "parallel"),
        name="moe_combine",
    )(*args)


S5_CHUNK = 512


def kernel(x, c, positions, ada_w, ada_b, norm_mix_g, norm_ffn_g, w_in, w_out,
           s5_lambda_re, s5_lambda_im, s5_log_step, s5_b_re, s5_b_im, s5_c_re, s5_c_im,
           s5_d, s5_glu_w, s5_glu_b,
           rw_mu, rw_w0, rw_w2, rw_a0, rw_a2, rw_g2, rw_k_k, rw_k_a, rw_r_k, rw_ln_g, rw_ln_b,
           rw_v0, rw_v1, rw_v2,
           mla_q_norm_g, mla_kv_norm_g, mla_w_q_up, mla_w_kv_up,
           router_w, router_b, ex_w_gate, ex_b_gate, ex_w_up, ex_b_up, ex_w_down, ex_b_down,
           final_norm_g):
    depth = w_in.shape[0]
    mod = adaln_mod(c, ada_w, ada_b)
    cos_t, sin_t = rope_tables(positions)
    v_first = None
    for l in range(depth):
        sh1, sc1, g1, sh2, sc2, g2 = [m[:, None, :] for m in jnp.split(mod[l], N_MOD, axis=-1)]
        s5_u, rw_in, mla_in, ret_in = in_proj(x, sc1, sh1, norm_mix_g[l], pack_w_in(w_in[l]))

        prep = s5_prepare(s5_lambda_re[l], s5_lambda_im[l], s5_log_step[l], s5_b_re[l], s5_b_im[l],
                          s5_c_re[l], s5_c_im[l], S5_CHUNK // SUBLANES)
        y_s5 = s5_mixer(s5_u, prep, s5_d[l], s5_glu_w[l], s5_glu_b[l], chunk=S5_CHUNK)

        if l == 0:
            w2p, a2p = rw_pack_weights(rw_w2[l], rw_a2[l])
            vmix = None
        else:
            w2p, a2p, v1p, v2p = rw_pack_weights(rw_w2[l], rw_a2[l], rw_v1[l - 1], rw_v2[l - 1])
            vmix = (v_first, rw_v0[l - 1], v1p, v2p)
        r_rw, lw_rw, k_rw, kk_rw, b_rw, v_rw, bonus, g_rw = rw_prep(
            rw_in, rw_mu[l], rw_w0[l], w2p, rw_a0[l], a2p, rw_g2[l], rw_k_k[l], rw_k_a[l],
            rw_r_k[l].reshape(-1), vmix)
        if l == 0:
            v_first = v_rw
        y_rw = wkv_sequential(*wkv_precompute(r_rw, lw_rw, k_rw, kk_rw, b_rw, v_rw))

        q, k, v = mla_prep(mla_in, cos_t, sin_t, mla_q_norm_g[l], mla_kv_norm_g[l],
                           mla_pack_weights(mla_w_q_up[l], mla_w_kv_up[l]))
        y_mla = causal_attention(q, k, v)

        y_ret = retention_mixer(ret_in, cos_t, sin_t)

        x, hn, top_idx, top_w, cnt = out_proj(x, y_s5, y_rw, bonus, g_rw, y_mla, y_ret, rw_ln_g[l], rw_ln_b[l],
                                              w_out[l], g1, norm_ffn_g[l], sc2, sh2, router_w[l], router_b[l])
        counts = jnp.sum(cnt[:, :, 0, :N_EXPERTS], axis=(0, 1)).astype(jnp.int32)
        yg, tok_local = moe_ffn(hn, top_idx, top_w, counts, l, ex_w_gate, ex_b_gate, ex_w_up, ex_b_up,
                                ex_w_down, ex_b_down)
        x = moe_combine(x, yg, tok_local, g2, final_norm_g if l == depth - 1 else None)
    return x
```

```python
import functools
import math

import numpy as np
import jax
import jax.numpy as jnp
from jax import lax
from jax.experimental import pallas as pl
from jax.experimental.pallas import tpu as pltpu

F32 = jnp.float32
BF16 = jnp.bfloat16

D_MODEL = 1024
GROUP_WIDTH = 256
S5_CH = 16
S5_GROUPS = 16
S5_STATE = 64
S5_FLAT = S5_GROUPS * S5_STATE
RW_HEADS = 4
RW_HEAD = 64
RW_GN_EPS = 64e-5
MLA_HEADS = 4
MLA_NOPE = 64
MLA_ROPE = 32
MLA_V = 64
MLA_Q_RANK = 256
MLA_KV_RANK = 128
RET_HEADS = 4
RET_QK = 32
RET_V = 64
ROPE_BASE = 10000.0
N_EXPERTS = 32
TOP_K = 4
SWIGLU_ALPHA = 1.702
SWIGLU_LIMIT = 7.0
NORM_EPS = 1e-5
N_MOD = 6

LANES = 128
SUBLANES = 8
VMEM_LIMIT_BYTES = 56 * 1024 * 1024

IN_S5 = (0, 256)
IN_RW = (256, 1280)
IN_MLA = (1280, 1920)
IN_RET = (1920, 2944)
IN_PACKED = 2944


def _params(*sem):
    return pltpu.CompilerParams(dimension_semantics=sem, vmem_limit_bytes=VMEM_LIMIT_BYTES)


def _bdot(a, b):
    return jnp.dot(a.astype(BF16), b.astype(BF16), preferred_element_type=F32)


def _split_dot(a, b_bf16):
    hi = a.astype(BF16)
    lo = (a - hi.astype(F32)).astype(BF16)
    return (jnp.dot(hi, b_bf16, preferred_element_type=F32)
            + jnp.dot(lo, b_bf16, preferred_element_type=F32))


def _sigmoid(x):
    return 1.0 / (1.0 + jnp.exp(-x))


def _adaln_kernel(c_ref, w_ref, b_ref, o_ref):
    c = c_ref[...]
    cond = c * _sigmoid(c)
    o_ref[0] = _bdot(cond, w_ref[0]) + b_ref[0]


def adaln_mod(c, ada_w, ada_b):
    depth, d, n = ada_w.shape
    bsz = c.shape[0]
    tn = 1536
    return pl.pallas_call(
        _adaln_kernel,
        grid=(depth, n // tn),
        in_specs=[pl.BlockSpec((bsz, d), lambda l, j: (0, 0)),
                  pl.BlockSpec((1, d, tn), lambda l, j: (l, 0, j)),
                  pl.BlockSpec((1, 1, tn), lambda l, j: (l, 0, j))],
        out_specs=pl.BlockSpec((1, bsz, tn), lambda l, j: (l, 0, j)),
        out_shape=jax.ShapeDtypeStruct((depth, bsz, n), F32),
        compiler_params=_params("parallel", "parallel"),
        name="adaln_mod",
    )(c, ada_w, ada_b.reshape(depth, 1, n))


def _inproj_kernel(x_ref, sc_ref, sh_ref, g_ref, w_ref, s5_ref, rw_ref, mla_ref, ret_ref):
    x = x_ref[0]
    ms = jnp.mean(x * x, axis=-1, keepdims=True)
    hn = x * lax.rsqrt(ms + NORM_EPS) * g_ref[...]
    hn = hn * (1.0 + sc_ref[0]) + sh_ref[0]
    p = jnp.dot(hn.astype(BF16), w_ref[...], preferred_element_type=F32)
    s5_ref[0] = p[:, IN_S5[0]:IN_S5[1]]
    rw_ref[0] = p[:, IN_RW[0]:IN_RW[1]]
    mla_ref[0] = p[:, IN_MLA[0]:IN_MLA[1]]
    ret_ref[0] = p[:, IN_RET[0]:IN_RET[1]]


def in_proj(x, sc, sh, g, w_packed, tm=512):
    bsz, seq, d = x.shape
    widths = [b - a for a, b in (IN_S5, IN_RW, IN_MLA, IN_RET)]
    row = lambda b, i: (b, i, 0)
    per_b = lambda b, i: (b, 0, 0)
    return pl.pallas_call(
        _inproj_kernel,
        grid=(bsz, seq // tm),
        in_specs=[pl.BlockSpec((1, tm, d), row),
                  pl.BlockSpec((1, 1, d), per_b),
                  pl.BlockSpec((1, 1, d), per_b),
                  pl.BlockSpec((1, d), lambda b, i: (0, 0)),
                  pl.BlockSpec((d, IN_PACKED), lambda b, i: (0, 0))],
        out_specs=[pl.BlockSpec((1, tm, w), row) for w in widths],
        out_shape=[jax.ShapeDtypeStruct((bsz, seq, w), F32) for w in widths],
        compiler_params=_params("parallel", "parallel"),
        name="in_proj",
    )(x, sc, sh, g.reshape(1, d), w_packed)


def _swap_halves(cols, block):
    cols = np.asarray(cols).reshape(-1, 2, block // 2)
    return cols[:, ::-1, :].reshape(-1)


def pack_w_in(w_in_l):
    zero = w_in_l.shape[1]
    s5 = np.arange(0, 256)
    rw = np.arange(256, 1280)
    qc = np.arange(1280, 1536)
    kvc = np.arange(1536, 1664)
    kpe = np.arange(1664, 1696)
    z = lambda n: np.full((n,), zero)
    kpe_slot = np.concatenate([z(MLA_NOPE), kpe, z(LANES - MLA_NOPE - MLA_ROPE)])
    kpe_sw_slot = np.concatenate([z(MLA_NOPE), _swap_halves(kpe, MLA_ROPE), z(LANES - MLA_NOPE - MLA_ROPE)])
    rq = np.arange(1696, 1824)
    rk = np.arange(1824, 1952)
    rv = np.arange(1952, 2208)
    rg = np.arange(2208, 2464)
    idx = np.concatenate([s5, rw, qc, kvc, kpe_slot, kpe_sw_slot,
                          rq, rk, _swap_halves(rq, RET_QK), _swap_halves(rk, RET_QK), rv, rg])
    assert idx.shape[0] == IN_PACKED
    w_ext = jnp.concatenate([w_in_l, jnp.zeros((w_in_l.shape[0], 1), w_in_l.dtype)], axis=1)
    return jnp.take(w_ext, jnp.asarray(idx, jnp.int32), axis=1).astype(BF16)


def rope_tables(positions):
    inv = ROPE_BASE ** (-jnp.arange(0, MLA_ROPE, 2, dtype=F32) / MLA_ROPE)
    ang = positions.astype(F32)[..., None] * inv
    cos, sin = jnp.cos(ang), jnp.sin(ang)
    reps = LANES // MLA_ROPE
    cos_t = jnp.tile(jnp.concatenate([cos, cos], axis=-1), (1, 1, reps))
    sin_t = jnp.tile(jnp.concatenate([-sin, sin], axis=-1), (1, 1, reps))
    return cos_t, sin_t


def _s5_kernel(u_ref, bre_ref, bim_ref, cre_ref, cim_ref, lam_ref, lamq_ref, lamseg_ref,
               d_ref, gw_ref, gb_ref, o_ref, sre_ref, sim_ref, st_ref, *, nq):
    @pl.when(pl.program_id(1) == 0)
    def _():
        st_ref[...] = jnp.zeros_like(st_ref)

    u = u_ref[0]
    ub = u.astype(BF16)
    sre_ref[...] = jnp.dot(ub, bre_ref[...], preferred_element_type=F32)
    sim_ref[...] = jnp.dot(ub, bim_ref[...], preferred_element_type=F32)
    lam_re = lam_ref[0:1, :]
    lam_im = lam_ref[1:2, :]

    def scan_body(q, carry):
        cr, ci = carry
        rows = pl.ds(pl.multiple_of(q * SUBLANES, SUBLANES), SUBLANES)
        nr = lam_re * cr - lam_im * ci + sre_ref[rows, :]
        ni = lam_re * ci + lam_im * cr + sim_ref[rows, :]
        sre_ref[rows, :] = nr
        sim_ref[rows, :] = ni
        return nr, ni

    zero = jnp.zeros((SUBLANES, S5_FLAT), F32)
    end_re, end_im = lax.fori_loop(0, nq, scan_body, (zero, zero))

    seg_re = lamseg_ref[0:1, :]
    seg_im = lamseg_ref[1:2, :]
    cr, ci = st_ref[0:1, :], st_ref[1:2, :]
    in_re, in_im = [], []
    for r in range(SUBLANES):
        in_re.append(cr)
        in_im.append(ci)
        er, ei = end_re[r:r + 1, :], end_im[r:r + 1, :]
        cr, ci = seg_re * cr - seg_im * ci + er, seg_re * ci + seg_im * cr + ei
    st_ref[0:1, :] = cr
    st_ref[1:2, :] = ci
    car_re = jnp.concatenate(in_re, axis=0)
    car_im = jnp.concatenate(in_im, axis=0)

    def fix_body(q, _):
        rows = pl.ds(pl.multiple_of(q * SUBLANES, SUBLANES), SUBLANES)
        pr = lamq_ref[0, pl.ds(q, 1), :]
        pi = lamq_ref[1, pl.ds(q, 1), :]
        sre_ref[rows, :] = sre_ref[rows, :] + (pr * car_re - pi * car_im)
        sim_ref[rows, :] = sim_ref[rows, :] + (pr * car_im + pi * car_re)
        return 0

    lax.fori_loop(0, nq, fix_body, 0)

    y = (jnp.dot(sre_ref[...].astype(BF16), cre_ref[...], preferred_element_type=F32)
         - jnp.dot(sim_ref[...].astype(BF16), cim_ref[...], preferred_element_type=F32))
    y = y + d_ref[...] * u
    y = jax.nn.gelu(y)
    gate = jnp.dot(y.astype(BF16), gw_ref[...], preferred_element_type=F32) + gb_ref[...]
    o_ref[0] = y * _sigmoid(gate)


def s5_prepare(lam_re, lam_im, log_step, b_re, b_im, c_re, c_im, nq):
    dt = jnp.exp(log_step.astype(F32))[:, None]
    mag = jnp.exp(lam_re * dt)
    lb_re = mag * jnp.cos(lam_im * dt)
    lb_im = mag * jnp.sin(lam_im * dt)
    den = lam_re * lam_re + lam_im * lam_im
    n_re = lb_re - 1.0
    f_re = (n_re * lam_re + lb_im * lam_im) / den
    f_im = (lb_im * lam_re - n_re * lam_im) / den
    bb_re = f_re[..., None] * b_re - f_im[..., None] * b_im
    bb_im = f_re[..., None] * b_im + f_im[..., None] * b_re
    eye = jnp.eye(S5_GROUPS, dtype=F32)
    bd_in = lambda t: jnp.einsum('gph,gk->ghkp', t, eye).reshape(GROUP_WIDTH, S5_FLAT).astype(BF16)
    bd_out = lambda t: jnp.einsum('ghp,gk->gpkh', t, eye).reshape(S5_FLAT, GROUP_WIDTH).astype(BF16)
    lam = jnp.stack([lb_re.reshape(-1), lb_im.reshape(-1)])

    def power(n):
        n = n[:, None, None]
        m = jnp.exp(n * (lam_re * dt)[None])
        a = n * (lam_im * dt)[None]
        return jnp.stack([(m * jnp.cos(a)).reshape(-1, S5_FLAT), (m * jnp.sin(a)).reshape(-1, S5_FLAT)])

    lam_q = power(jnp.arange(1, nq + 1, dtype=F32))
    lam_seg = power(jnp.full((1,), float(nq), F32))[:, 0, :]
    return bd_in(bb_re), bd_in(bb_im), bd_out(c_re), bd_out(c_im), lam, lam_q, lam_seg


def s5_mixer(u, prep, d_skip, glu_w, glu_b, chunk=512):
    bsz, seq, w = u.shape
    nq = chunk // SUBLANES
    nchunk = seq // chunk
    bre, bim, cre, cim, lam, lam_q, lam_seg = prep
    up = u.reshape(bsz, nchunk, SUBLANES, nq, w).transpose(0, 1, 3, 2, 4).reshape(bsz, seq, w)
    const2 = lambda b, i: (0, 0)
    out = pl.pallas_call(
        functools.partial(_s5_kernel, nq=nq),
        grid=(bsz, nchunk),
        in_specs=[pl.BlockSpec((1, chunk, w), lambda b, i: (b, i, 0)),
                  pl.BlockSpec((w, S5_FLAT), const2),
                  pl.BlockSpec((w, S5_FLAT), const2),
                  pl.BlockSpec((S5_FLAT, w), const2),
                  pl.BlockSpec((S5_FLAT, w), const2),
                  pl.BlockSpec((2, S5_FLAT), const2),
                  pl.BlockSpec((2, nq, S5_FLAT), lambda b, i: (0, 0, 0)),
                  pl.BlockSpec((2, S5_FLAT), const2),
                  pl.BlockSpec((1, w), const2),
                  pl.BlockSpec((w, w), const2),
                  pl.BlockSpec((1, w), const2)],
        out_specs=pl.BlockSpec((1, chunk, w), lambda b, i: (b, i, 0)),
        out_shape=jax.ShapeDtypeStruct((bsz, seq, w), F32),
        scratch_shapes=[pltpu.VMEM((chunk, S5_FLAT), F32),
                        pltpu.VMEM((chunk, S5_FLAT), F32),
                        pltpu.VMEM((2, S5_FLAT), F32)],
        compiler_params=_params("parallel", "arbitrary"),
        name="s5_mixer",
    )(up, bre, bim, cre, cim, lam, lam_q, lam_seg,
      d_skip.reshape(1, w), glu_w.astype(BF16), glu_b.reshape(1, w))
    return out.reshape(bsz, nchunk, nq, SUBLANES, w).transpose(0, 1, 3, 2, 4).reshape(bsz, seq, w)


def _ret_kernel(h_ref, cos_ref, sin_ref, intra_ref, qw_ref, kw_ref, dec_ref, ones_ref,
                o_ref, st_ref, *, chunk):
    @pl.when(pl.program_id(1) == 0)
    def _():
        st_ref[...] = jnp.zeros_like(st_ref)

    h = h_ref[0]
    cos = cos_ref[0]
    sin = sin_ref[0]
    nqk = RET_HEADS * RET_QK
    q = h[:, 0:nqk] * cos + h[:, 2 * nqk:3 * nqk] * sin
    k = (h[:, nqk:2 * nqk] * cos + h[:, 3 * nqk:4 * nqk] * sin) * (RET_QK ** -0.5)
    v = h[:, 4 * nqk:4 * nqk + GROUP_WIDTH]
    g = h[:, 4 * nqk + GROUP_WIDTH:]
    kb = k.astype(BF16)
    lane_qk = lax.broadcasted_iota(jnp.int32, (chunk, nqk), 1) // RET_QK
    lane_v = lax.broadcasted_iota(jnp.int32, (chunk, GROUP_WIDTH), 1) // RET_V
    state = st_ref[...]
    o = _bdot(q * qw_ref[...], state)
    for hd in range(RET_HEADS):
        qh = jnp.where(lane_qk == hd, q, 0.0).astype(BF16)
        s = lax.dot_general(qh, kb, (((1,), (1,)), ((), ())), preferred_element_type=F32)
        s = s * intra_ref[hd]
        vh = jnp.where(lane_v == hd, v, 0.0).astype(BF16)
        o = o + jnp.dot(s.astype(BF16), vh, preferred_element_type=F32)
    kv = lax.dot_general((k * kw_ref[...]).astype(BF16), v.astype(BF16),
                         (((0,), (0,)), ((), ())), preferred_element_type=F32)
    dec = dec_ref[...]
    st_ref[...] = state * dec + jnp.where(dec > 0.0, kv, 0.0)
    ms = _split_dot(o * o, ones_ref[...])
    o = o * lax.rsqrt(ms + NORM_EPS)
    o_ref[0] = o * (g * _sigmoid(g))


def retention_tables(chunk):
    log_gamma = np.log1p(-np.exp2(-5.0 - np.arange(RET_HEADS, dtype=np.float64)))
    idx = np.arange(chunk, dtype=np.float64)
    diff = idx[:, None] - idx[None, :]
    intra = np.where(diff >= 0, np.exp(np.maximum(diff, 0.0)[None] * log_gamma[:, None, None]), 0.0)
    q_w = np.repeat(np.exp((idx + 1.0)[:, None] * log_gamma[None, :]), RET_QK, axis=1)
    k_w = np.repeat(np.exp((chunk - 1.0 - idx)[:, None] * log_gamma[None, :]), RET_QK, axis=1)
    head_q = np.arange(RET_HEADS * RET_QK) // RET_QK
    head_v = np.arange(GROUP_WIDTH) // RET_V
    same = head_q[:, None] == head_v[None, :]
    dec = np.where(same, np.exp(chunk * log_gamma)[head_q][:, None], 0.0)
    ones = (head_v[:, None] == head_v[None, :]).astype(np.float64) / RET_V
    f = lambda a: jnp.asarray(a, F32)
    return f(intra), f(q_w), f(k_w), f(dec), jnp.asarray(ones, BF16)


def retention_mixer(h, cos_t, sin_t, chunk=256):
    bsz, seq, wh = h.shape
    intra, q_w, k_w, dec, ones = retention_tables(chunk)
    nqk = RET_HEADS * RET_QK
    row = lambda b, i: (b, i, 0)
    c2 = lambda b, i: (0, 0)
    return pl.pallas_call(
        functools.partial(_ret_kernel, chunk=chunk),
        grid=(bsz, seq // chunk),
        in_specs=[pl.BlockSpec((1, chunk, wh), row),
                  pl.BlockSpec((1, chunk, LANES), row),
                  pl.BlockSpec((1, chunk, LANES), row),
                  pl.BlockSpec((RET_HEADS, chunk, chunk), lambda b, i: (0, 0, 0)),
                  pl.BlockSpec((chunk, nqk), c2),
                  pl.BlockSpec((chunk, nqk), c2),
                  pl.BlockSpec((nqk, GROUP_WIDTH), c2),
                  pl.BlockSpec((GROUP_WIDTH, GROUP_WIDTH), c2)],
        out_specs=pl.BlockSpec((1, chunk, GROUP_WIDTH), row),
        out_shape=jax.ShapeDtypeStruct((bsz, seq, GROUP_WIDTH), F32),
        scratch_shapes=[pltpu.VMEM((nqk, GROUP_WIDTH), F32)],
        compiler_params=_params("parallel", "arbitrary"),
        name="retention",
    )(h, cos_t, sin_t, intra, q_w, k_w, dec, ones)


def _mla_prep_kernel(h_ref, cos_ref, sin_ref, qg_ref, kvg_ref, wqa_ref, wqb_ref, wk_ref, wv_ref,
                     q_ref, k_ref, v_ref, *, scale):
    h = h_ref[0]
    tm = h.shape[0]
    lane = lax.broadcasted_iota(jnp.int32, (tm, LANES), 1)
    is_nope = lane < MLA_NOPE
    is_rope = jnp.logical_and(lane >= MLA_NOPE, lane < MLA_NOPE + MLA_ROPE)
    cm = jnp.where(is_nope, 1.0, jnp.where(is_rope, cos_ref[0], 0.0))
    sm = jnp.where(is_rope, sin_ref[0], 0.0)

    qc = h[:, 0:MLA_Q_RANK]
    qn = (qc * lax.rsqrt(jnp.mean(qc * qc, axis=-1, keepdims=True) + NORM_EPS) * qg_ref[...]).astype(BF16)
    kvc = h[:, MLA_Q_RANK:MLA_Q_RANK + MLA_KV_RANK]
    kvn = (kvc * lax.rsqrt(jnp.mean(kvc * kvc, axis=-1, keepdims=True) + NORM_EPS) * kvg_ref[...]).astype(BF16)
    off = MLA_Q_RANK + MLA_KV_RANK
    kpe = h[:, off:off + LANES] * cm + h[:, off + LANES:off + 2 * LANES] * sm

    qa = jnp.dot(qn, wqa_ref[...], preferred_element_type=F32)
    qb = jnp.dot(qn, wqb_ref[...], preferred_element_type=F32)
    kn = jnp.dot(kvn, wk_ref[...], preferred_element_type=F32)
    v_ref[0] = jnp.dot(kvn, wv_ref[...], preferred_element_type=F32).astype(BF16)
    for hd in range(MLA_HEADS):
        sl = slice(hd * LANES, (hd + 1) * LANES)
        q_ref[0, hd] = ((qa[:, sl] * cm + qb[:, sl] * sm) * scale).astype(BF16)
        k_ref[0, hd] = (kn[:, sl] + kpe).astype(BF16)


def mla_pack_weights(w_q_up, w_kv_up):
    dq = MLA_NOPE + MLA_ROPE
    zq = w_q_up.shape[1]
    zk = w_kv_up.shape[1]
    z = lambda n, zero: np.full((n,), zero)
    ia, ib, ik, iv = [], [], [], []
    for hd in range(MLA_HEADS):
        nope = np.arange(hd * dq, hd * dq + MLA_NOPE)
        pe = np.arange(hd * dq + MLA_NOPE, (hd + 1) * dq)
        pad = LANES - dq
        ia += [nope, pe, z(pad, zq)]
        ib += [z(MLA_NOPE, zq), _swap_halves(pe, MLA_ROPE), z(pad, zq)]
        kv0 = hd * (MLA_NOPE + MLA_V)
        ik += [np.arange(kv0, kv0 + MLA_NOPE), z(LANES - MLA_NOPE, zk)]
        iv += [np.arange(kv0 + MLA_NOPE, kv0 + MLA_NOPE + MLA_V)]
    ext = lambda w: jnp.concatenate([w, jnp.zeros((w.shape[0], 1), w.dtype)], axis=1)
    take = lambda w, idx: jnp.take(ext(w), jnp.asarray(np.concatenate(idx), jnp.int32), axis=1).astype(BF16)
    return take(w_q_up, ia), take(w_q_up, ib), take(w_kv_up, ik), take(w_kv_up, iv)


def mla_prep(h, cos_t, sin_t, q_norm_g, kv_norm_g, packed, tm=512):
    bsz, seq, wh = h.shape
    wqa, wqb, wk, wv = packed
    row = lambda b, i: (b, i, 0)
    c2 = lambda b, i: (0, 0)
    hrow = lambda b, i: (b, 0, i, 0)
    scale = (MLA_NOPE + MLA_ROPE) ** -0.5
    return pl.pallas_call(
        functools.partial(_mla_prep_kernel, scale=scale),
        grid=(bsz, seq // tm),
        in_specs=[pl.BlockSpec((1, tm, wh), row),
                  pl.BlockSpec((1, tm, LANES), row),
                  pl.BlockSpec((1, tm, LANES), row),
                  pl.BlockSpec((1, MLA_Q_RANK), c2),
                  pl.BlockSpec((1, MLA_KV_RANK), c2),
                  pl.BlockSpec(wqa.shape, c2),
                  pl.BlockSpec(wqb.shape, c2),
                  pl.BlockSpec(wk.shape, c2),
                  pl.BlockSpec(wv.shape, c2)],
        out_specs=[pl.BlockSpec((1, MLA_HEADS, tm, LANES), hrow),
                   pl.BlockSpec((1, MLA_HEADS, tm, LANES), hrow),
                   pl.BlockSpec((1, tm, GROUP_WIDTH), row)],
        out_shape=[jax.ShapeDtypeStruct((bsz, MLA_HEADS, seq, LANES), BF16),
                   jax.ShapeDtypeStruct((bsz, MLA_HEADS, seq, LANES), BF16),
                   jax.ShapeDtypeStruct((bsz, seq, GROUP_WIDTH), BF16)],
        compiler_params=_params("parallel", "parallel"),
        name="mla_prep",
    )(h, cos_t, sin_t, q_norm_g.reshape(1, -1), kv_norm_g.reshape(1, -1), wqa, wqb, wk, wv)


def _attn_kernel(q_ref, k_ref, v_ref, o_ref, m_ref, l_ref, acc_ref, *, blk):
    qi = pl.program_id(2)
    rep = blk // LANES

    heads = range(2)

    def step(j, masked):
        rows = pl.ds(pl.multiple_of(j * blk, blk), blk)
        vs = v_ref[0, rows, :]
        s = [lax.dot_general(q_ref[0, hh], k_ref[0, hh, rows, :], (((1,), (1,)), ((), ())),
                             preferred_element_type=F32) for hh in heads]
        if masked:
            r = lax.broadcasted_iota(jnp.int32, (blk, blk), 0)
            c = lax.broadcasted_iota(jnp.int32, (blk, blk), 1)
            s = [jnp.where(c <= r, t, -jnp.inf) for t in s]
        m_prev = [m_ref[hh] for hh in heads]
        m_new = [jnp.maximum(m_prev[hh], jnp.max(s[hh], axis=-1, keepdims=True)) for hh in heads]
        p = [jnp.exp(s[hh] - jnp.concatenate([m_new[hh]] * rep, axis=1)) for hh in heads]
        alpha = [jnp.exp(m_prev[hh] - m_new[hh]) for hh in heads]
        for hh in heads:
            l_ref[hh] = alpha[hh] * l_ref[hh] + jnp.sum(p[hh], axis=-1, keepdims=True)
            acc_ref[hh] = alpha[hh] * acc_ref[hh] + jnp.dot(p[hh].astype(BF16), vs, preferred_element_type=F32)
            m_ref[hh] = m_new[hh]

    for hh in heads:
        m_ref[hh] = jnp.full((blk, LANES), -jnp.inf, F32)
        l_ref[hh] = jnp.zeros((blk, LANES), F32)
        acc_ref[hh] = jnp.zeros((blk, LANES), F32)

    def body(j, _):
        step(j, False)
        return 0

    lax.fori_loop(0, qi, body, 0)
    step(qi, True)

    lane = lax.broadcasted_iota(jnp.int32, (blk, LANES), 1)
    o0 = acc_ref[0] / l_ref[0]
    o1 = acc_ref[1] / l_ref[1]
    o_ref[0] = jnp.where(lane < MLA_V, o0, o1)


def causal_attention(q, k, v, blk=1024):
    bsz, nh, seq, dk = q.shape
    return pl.pallas_call(
        functools.partial(_attn_kernel, blk=blk),
        grid=(bsz, nh // 2, seq // blk),
        in_specs=[pl.BlockSpec((1, 2, blk, dk), lambda b, p, i: (b, p, i, 0)),
                  pl.BlockSpec((1, 2, seq, dk), lambda b, p, i: (b, p, 0, 0)),
                  pl.BlockSpec((1, seq, LANES), lambda b, p, i: (b, 0, p))],
        out_specs=pl.BlockSpec((1, blk, LANES), lambda b, p, i: (b, i, p)),
        out_shape=jax.ShapeDtypeStruct((bsz, seq, nh // 2 * LANES), F32),
        scratch_shapes=[pltpu.VMEM((2, blk, LANES), F32)] * 3,
        compiler_params=_params("parallel", "parallel", "arbitrary"),
        name="mla_attention",
    )(q, k, v)


def _rw_prep_kernel(*refs, first):
    if first:
        (h_ref, prev_ref, mu_ref, w0_ref, w2_ref, a0_ref, a2_ref, g2_ref, kk_ref, ka_ref, rk_ref, ones_ref,
         r_out, lw_out, k_out, kk_out, b_out, v_ref, bonus_ref, g_ref) = refs
    else:
        (h_ref, prev_ref, mu_ref, w0_ref, w2_ref, a0_ref, a2_ref, g2_ref, kk_ref, ka_ref, rk_ref, ones_ref,
         vf_ref, v0_ref, v1_ref, v2_ref,
         r_out, lw_out, k_out, kk_out, b_out, v_ref, bonus_ref, g_ref) = refs
    h = h_ref[0]
    tm = h.shape[0]
    last = prev_ref[0, SUBLANES - 1:SUBLANES, :]
    last = jnp.where(pl.program_id(1) == 0, 0.0, last)
    row = lax.broadcasted_iota(jnp.int32, h.shape, 0)
    h_prev = jnp.where(row == 0, last, pltpu.roll(h, 1, axis=0))
    h = h + (h_prev - h) * mu_ref[...]
    w = GROUP_WIDTH
    r = h[:, 0:w]
    k = h[:, w:2 * w]
    v = h[:, 2 * w:3 * w]
    wa = h[:, 3 * w:3 * w + LANES]
    gd = h[:, 3 * w + LANES:]
    z = w0_ref[...] + _bdot(jnp.tanh(wa), w2_ref[...])
    nz = -z
    softplus = jnp.maximum(nz, 0.0) + jnp.log(1.0 + jnp.exp(-jnp.abs(nz)))
    log_decay = -jnp.exp(-softplus - 0.5)
    a = _sigmoid(a0_ref[...] + _bdot(wa, a2_ref[...]))
    g_ref[0] = _bdot(_sigmoid(gd), g2_ref[...])
    if not first:
        mix = _sigmoid(v0_ref[...] + _bdot(_bdot(v, v1_ref[...]), v2_ref[...]))
        v = v + (vf_ref[0] - v) * mix
    kk = k * kk_ref[...]
    norm = jnp.sqrt(_split_dot(kk * kk, ones_ref[...]))
    kk = kk / jnp.maximum(norm, 1e-12)
    k = k * (1.0 + (a - 1.0) * ka_ref[...])
    v_ref[0] = v
    bonus_ref[0] = _split_dot(r * k * rk_ref[...], ones_ref[...]) * v
    r_out[0] = r
    lw_out[0] = log_decay
    k_out[0] = k
    kk_out[0] = kk
    b_out[0] = kk * a


def rw_pack_weights(w2, a2, v1=None, v2=None):
    zeros = lambda n, m: jnp.zeros((n, m), F32)
    half = LANES // 2
    w2p = jnp.concatenate([w2, zeros(half, GROUP_WIDTH)], axis=0).astype(BF16)
    a2p = jnp.concatenate([zeros(half, GROUP_WIDTH), a2], axis=0).astype(BF16)
    if v1 is None:
        return w2p, a2p
    v1p = jnp.concatenate([v1, zeros(GROUP_WIDTH, LANES - v1.shape[1])], axis=1).astype(BF16)
    v2p = jnp.concatenate([v2, zeros(LANES - v2.shape[0], GROUP_WIDTH)], axis=0).astype(BF16)
    return w2p, a2p, v1p, v2p


def head_ones(width, head, scale):
    hd = np.arange(width) // head
    return jnp.asarray((hd[:, None] == hd[None, :]).astype(np.float32) * scale, BF16)


def rw_prep(h, mu, w0, w2p, a0, a2p, g2, k_k, k_a, r_k, vmix=None, tm=512):
    bsz, seq, wh = h.shape
    w = GROUP_WIDTH
    first = vmix is None
    row = lambda b, i: (b, i, 0)
    c2 = lambda b, i: (0, 0)
    vec = lambda t: t.reshape(1, -1)
    prev_map = lambda b, i: (b, jnp.maximum(i * (tm // SUBLANES) - 1, 0), 0)
    args = [h, h, vec(mu), vec(w0), w2p, vec(a0), a2p, g2.astype(BF16), vec(k_k), vec(k_a), vec(r_k),
            head_ones(w, RW_HEAD, 1.0)]
    specs = [pl.BlockSpec((1, tm, wh), row), pl.BlockSpec((1, SUBLANES, wh), prev_map),
             pl.BlockSpec((1, wh), c2), pl.BlockSpec((1, w), c2), pl.BlockSpec((LANES, w), c2),
             pl.BlockSpec((1, w), c2), pl.BlockSpec((LANES, w), c2), pl.BlockSpec((LANES, w), c2),
             pl.BlockSpec((1, w), c2), pl.BlockSpec((1, w), c2), pl.BlockSpec((1, w), c2),
             pl.BlockSpec((w, w), c2)]
    if not first:
        v_first, v0, v1p, v2p = vmix
        args += [v_first, vec(v0), v1p, v2p]
        specs += [pl.BlockSpec((1, tm, w), row), pl.BlockSpec((1, w), c2),
                  pl.BlockSpec((w, LANES), c2), pl.BlockSpec((LANES, w), c2)]
    return pl.pallas_call(
        functools.partial(_rw_prep_kernel, first=first),
        grid=(bsz, seq // tm),
        in_specs=specs,
        out_specs=[pl.BlockSpec((1, tm, w), row)] * 8,
        out_shape=[jax.ShapeDtypeStruct((bsz, seq, w), F32)] * 8,
        compiler_params=_params("parallel", "parallel"),
        name="rwkv_prep",
    )(*args)


WKV_CHUNK = 64
_NN = (((1,), (0,)), ((), ()))
_NT = (((1,), (1,)), ((), ()))
_TN = (((0,), (0,)), ((), ()))


def _split(a):
    hi = a.astype(BF16)
    return hi, (a - hi.astype(F32)).astype(BF16)


def _mm(a, b, dims=_NN, passes=1):
    if passes == 1:
        return lax.dot_general(a.astype(BF16), b.astype(BF16), dims, preferred_element_type=F32)
    ah, al = _split(a)
    bh, bl = _split(b)
    d = lambda x, y: lax.dot_general(x, y, dims, preferred_element_type=F32)
    return d(ah, bh) + (d(ah, bl) + d(al, bh))


def _wkv_pre_kernel(r_ref, lw_ref, k_ref, kk_ref, b_ref, v_ref, tri_ref,
                    q_ref, y0_ref, phi_ref, dlt_ref, *, nchunk):
    c = WKV_CHUNK
    n2 = 2 * c
    lane = lax.broadcasted_iota(jnp.int32, (c, LANES), 1)
    head0 = lane < RW_HEAD
    row = lax.broadcasted_iota(jnp.int32, (n2, n2), 0)
    col = lax.broadcasted_iota(jnp.int32, (n2, n2), 1)
    strict = col < row
    incl = col <= row
    eye = (row == col).astype(F32)
    stack = lambda x: jnp.concatenate([jnp.where(head0, x, 0.0), jnp.where(head0, 0.0, x)], axis=0)
    fold = lambda x: x[:c] + x[c:]
    tri = tri_ref[...]
    chains = [(ci, p) for ci in range(nchunk) for p in range(GROUP_WIDTH // LANES)]
    st = []
    for ci, p in chains:
        sl = slice(p * LANES, (p + 1) * LANES)
        rows = slice(ci * c, (ci + 1) * c)
        lw = lw_ref[0, rows, sl]
        lw_hi, lw_lo = _split(lw)
        cum = (jnp.dot(tri, lw_hi, preferred_element_type=F32)
               + jnp.dot(tri, lw_lo, preferred_element_type=F32))
        last = cum[c - 1:c, :]
        e_in = jnp.exp(cum)
        e_ex = jnp.exp(cum - lw)
        e_neg = jnp.exp(-cum)
        e_end = jnp.exp(last - cum)
        kk, r, k, b, v = kk_ref[0, rows, sl], r_ref[0, rows, sl], k_ref[0, rows, sl], b_ref[0, rows, sl], v_ref[0, rows, sl]
        kt2 = stack(kk * e_ex)
        rt2 = stack(r * e_in)
        lhs = jnp.concatenate([kt2, rt2], axis=0)
        gb = _mm(lhs, stack(b * e_neg), _NT)
        gk = _mm(lhs, stack(k * e_neg), _NT)
        x = -jnp.where(strict, gb[:n2], 0.0)
        st.append(dict(sl=sl, rows=rows, kt2=kt2, rt2=rt2, kp2=stack(k * e_end), bp2=stack(b * e_end), v2=stack(v),
                       g2=jnp.where(strict, gk[:n2], 0.0), rb2=jnp.where(incl, gb[n2:], 0.0),
                       rk2=jnp.where(incl, gk[n2:], 0.0), last=last, x=x, t=eye + x))
    for _ in range(int(np.log2(c)) - 1):
        for d in st:
            d["x"] = _mm(d["x"], d["x"])
        for d in st:
            d["t"] = d["t"] + _mm(d["t"], d["x"])
    for d in st:
        d["gv2"] = _mm(d["g2"], d["v2"])
    for d in st:
        d["tku"] = _mm(d["t"], jnp.concatenate([d["kt2"], d["gv2"]], axis=1))
    for d in st:
        d["rbz"] = _mm(d["rb2"], d["tku"])
    for (ci, p), d in zip(chains, st):
        sl, rows, tku, rbz = d["sl"], d["rows"], d["tku"], d["rbz"]
        q_ref[0, rows, sl] = fold(d["rt2"] - rbz[:, :LANES])
        y0_ref[0, rows, sl] = fold(_mm(d["rk2"], d["v2"]) - rbz[:, LANES:])
        decay = jnp.where(row == col, jnp.broadcast_to(jnp.exp(d["last"]), (n2, LANES)), 0.0)
        phi_ref[0, ci, p] = decay - _mm(d["bp2"], tku[:, :LANES], _TN)
        dlt_ref[0, ci, p] = _mm(d["kp2"], d["v2"], _TN) - _mm(d["bp2"], tku[:, LANES:], _TN)


def wkv_precompute(r, lw, k, kk, b, v, block=512):
    bsz, seq, w = r.shape
    c = WKV_CHUNK
    nchunk = block // c
    npair = w // LANES
    tri = jnp.asarray(np.tril(np.ones((c, c), np.float32)), BF16)
    row = lambda bb, i: (bb, i, 0)
    blk = pl.BlockSpec((1, block, w), row)
    mat = pl.BlockSpec((1, nchunk, npair, LANES, LANES), lambda bb, i: (bb, i, 0, 0, 0))
    mshape = jax.ShapeDtypeStruct((bsz, seq // c, npair, LANES, LANES), F32)
    return pl.pallas_call(
        functools.partial(_wkv_pre_kernel, nchunk=nchunk),
        grid=(bsz, seq // block),
        in_specs=[blk] * 6 + [pl.BlockSpec((c, c), lambda bb, i: (0, 0))],
        out_specs=[blk, blk, mat, mat],
        out_shape=[jax.ShapeDtypeStruct((bsz, seq, w), F32)] * 2 + [mshape, mshape],
        compiler_params=_params("parallel", "parallel"),
        name="wkv7_chunk_prep",
    )(r, lw, k, kk, b, v, tri)


def _wkv_seq_kernel(q_ref, y0_ref, phi_ref, dlt_ref, y_ref, st_ref, *, nchunk):
    @pl.when(pl.program_id(1) == 0)
    def _():
        st_ref[...] = jnp.zeros_like(st_ref)

    c = WKV_CHUNK
    npair = GROUP_WIDTH // LANES
    m = [st_ref[p] for p in range(npair)]
    for i in range(nchunk):
        rows = slice(i * c, (i + 1) * c)
        for p in range(npair):
            sl = slice(p * LANES, (p + 1) * LANES)
            y_ref[0, rows, sl] = _mm(q_ref[0, rows, sl], m[p], _NN, 3) + y0_ref[0, rows, sl]
        m = [_mm(phi_ref[0, i, p], m[p], _NN, 3) + dlt_ref[0, i, p] for p in range(npair)]
    for p in range(npair):
        st_ref[p] = m[p]


def wkv_sequential(q, y0, phi, dlt, block=512):
    bsz, seq, w = q.shape
    c = WKV_CHUNK
    nchunk = block // c
    npair = w // LANES
    row = lambda bb, i: (bb, i, 0)
    blk = pl.BlockSpec((1, block, w), row)
    mat = pl.BlockSpec((1, nchunk, npair, LANES, LANES), lambda bb, i: (bb, i, 0, 0, 0))
    return pl.pallas_call(
        functools.partial(_wkv_seq_kernel, nchunk=nchunk),
        grid=(bsz, seq // block),
        in_specs=[blk, blk, mat, mat],
        out_specs=blk,
        out_shape=jax.ShapeDtypeStruct((bsz, seq, w), F32),
        scratch_shapes=[pltpu.VMEM((npair, LANES, LANES), F32)],
        compiler_params=_params("parallel", "arbitrary"),
        name="wkv7_chunk_scan",
    )(q, y0, phi, dlt)


def _outproj_kernel(x_ref, ys5_ref, yrw_ref, bonus_ref, grw_ref, ymla_ref, yret_ref, lng_ref, lnb_ref, ones_ref,
                    wout_ref, g1_ref, ng_ref, sc_ref, sh_ref, rhi_ref, rlo_ref, rb_ref,
                    xo_ref, hn_ref, idx_ref, tw_ref, cnt_ref):
    y = yrw_ref[0]
    mean = _split_dot(y, ones_ref[...])
    yc = y - mean
    var = _split_dot(yc * yc, ones_ref[...])
    yrw = (yc * lax.rsqrt(var + RW_GN_EPS) * lng_ref[...] + lnb_ref[...] + bonus_ref[0]) * grw_ref[0]
    w = GROUP_WIDTH
    mixed = (_bdot(ys5_ref[0], wout_ref[0:w, :]) + _bdot(yrw, wout_ref[w:2 * w, :])
             + _bdot(ymla_ref[0], wout_ref[2 * w:3 * w, :]) + _bdot(yret_ref[0], wout_ref[3 * w:4 * w, :]))
    x = x_ref[0] + g1_ref[0] * mixed
    xo_ref[0] = x
    ms = jnp.mean(x * x, axis=-1, keepdims=True)
    hn = x * lax.rsqrt(ms + NORM_EPS) * ng_ref[...]
    hn = hn * (1.0 + sc_ref[0]) + sh_ref[0]
    hi = hn.astype(BF16)
    hn_ref[0] = hi
    lo = (hn - hi.astype(F32)).astype(BF16)
    logits = (jnp.dot(hi, rhi_ref[...], preferred_element_type=F32)
              + jnp.dot(lo, rhi_ref[...], preferred_element_type=F32)
              + jnp.dot(hi, rlo_ref[...], preferred_element_type=F32) + rb_ref[...])
    lane = lax.broadcasted_iota(jnp.int32, logits.shape, 1)
    lane_f = lane.astype(F32)
    cur = jnp.where(lane < N_EXPERTS, logits, -jnp.inf)
    idx_out = jnp.zeros(logits.shape, F32)
    val_out = jnp.zeros(logits.shape, F32)
    picked = jnp.zeros(logits.shape, F32)
    top = None
    denom = None
    for j in range(TOP_K):
        m = jnp.max(cur, axis=-1, keepdims=True)
        sel = jnp.min(jnp.where(cur == m, lane_f, float(LANES)), axis=-1, keepdims=True)
        hit = lane_f == sel
        cur = jnp.where(hit, -jnp.inf, cur)
        picked = picked + jnp.where(hit, 1.0, 0.0)
        top = m if top is None else top
        e = jnp.exp(m - top)
        denom = e if denom is None else denom + e
        idx_out = jnp.where(lane == j, sel, idx_out)
        val_out = jnp.where(lane == j, e, val_out)
    idx_ref[0] = idx_out.astype(jnp.int32)
    tw_ref[0] = val_out / denom
    cnt_ref[0, 0] = jnp.broadcast_to(jnp.sum(picked, axis=0, keepdims=True), (SUBLANES, LANES))


def out_proj(x, y_s5, y_rw, bonus, g_rw, y_mla, y_ret, ln_g, ln_b, w_out, g1, norm_g, sc2, sh2,
             router_w, router_b, tm=512):
    bsz, seq, d = x.shape
    w = GROUP_WIDTH
    row = lambda b, i: (b, i, 0)
    per_b = lambda b, i: (b, 0, 0)
    c2 = lambda b, i: (0, 0)
    vec = lambda t: t.reshape(1, -1)
    pad = LANES - router_w.shape[1]
    rw_pad = jnp.concatenate([router_w, jnp.zeros((d, pad), F32)], axis=1)
    r_hi = rw_pad.astype(BF16)
    r_lo = (rw_pad - r_hi.astype(F32)).astype(BF16)
    rb = jnp.concatenate([router_b, jnp.zeros((pad,), F32)]).reshape(1, LANES)
    mixer = pl.BlockSpec((1, tm, w), row)
    return pl.pallas_call(
        _outproj_kernel,
        grid=(bsz, seq // tm),
        in_specs=[pl.BlockSpec((1, tm, d), row), mixer, mixer, mixer, mixer, mixer, mixer,
                  pl.BlockSpec((1, w), c2), pl.BlockSpec((1, w), c2), pl.BlockSpec((w, w), c2),
                  pl.BlockSpec((4 * w, d), c2),
                  pl.BlockSpec((1, 1, d), per_b), pl.BlockSpec((1, d), c2),
                  pl.BlockSpec((1, 1, d), per_b), pl.BlockSpec((1, 1, d), per_b),
                  pl.BlockSpec((d, LANES), c2), pl.BlockSpec((d, LANES), c2), pl.BlockSpec((1, LANES), c2)],
        out_specs=[pl.BlockSpec((1, tm, d), row), pl.BlockSpec((1, tm, d), row), pl.BlockSpec((1, tm, LANES), row),
                   pl.BlockSpec((1, tm, LANES), row), pl.BlockSpec((1, 1, SUBLANES, LANES), lambda b, i: (b, i, 0, 0))],
        out_shape=[jax.ShapeDtypeStruct((bsz, seq, d), F32), jax.ShapeDtypeStruct((bsz, seq, d), BF16),
                   jax.ShapeDtypeStruct((bsz, seq, LANES), jnp.int32), jax.ShapeDtypeStruct((bsz, seq, LANES), F32),
                   jax.ShapeDtypeStruct((bsz, seq // tm, SUBLANES, LANES), F32)],
        compiler_params=_params("parallel", "parallel"),
        name="out_proj",
    )(x, y_s5, y_rw, bonus, g_rw, y_mla, y_ret, vec(ln_g), vec(ln_b), head_ones(w, RW_HEAD, 1.0 / RW_HEAD),
      w_out.astype(BF16), g1, vec(norm_g), sc2, sh2, r_hi, r_lo, rb)


MOE_ROWS = 512


def _moe_kernel(blk_e_ref, blk_on_ref, x_ref, wg_ref, bg_ref, wu_ref, bu_ref, wd_ref, bd_ref, o_ref, wb_ref):
    i = pl.program_id(0)
    changed = jnp.logical_or(i == 0, blk_e_ref[i] != blk_e_ref[jnp.maximum(i - 1, 0)])

    @pl.when(changed)
    def _():
        wb_ref[0] = wg_ref[0, 0].astype(BF16)
        wb_ref[1] = wu_ref[0, 0].astype(BF16)
        wb_ref[2] = wd_ref[0, 0].astype(BF16)

    @pl.when(blk_on_ref[i] > 0)
    def _():
        x = x_ref[...]
        gt = jnp.minimum(jnp.dot(x, wb_ref[0], preferred_element_type=F32) + bg_ref[0, 0], SWIGLU_LIMIT)
        up = jnp.clip(jnp.dot(x, wb_ref[1], preferred_element_type=F32) + bu_ref[0, 0], -SWIGLU_LIMIT, SWIGLU_LIMIT)
        act = gt * _sigmoid(SWIGLU_ALPHA * gt) * (up + 1.0)
        o_ref[...] = (jnp.dot(act.astype(BF16), wb_ref[2], preferred_element_type=F32) + bd_ref[0, 0]).astype(BF16)


def moe_experts(xb, blk_e, blk_on, layer, w_gate, b_gate, w_up, b_up, w_down, b_down):
    p_rows, d = xb.shape
    depth, n_e, _, de = w_gate.shape
    wmap = lambda i, e, on: (layer, e[i], 0, 0)
    rows = lambda i, e, on: (i, 0)
    return pl.pallas_call(
        _moe_kernel,
        grid_spec=pltpu.PrefetchScalarGridSpec(
            num_scalar_prefetch=2,
            grid=(p_rows // MOE_ROWS,),
            in_specs=[pl.BlockSpec((MOE_ROWS, d), rows),
                      pl.BlockSpec((1, 1, d, de), wmap), pl.BlockSpec((1, 1, 1, de), wmap),
                      pl.BlockSpec((1, 1, d, de), wmap), pl.BlockSpec((1, 1, 1, de), wmap),
                      pl.BlockSpec((1, 1, de, d), wmap), pl.BlockSpec((1, 1, 1, d), wmap)],
            out_specs=pl.BlockSpec((MOE_ROWS, d), rows),
            scratch_shapes=[pltpu.VMEM((3, d, de), BF16)]),
        out_shape=jax.ShapeDtypeStruct((p_rows, d), BF16),
        compiler_params=_params("arbitrary"),
        name="moe_experts",
    )(blk_e, blk_on, xb, w_gate, b_gate.reshape(depth, n_e, 1, de), w_up, b_up.reshape(depth, n_e, 1, de),
      w_down, b_down.reshape(depth, n_e, 1, d))


def moe_route(top_idx, counts):
    t = top_idx.shape[0]
    n_assign = t * TOP_K
    flat_e = top_idx.reshape(-1)
    iota = jnp.arange(n_assign, dtype=jnp.int32)
    _, order = lax.sort((flat_e, iota), num_keys=1)
    _, rank = lax.sort((order, iota), num_keys=1)
    start = jnp.cumsum(counts) - counts
    padded = (counts + MOE_ROWS - 1) // MOE_ROWS * MOE_ROWS
    pad_end = jnp.cumsum(padded)
    pad_start = pad_end - padded
    pos = rank + (pad_start - start)[flat_e]
    p_rows = n_assign + N_EXPERTS * MOE_ROWS
    n_blocks = p_rows // MOE_ROWS
    blk_first = jnp.arange(n_blocks, dtype=jnp.int32) * MOE_ROWS
    blk_e = jnp.minimum(jnp.sum(pad_end[None, :] <= blk_first[:, None], axis=1, dtype=jnp.int32), N_EXPERTS - 1)
    blk_within = blk_first - pad_start[blk_e]
    blk_left = counts[blk_e] - blk_within
    blk_on = (blk_left > 0).astype(jnp.int32)
    r = jnp.arange(MOE_ROWS, dtype=jnp.int32)[None, :]
    src = jnp.clip((start[blk_e] + blk_within)[:, None] + r, 0, n_assign - 1)
    row = blk_first[:, None] + r
    buf_tok = jnp.where(r < blk_left[:, None], order[src.reshape(-1)].reshape(n_blocks, MOE_ROWS) // TOP_K, row % t)
    return buf_tok.reshape(-1), pos.reshape(t, TOP_K), blk_e, blk_on


def moe_ffn(hn_bf16, top_idx, counts, layer, w_gate, b_gate, w_up, b_up, w_down, b_down):
    bsz, seq, d = hn_bf16.shape
    t = bsz * seq
    buf_tok, pos, blk_e, blk_on = moe_route(top_idx.reshape(t, LANES)[:, :TOP_K], counts)
    xb = hn_bf16.reshape(t, d).at[buf_tok].get(mode='promise_in_bounds')
    yb = moe_experts(xb, blk_e, blk_on, layer, w_gate, b_gate, w_up, b_up, w_down, b_down)
    yg = yb.at[pos.T.reshape(-1)].get(mode='promise_in_bounds')
    return yg.reshape(TOP_K, bsz, seq, d)


def _combine_kernel(*refs, final):
    if final:
        x_ref, y_ref, w_ref, g_ref, ng_ref, o_ref = refs
    else:
        x_ref, y_ref, w_ref, g_ref, o_ref = refs
    w = w_ref[0]
    y = y_ref[0, 0].astype(F32) * w[:, 0:1]
    for j in range(1, TOP_K):
        y = y + y_ref[j, 0].astype(F32) * w[:, j:j + 1]
    x = x_ref[0] + g_ref[0] * y
    if final:
        x = x * lax.rsqrt(jnp.mean(x * x, axis=-1, keepdims=True) + NORM_EPS) * ng_ref[...]
    o_ref[0] = x


def moe_combine(x, yg, top_w, gate, final_g=None, tm=512):
    bsz, seq, d = x.shape
    row = lambda b, i: (b, i, 0)
    specs = [pl.BlockSpec((1, tm, d), row),
             pl.BlockSpec((TOP_K, 1, tm, d), lambda b, i: (0, b, i, 0)),
             pl.BlockSpec((1, tm, LANES), row),
             pl.BlockSpec((1, 1, d), lambda b, i: (b, 0, 0))]
    args = [x, yg, top_w, gate]
    if final_g is not None:
        specs.append(pl.BlockSpec((1, d), lambda b, i: (0, 0)))
        args.append(final_g.reshape(1, d))
    return pl.pallas_call(
        functools.partial(_combine_kernel, final=final_g is not None),
        grid=(bsz, seq // tm),
        in_specs=specs,
        out_specs=pl.BlockSpec((1, tm, d), row),
        out_shape=jax.ShapeDtypeStruct((bsz, seq, d), F32),
        compiler_params=_params("parallel", "parallel"),
        name="moe_combine",
    )(*args)


S5_CHUNK = 512


def kernel(x, c, positions, ada_w, ada_b, norm_mix_g, norm_ffn_g, w_in, w_out,
           s5_lambda_re, s5_lambda_im, s5_log_step, s5_b_re, s5_b_im, s5_c_re, s5_c_im,
           s5_d, s5_glu_w, s5_glu_b,
           rw_mu, rw_w0, rw_w2, rw_a0, rw_a2, rw_g2, rw_k_k, rw_k_a, rw_r_k, rw_ln_g, rw_ln_b,
           rw_v0, rw_v1, rw_v2,
           mla_q_norm_g, mla_kv_norm_g, mla_w_q_up, mla_w_kv_up,
           router_w, router_b, ex_w_gate, ex_b_gate, ex_w_up, ex_b_up, ex_w_down, ex_b_down,
           final_norm_g):
    depth = w_in.shape[0]
    mod = adaln_mod(c, ada_w, ada_b)
    cos_t, sin_t = rope_tables(positions)
    v_first = None
    for l in range(depth):
        sh1, sc1, g1, sh2, sc2, g2 = [m[:, None, :] for m in jnp.split(mod[l], N_MOD, axis=-1)]
        s5_u, rw_in, mla_in, ret_in = in_proj(x, sc1, sh1, norm_mix_g[l], pack_w_in(w_in[l]))

        prep = s5_prepare(s5_lambda_re[l], s5_lambda_im[l], s5_log_step[l], s5_b_re[l], s5_b_im[l],
                          s5_c_re[l], s5_c_im[l], S5_CHUNK // SUBLANES)
        y_s5 = s5_mixer(s5_u, prep, s5_d[l], s5_glu_w[l], s5_glu_b[l], chunk=S5_CHUNK)

        if l == 0:
            w2p, a2p = rw_pack_weights(rw_w2[l], rw_a2[l])
            vmix = None
        else:
            w2p, a2p, v1p, v2p = rw_pack_weights(rw_w2[l], rw_a2[l], rw_v1[l - 1], rw_v2[l - 1])
            vmix = (v_first, rw_v0[l - 1], v1p, v2p)
        r_rw, lw_rw, k_rw, kk_rw, b_rw, v_rw, bonus, g_rw = rw_prep(
            rw_in, rw_mu[l], rw_w0[l], w2p, rw_a0[l], a2p, rw_g2[l], rw_k_k[l], rw_k_a[l],
            rw_r_k[l].reshape(-1), vmix)
        if l == 0:
            v_first = v_rw
        y_rw = wkv_sequential(*wkv_precompute(r_rw, lw_rw, k_rw, kk_rw, b_rw, v_rw))

        q, k, v = mla_prep(mla_in, cos_t, sin_t, mla_q_norm_g[l], mla_kv_norm_g[l],
                           mla_pack_weights(mla_w_q_up[l], mla_w_kv_up[l]))
        y_mla = causal_attention(q, k, v)

        y_ret = retention_mixer(ret_in, cos_t, sin_t)

        x, hn, top_idx, top_w, cnt = out_proj(x, y_s5, y_rw, bonus, g_rw, y_mla, y_ret, rw_ln_g[l], rw_ln_b[l],
                                              w_out[l], g1, norm_ffn_g[l], sc2, sh2, router_w[l], router_b[l])
        counts = jnp.sum(cnt[:, :, 0, :N_EXPERTS], axis=(0, 1)).astype(jnp.int32)
        yg = moe_ffn(hn, top_idx, counts, l, ex_w_gate, ex_b_gate, ex_w_up, ex_b_up, ex_w_down, ex_b_down)
        x = moe_combine(x, yg, top_w, g2, final_norm_g if l == depth - 1 else None)
    return x
```

```python
import functools
import math

import numpy as np
import jax
import jax.numpy as jnp
from jax import lax
from jax.experimental import pallas as pl
from jax.experimental.pallas import tpu as pltpu

F32 = jnp.float32
BF16 = jnp.bfloat16

D_MODEL = 1024
GROUP_WIDTH = 256
S5_CH = 16
S5_GROUPS = 16
S5_STATE = 64
S5_FLAT = S5_GROUPS * S5_STATE
RW_HEADS = 4
RW_HEAD = 64
RW_GN_EPS = 64e-5
MLA_HEADS = 4
MLA_NOPE = 64
MLA_ROPE = 32
MLA_V = 64
MLA_Q_RANK = 256
MLA_KV_RANK = 128
RET_HEADS = 4
RET_QK = 32
RET_V = 64
ROPE_BASE = 10000.0
N_EXPERTS = 32
TOP_K = 4
SWIGLU_ALPHA = 1.702
SWIGLU_LIMIT = 7.0
NORM_EPS = 1e-5
N_MOD = 6

LANES = 128
SUBLANES = 8
VMEM_LIMIT_BYTES = 56 * 1024 * 1024

IN_S5 = (0, 256)
IN_RW = (256, 1280)
IN_MLA = (1280, 1920)
IN_RET = (1920, 2944)
IN_PACKED = 2944


def _params(*sem):
    return pltpu.CompilerParams(dimension_semantics=sem, vmem_limit_bytes=VMEM_LIMIT_BYTES)


def _bdot(a, b):
    return jnp.dot(a.astype(BF16), b.astype(BF16), preferred_element_type=F32)


def _split_dot(a, b_bf16):
    hi = a.astype(BF16)
    lo = (a - hi.astype(F32)).astype(BF16)
    return (jnp.dot(hi, b_bf16, preferred_element_type=F32)
            + jnp.dot(lo, b_bf16, preferred_element_type=F32))


def _sigmoid(x):
    return 1.0 / (1.0 + jnp.exp(-x))


def _adaln_kernel(c_ref, w_ref, b_ref, o_ref):
    c = c_ref[...]
    cond = c * _sigmoid(c)
    o_ref[0] = _bdot(cond, w_ref[0]) + b_ref[0]


def adaln_mod(c, ada_w, ada_b):
    depth, d, n = ada_w.shape
    bsz = c.shape[0]
    tn = 1536
    return pl.pallas_call(
        _adaln_kernel,
        grid=(depth, n // tn),
        in_specs=[pl.BlockSpec((bsz, d), lambda l, j: (0, 0)),
                  pl.BlockSpec((1, d, tn), lambda l, j: (l, 0, j)),
                  pl.BlockSpec((1, 1, tn), lambda l, j: (l, 0, j))],
        out_specs=pl.BlockSpec((1, bsz, tn), lambda l, j: (l, 0, j)),
        out_shape=jax.ShapeDtypeStruct((depth, bsz, n), F32),
        compiler_params=_params("parallel", "parallel"),
        name="adaln_mod",
    )(c, ada_w, ada_b.reshape(depth, 1, n))


def _routed_sum(y_ref, w):
    y = y_ref[0, 0].astype(F32) * w[:, 0:1]
    for j in range(1, TOP_K):
        y = y + y_ref[j, 0].astype(F32) * w[:, j:j + 1]
    return y


def _inproj_kernel(*refs, ffn):
    if ffn:
        (x_ref, y_ref, tw_ref, g2_ref, sc_ref, sh_ref, g_ref, w_ref,
         xo_ref, s5_ref, rw_ref, mla_ref, ret_ref) = refs
        x = x_ref[0] + g2_ref[0] * _routed_sum(y_ref, tw_ref[0])
        xo_ref[0] = x
    else:
        x_ref, sc_ref, sh_ref, g_ref, w_ref, s5_ref, rw_ref, mla_ref, ret_ref = refs
        x = x_ref[0]
    ms = jnp.mean(x * x, axis=-1, keepdims=True)
    hn = x * lax.rsqrt(ms + NORM_EPS) * g_ref[...]
    hn = hn * (1.0 + sc_ref[0]) + sh_ref[0]
    p = jnp.dot(hn.astype(BF16), w_ref[...], preferred_element_type=F32)
    s5_ref[0] = p[:, IN_S5[0]:IN_S5[1]]
    rw_ref[0] = p[:, IN_RW[0]:IN_RW[1]]
    mla_ref[0] = p[:, IN_MLA[0]:IN_MLA[1]].astype(BF16)
    ret_ref[0] = p[:, IN_RET[0]:IN_RET[1]].astype(BF16)


def in_proj(x, sc, sh, g, w_packed, ffn=None, tm=512):
    bsz, seq, d = x.shape
    widths = [b - a for a, b in (IN_S5, IN_RW, IN_MLA, IN_RET)]
    row = lambda b, i: (b, i, 0)
    per_b = lambda b, i: (b, 0, 0)
    specs = [pl.BlockSpec((1, tm, d), row)]
    args = [x]
    out_specs = [pl.BlockSpec((1, tm, w), row) for w in widths]
    out_shape = [jax.ShapeDtypeStruct((bsz, seq, w), dt) for w, dt in zip(widths, (F32, F32, BF16, BF16))]
    if ffn is not None:
        yg, top_w, gate = ffn
        specs += [pl.BlockSpec((TOP_K, 1, tm, d), lambda b, i: (0, b, i, 0)), pl.BlockSpec((1, tm, LANES), row),
                  pl.BlockSpec((1, 1, d), per_b)]
        args += [yg, top_w, gate]
        out_specs = [pl.BlockSpec((1, tm, d), row)] + out_specs
        out_shape = [jax.ShapeDtypeStruct((bsz, seq, d), F32)] + out_shape
    specs += [pl.BlockSpec((1, 1, d), per_b), pl.BlockSpec((1, 1, d), per_b),
              pl.BlockSpec((1, d), lambda b, i: (0, 0)), pl.BlockSpec((d, IN_PACKED), lambda b, i: (0, 0))]
    args += [sc, sh, g.reshape(1, d), w_packed]
    return pl.pallas_call(
        functools.partial(_inproj_kernel, ffn=ffn is not None),
        grid=(bsz, seq // tm),
        in_specs=specs,
        out_specs=out_specs,
        out_shape=out_shape,
        compiler_params=_params("parallel", "parallel"),
        name="in_proj",
    )(*args)


def _swap_halves(cols, block):
    cols = np.asarray(cols).reshape(-1, 2, block // 2)
    return cols[:, ::-1, :].reshape(-1)


def pack_w_in(w_in_l):
    zero = w_in_l.shape[1]
    s5 = np.arange(0, 256)
    rw = np.arange(256, 1280)
    qc = np.arange(1280, 1536)
    kvc = np.arange(1536, 1664)
    kpe = np.arange(1664, 1696)
    z = lambda n: np.full((n,), zero)
    kpe_slot = np.concatenate([z(MLA_NOPE), kpe, z(LANES - MLA_NOPE - MLA_ROPE)])
    kpe_sw_slot = np.concatenate([z(MLA_NOPE), _swap_halves(kpe, MLA_ROPE), z(LANES - MLA_NOPE - MLA_ROPE)])
    rq = np.arange(1696, 1824)
    rk = np.arange(1824, 1952)
    rv = np.arange(1952, 2208)
    rg = np.arange(2208, 2464)
    idx = np.concatenate([s5, rw, qc, kvc, kpe_slot, kpe_sw_slot,
                          rq, rk, _swap_halves(rq, RET_QK), _swap_halves(rk, RET_QK), rv, rg])
    assert idx.shape[0] == IN_PACKED
    w_ext = jnp.concatenate([w_in_l, jnp.zeros((w_in_l.shape[0], 1), w_in_l.dtype)], axis=1)
    return jnp.take(w_ext, jnp.asarray(idx, jnp.int32), axis=1).astype(BF16)


def rope_tables(positions):
    inv = ROPE_BASE ** (-jnp.arange(0, MLA_ROPE, 2, dtype=F32) / MLA_ROPE)
    ang = positions.astype(F32)[..., None] * inv
    cos, sin = jnp.cos(ang), jnp.sin(ang)
    reps = LANES // MLA_ROPE
    cos_t = jnp.tile(jnp.concatenate([cos, cos], axis=-1), (1, 1, reps))
    sin_t = jnp.tile(jnp.concatenate([-sin, sin], axis=-1), (1, 1, reps))
    return cos_t, sin_t


def _s5_kernel(u_ref, bre_ref, bim_ref, cre_ref, cim_ref, lam_ref, lamq_ref, lamseg_ref,
               d_ref, gw_ref, gb_ref, o_ref, sre_ref, sim_ref, st_ref, *, nq):
    @pl.when(pl.program_id(1) == 0)
    def _():
        st_ref[...] = jnp.zeros_like(st_ref)

    u = u_ref[0]
    ub = u.astype(BF16)
    sre_ref[...] = jnp.dot(ub, bre_ref[...], preferred_element_type=F32)
    sim_ref[...] = jnp.dot(ub, bim_ref[...], preferred_element_type=F32)
    lam_re = lam_ref[0:1, :]
    lam_im = lam_ref[1:2, :]

    def scan_body(q, carry):
        cr, ci = carry
        rows = pl.ds(pl.multiple_of(q * SUBLANES, SUBLANES), SUBLANES)
        nr = lam_re * cr - lam_im * ci + sre_ref[rows, :]
        ni = lam_re * ci + lam_im * cr + sim_ref[rows, :]
        sre_ref[rows, :] = nr
        sim_ref[rows, :] = ni
        return nr, ni

    zero = jnp.zeros((SUBLANES, S5_FLAT), F32)
    end_re, end_im = lax.fori_loop(0, nq, scan_body, (zero, zero))

    seg_re = lamseg_ref[0:1, :]
    seg_im = lamseg_ref[1:2, :]
    cr, ci = st_ref[0:1, :], st_ref[1:2, :]
    in_re, in_im = [], []
    for r in range(SUBLANES):
        in_re.append(cr)
        in_im.append(ci)
        er, ei = end_re[r:r + 1, :], end_im[r:r + 1, :]
        cr, ci = seg_re * cr - seg_im * ci + er, seg_re * ci + seg_im * cr + ei
    st_ref[0:1, :] = cr
    st_ref[1:2, :] = ci
    car_re = jnp.concatenate(in_re, axis=0)
    car_im = jnp.concatenate(in_im, axis=0)

    def fix_body(q, _):
        rows = pl.ds(pl.multiple_of(q * SUBLANES, SUBLANES), SUBLANES)
        pr = lamq_ref[0, pl.ds(q, 1), :]
        pi = lamq_ref[1, pl.ds(q, 1), :]
        sre_ref[rows, :] = sre_ref[rows, :] + (pr * car_re - pi * car_im)
        sim_ref[rows, :] = sim_ref[rows, :] + (pr * car_im + pi * car_re)
        return 0

    lax.fori_loop(0, nq, fix_body, 0)

    y = (jnp.dot(sre_ref[...].astype(BF16), cre_ref[...], preferred_element_type=F32)
         - jnp.dot(sim_ref[...].astype(BF16), cim_ref[...], preferred_element_type=F32))
    y = y + d_ref[...] * u
    y = jax.nn.gelu(y)
    gate = jnp.dot(y.astype(BF16), gw_ref[...], preferred_element_type=F32) + gb_ref[...]
    o_ref[0] = y * _sigmoid(gate)


def s5_prepare(lam_re, lam_im, log_step, b_re, b_im, c_re, c_im, nq):
    dt = jnp.exp(log_step.astype(F32))[:, None]
    mag = jnp.exp(lam_re * dt)
    lb_re = mag * jnp.cos(lam_im * dt)
    lb_im = mag * jnp.sin(lam_im * dt)
    den = lam_re * lam_re + lam_im * lam_im
    n_re = lb_re - 1.0
    f_re = (n_re * lam_re + lb_im * lam_im) / den
    f_im = (lb_im * lam_re - n_re * lam_im) / den
    bb_re = f_re[..., None] * b_re - f_im[..., None] * b_im
    bb_im = f_re[..., None] * b_im + f_im[..., None] * b_re
    eye = jnp.eye(S5_GROUPS, dtype=F32)
    bd_in = lambda t: jnp.einsum('gph,gk->ghkp', t, eye).reshape(GROUP_WIDTH, S5_FLAT).astype(BF16)
    bd_out = lambda t: jnp.einsum('ghp,gk->gpkh', t, eye).reshape(S5_FLAT, GROUP_WIDTH).astype(BF16)
    lam = jnp.stack([lb_re.reshape(-1), lb_im.reshape(-1)])

    def power(n):
        n = n[:, None, None]
        m = jnp.exp(n * (lam_re * dt)[None])
        a = n * (lam_im * dt)[None]
        return jnp.stack([(m * jnp.cos(a)).reshape(-1, S5_FLAT), (m * jnp.sin(a)).reshape(-1, S5_FLAT)])

    lam_q = power(jnp.arange(1, nq + 1, dtype=F32))
    lam_seg = power(jnp.full((1,), float(nq), F32))[:, 0, :]
    return bd_in(bb_re), bd_in(bb_im), bd_out(c_re), bd_out(c_im), lam, lam_q, lam_seg


def s5_mixer(u, prep, d_skip, glu_w, glu_b, chunk=512):
    bsz, seq, w = u.shape
    nq = chunk // SUBLANES
    nchunk = seq // chunk
    bre, bim, cre, cim, lam, lam_q, lam_seg = prep
    up = u.reshape(bsz, nchunk, SUBLANES, nq, w).transpose(0, 1, 3, 2, 4).reshape(bsz, seq, w)
    const2 = lambda b, i: (0, 0)
    out = pl.pallas_call(
        functools.partial(_s5_kernel, nq=nq),
        grid=(bsz, nchunk),
        in_specs=[pl.BlockSpec((1, chunk, w), lambda b, i: (b, i, 0)),
                  pl.BlockSpec((w, S5_FLAT), const2),
                  pl.BlockSpec((w, S5_FLAT), const2),
                  pl.BlockSpec((S5_FLAT, w), const2),
                  pl.BlockSpec((S5_FLAT, w), const2),
                  pl.BlockSpec((2, S5_FLAT), const2),
                  pl.BlockSpec((2, nq, S5_FLAT), lambda b, i: (0, 0, 0)),
                  pl.BlockSpec((2, S5_FLAT), const2),
                  pl.BlockSpec((1, w), const2),
                  pl.BlockSpec((w, w), const2),
                  pl.BlockSpec((1, w), const2)],
        out_specs=pl.BlockSpec((1, chunk, w), lambda b, i: (b, i, 0)),
        out_shape=jax.ShapeDtypeStruct((bsz, seq, w), F32),
        scratch_shapes=[pltpu.VMEM((chunk, S5_FLAT), F32),
                        pltpu.VMEM((chunk, S5_FLAT), F32),
                        pltpu.VMEM((2, S5_FLAT), F32)],
        compiler_params=_params("parallel", "arbitrary"),
        name="s5_mixer",
    )(up, bre, bim, cre, cim, lam, lam_q, lam_seg,
      d_skip.reshape(1, w), glu_w.astype(BF16), glu_b.reshape(1, w))
    return out.reshape(bsz, nchunk, nq, SUBLANES, w).transpose(0, 1, 3, 2, 4).reshape(bsz, seq, w)


def _ret_kernel(h_ref, cos_ref, sin_ref, intra_ref, qw_ref, kw_ref, dec_ref, ones_ref,
                o_ref, st_ref, *, chunk):
    @pl.when(pl.program_id(1) == 0)
    def _():
        st_ref[...] = jnp.zeros_like(st_ref)

    h = h_ref[0].astype(F32)
    cos = cos_ref[0]
    sin = sin_ref[0]
    nqk = RET_HEADS * RET_QK
    q = h[:, 0:nqk] * cos + h[:, 2 * nqk:3 * nqk] * sin
    k = (h[:, nqk:2 * nqk] * cos + h[:, 3 * nqk:4 * nqk] * sin) * (RET_QK ** -0.5)
    v = h[:, 4 * nqk:4 * nqk + GROUP_WIDTH]
    g = h[:, 4 * nqk + GROUP_WIDTH:]
    kb = k.astype(BF16)
    lane_qk = lax.broadcasted_iota(jnp.int32, (chunk, nqk), 1) // RET_QK
    lane_v = lax.broadcasted_iota(jnp.int32, (chunk, GROUP_WIDTH), 1) // RET_V
    state = st_ref[...]
    o = _bdot(q * qw_ref[...], state)
    for hd in range(RET_HEADS):
        qh = jnp.where(lane_qk == hd, q, 0.0).astype(BF16)
        s = lax.dot_general(qh, kb, (((1,), (1,)), ((), ())), preferred_element_type=F32)
        s = s * intra_ref[hd]
        vh = jnp.where(lane_v == hd, v, 0.0).astype(BF16)
        o = o + jnp.dot(s.astype(BF16), vh, preferred_element_type=F32)
    kv = lax.dot_general((k * kw_ref[...]).astype(BF16), v.astype(BF16),
                         (((0,), (0,)), ((), ())), preferred_element_type=F32)
    dec = dec_ref[...]
    st_ref[...] = state * dec + jnp.where(dec > 0.0, kv, 0.0)
    ms = _split_dot(o * o, ones_ref[...])
    o = o * lax.rsqrt(ms + NORM_EPS)
    o_ref[0] = o * (g * _sigmoid(g))


def retention_tables(chunk):
    log_gamma = np.log1p(-np.exp2(-5.0 - np.arange(RET_HEADS, dtype=np.float64)))
    idx = np.arange(chunk, dtype=np.float64)
    diff = idx[:, None] - idx[None, :]
    intra = np.where(diff >= 0, np.exp(np.maximum(diff, 0.0)[None] * log_gamma[:, None, None]), 0.0)
    q_w = np.repeat(np.exp((idx + 1.0)[:, None] * log_gamma[None, :]), RET_QK, axis=1)
    k_w = np.repeat(np.exp((chunk - 1.0 - idx)[:, None] * log_gamma[None, :]), RET_QK, axis=1)
    head_q = np.arange(RET_HEADS * RET_QK) // RET_QK
    head_v = np.arange(GROUP_WIDTH) // RET_V
    same = head_q[:, None] == head_v[None, :]
    dec = np.where(same, np.exp(chunk * log_gamma)[head_q][:, None], 0.0)
    ones = (head_v[:, None] == head_v[None, :]).astype(np.float64) / RET_V
    f = lambda a: jnp.asarray(a, F32)
    return f(intra), f(q_w), f(k_w), f(dec), jnp.asarray(ones, BF16)


def retention_mixer(h, cos_t, sin_t, chunk=256):
    bsz, seq, wh = h.shape
    intra, q_w, k_w, dec, ones = retention_tables(chunk)
    nqk = RET_HEADS * RET_QK
    row = lambda b, i: (b, i, 0)
    c2 = lambda b, i: (0, 0)
    return pl.pallas_call(
        functools.partial(_ret_kernel, chunk=chunk),
        grid=(bsz, seq // chunk),
        in_specs=[pl.BlockSpec((1, chunk, wh), row),
                  pl.BlockSpec((1, chunk, LANES), row),
                  pl.BlockSpec((1, chunk, LANES), row),
                  pl.BlockSpec((RET_HEADS, chunk, chunk), lambda b, i: (0, 0, 0)),
                  pl.BlockSpec((chunk, nqk), c2),
                  pl.BlockSpec((chunk, nqk), c2),
                  pl.BlockSpec((nqk, GROUP_WIDTH), c2),
                  pl.BlockSpec((GROUP_WIDTH, GROUP_WIDTH), c2)],
        out_specs=pl.BlockSpec((1, chunk, GROUP_WIDTH), row),
        out_shape=jax.ShapeDtypeStruct((bsz, seq, GROUP_WIDTH), F32),
        scratch_shapes=[pltpu.VMEM((nqk, GROUP_WIDTH), F32)],
        compiler_params=_params("parallel", "arbitrary"),
        name="retention",
    )(h, cos_t, sin_t, intra, q_w, k_w, dec, ones)


def _mla_prep_kernel(h_ref, cos_ref, sin_ref, qg_ref, kvg_ref, wqa_ref, wqb_ref, wk_ref, wv_ref,
                     q_ref, k_ref, v_ref, *, scale):
    h = h_ref[0].astype(F32)
    tm = h.shape[0]
    lane = lax.broadcasted_iota(jnp.int32, (tm, LANES), 1)
    is_nope = lane < MLA_NOPE
    is_rope = jnp.logical_and(lane >= MLA_NOPE, lane < MLA_NOPE + MLA_ROPE)
    cm = jnp.where(is_nope, 1.0, jnp.where(is_rope, cos_ref[0], 0.0))
    sm = jnp.where(is_rope, sin_ref[0], 0.0)

    qc = h[:, 0:MLA_Q_RANK]
    qn = (qc * lax.rsqrt(jnp.mean(qc * qc, axis=-1, keepdims=True) + NORM_EPS) * qg_ref[...]).astype(BF16)
    kvc = h[:, MLA_Q_RANK:MLA_Q_RANK + MLA_KV_RANK]
    kvn = (kvc * lax.rsqrt(jnp.mean(kvc * kvc, axis=-1, keepdims=True) + NORM_EPS) * kvg_ref[...]).astype(BF16)
    off = MLA_Q_RANK + MLA_KV_RANK
    kpe = h[:, off:off + LANES] * cm + h[:, off + LANES:off + 2 * LANES] * sm

    qa = jnp.dot(qn, wqa_ref[...], preferred_element_type=F32)
    qb = jnp.dot(qn, wqb_ref[...], preferred_element_type=F32)
    kn = jnp.dot(kvn, wk_ref[...], preferred_element_type=F32)
    v_ref[0] = jnp.dot(kvn, wv_ref[...], preferred_element_type=F32).astype(BF16)
    for hd in range(MLA_HEADS):
        sl = slice(hd * LANES, (hd + 1) * LANES)
        q_ref[0, hd] = ((qa[:, sl] * cm + qb[:, sl] * sm) * scale).astype(BF16)
        k_ref[0, hd] = (kn[:, sl] + kpe).astype(BF16)


def mla_pack_weights(w_q_up, w_kv_up):
    dq = MLA_NOPE + MLA_ROPE
    zq = w_q_up.shape[1]
    zk = w_kv_up.shape[1]
    z = lambda n, zero: np.full((n,), zero)
    ia, ib, ik, iv = [], [], [], []
    for hd in range(MLA_HEADS):
        nope = np.arange(hd * dq, hd * dq + MLA_NOPE)
        pe = np.arange(hd * dq + MLA_NOPE, (hd + 1) * dq)
        pad = LANES - dq
        ia += [nope, pe, z(pad, zq)]
        ib += [z(MLA_NOPE, zq), _swap_halves(pe, MLA_ROPE), z(pad, zq)]
        kv0 = hd * (MLA_NOPE + MLA_V)
        ik += [np.arange(kv0, kv0 + MLA_NOPE), z(LANES - MLA_NOPE, zk)]
        iv += [np.arange(kv0 + MLA_NOPE, kv0 + MLA_NOPE + MLA_V)]
    ext = lambda w: jnp.concatenate([w, jnp.zeros((w.shape[0], 1), w.dtype)], axis=1)
    take = lambda w, idx: jnp.take(ext(w), jnp.asarray(np.concatenate(idx), jnp.int32), axis=1).astype(BF16)
    return take(w_q_up, ia), take(w_q_up, ib), take(w_kv_up, ik), take(w_kv_up, iv)


def mla_prep(h, cos_t, sin_t, q_norm_g, kv_norm_g, packed, tm=512):
    bsz, seq, wh = h.shape
    wqa, wqb, wk, wv = packed
    row = lambda b, i: (b, i, 0)
    c2 = lambda b, i: (0, 0)
    hrow = lambda b, i: (b, 0, i, 0)
    scale = (MLA_NOPE + MLA_ROPE) ** -0.5
    return pl.pallas_call(
        functools.partial(_mla_prep_kernel, scale=scale),
        grid=(bsz, seq // tm),
        in_specs=[pl.BlockSpec((1, tm, wh), row),
                  pl.BlockSpec((1, tm, LANES), row),
                  pl.BlockSpec((1, tm, LANES), row),
                  pl.BlockSpec((1, MLA_Q_RANK), c2),
                  pl.BlockSpec((1, MLA_KV_RANK), c2),
                  pl.BlockSpec(wqa.shape, c2),
                  pl.BlockSpec(wqb.shape, c2),
                  pl.BlockSpec(wk.shape, c2),
                  pl.BlockSpec(wv.shape, c2)],
        out_specs=[pl.BlockSpec((1, MLA_HEADS, tm, LANES), hrow),
                   pl.BlockSpec((1, MLA_HEADS, tm, LANES), hrow),
                   pl.BlockSpec((1, tm, GROUP_WIDTH), row)],
        out_shape=[jax.ShapeDtypeStruct((bsz, MLA_HEADS, seq, LANES), BF16),
                   jax.ShapeDtypeStruct((bsz, MLA_HEADS, seq, LANES), BF16),
                   jax.ShapeDtypeStruct((bsz, seq, GROUP_WIDTH), BF16)],
        compiler_params=_params("parallel", "parallel"),
        name="mla_prep",
    )(h, cos_t, sin_t, q_norm_g.reshape(1, -1), kv_norm_g.reshape(1, -1), wqa, wqb, wk, wv)


def _attn_kernel(q_ref, k_ref, v_ref, o_ref, m_ref, l_ref, acc_ref, *, blk):
    qi = pl.program_id(2)
    rep = blk // LANES

    heads = range(2)

    def step(j, masked):
        rows = pl.ds(pl.multiple_of(j * blk, blk), blk)
        vs = v_ref[0, rows, :]
        s = [lax.dot_general(q_ref[0, hh], k_ref[0, hh, rows, :], (((1,), (1,)), ((), ())),
                             preferred_element_type=F32) for hh in heads]
        if masked:
            r = lax.broadcasted_iota(jnp.int32, (blk, blk), 0)
            c = lax.broadcasted_iota(jnp.int32, (blk, blk), 1)
            s = [jnp.where(c <= r, t, -jnp.inf) for t in s]
        m_prev = [m_ref[hh] for hh in heads]
        m_new = [jnp.maximum(m_prev[hh], jnp.max(s[hh], axis=-1, keepdims=True)) for hh in heads]
        p = [jnp.exp(s[hh] - jnp.concatenate([m_new[hh]] * rep, axis=1)) for hh in heads]
        alpha = [jnp.exp(m_prev[hh] - m_new[hh]) for hh in heads]
        for hh in heads:
            l_ref[hh] = alpha[hh] * l_ref[hh] + jnp.sum(p[hh], axis=-1, keepdims=True)
            acc_ref[hh] = alpha[hh] * acc_ref[hh] + jnp.dot(p[hh].astype(BF16), vs, preferred_element_type=F32)
            m_ref[hh] = m_new[hh]

    for hh in heads:
        m_ref[hh] = jnp.full((blk, LANES), -jnp.inf, F32)
        l_ref[hh] = jnp.zeros((blk, LANES), F32)
        acc_ref[hh] = jnp.zeros((blk, LANES), F32)

    def body(j, _):
        step(j, False)
        return 0

    lax.fori_loop(0, qi, body, 0)
    step(qi, True)

    lane = lax.broadcasted_iota(jnp.int32, (blk, LANES), 1)
    o0 = acc_ref[0] / l_ref[0]
    o1 = acc_ref[1] / l_ref[1]
    o_ref[0] = jnp.where(lane < MLA_V, o0, o1)


def causal_attention(q, k, v, blk=1024):
    bsz, nh, seq, dk = q.shape
    return pl.pallas_call(
        functools.partial(_attn_kernel, blk=blk),
        grid=(bsz, nh // 2, seq // blk),
        in_specs=[pl.BlockSpec((1, 2, blk, dk), lambda b, p, i: (b, p, i, 0)),
                  pl.BlockSpec((1, 2, seq, dk), lambda b, p, i: (b, p, 0, 0)),
                  pl.BlockSpec((1, seq, LANES), lambda b, p, i: (b, 0, p))],
        out_specs=pl.BlockSpec((1, blk, LANES), lambda b, p, i: (b, i, p)),
        out_shape=jax.ShapeDtypeStruct((bsz, seq, nh // 2 * LANES), F32),
        scratch_shapes=[pltpu.VMEM((2, blk, LANES), F32)] * 3,
        compiler_params=_params("parallel", "parallel", "arbitrary"),
        name="mla_attention",
    )(q, k, v)


def _rw_prep_kernel(*refs, first):
    if first:
        (h_ref, prev_ref, mu_ref, w0_ref, w2_ref, a0_ref, a2_ref, g2_ref, kk_ref, ka_ref, rk_ref, ones_ref, tri_ref,
         q_ref, y0_ref, phi_ref, dlt_ref, v_ref, bonus_ref, g_ref) = refs
    else:
        (h_ref, prev_ref, mu_ref, w0_ref, w2_ref, a0_ref, a2_ref, g2_ref, kk_ref, ka_ref, rk_ref, ones_ref, tri_ref,
         vf_ref, v0_ref, v1_ref, v2_ref,
         q_ref, y0_ref, phi_ref, dlt_ref, v_ref, bonus_ref, g_ref) = refs
    h = h_ref[0]
    tm = h.shape[0]
    last = prev_ref[0, SUBLANES - 1:SUBLANES, :]
    last = jnp.where(pl.program_id(1) == 0, 0.0, last)
    row = lax.broadcasted_iota(jnp.int32, h.shape, 0)
    h_prev = jnp.where(row == 0, last, pltpu.roll(h, 1, axis=0))
    h = h + (h_prev - h) * mu_ref[...]
    w = GROUP_WIDTH
    r = h[:, 0:w]
    k = h[:, w:2 * w]
    v = h[:, 2 * w:3 * w]
    wa = h[:, 3 * w:3 * w + LANES]
    gd = h[:, 3 * w + LANES:]
    z = w0_ref[...] + _bdot(jnp.tanh(wa), w2_ref[...])
    nz = -z
    softplus = jnp.maximum(nz, 0.0) + jnp.log(1.0 + jnp.exp(-jnp.abs(nz)))
    log_decay = -jnp.exp(-softplus - 0.5)
    a = _sigmoid(a0_ref[...] + _bdot(wa, a2_ref[...]))
    g_ref[0] = _bdot(_sigmoid(gd), g2_ref[...])
    if not first:
        mix = _sigmoid(v0_ref[...] + _bdot(_bdot(v, v1_ref[...]), v2_ref[...]))
        v = v + (vf_ref[0] - v) * mix
    kk = k * kk_ref[...]
    norm = jnp.sqrt(_split_dot(kk * kk, ones_ref[...]))
    kk = kk / jnp.maximum(norm, 1e-12)
    k = k * (1.0 + (a - 1.0) * ka_ref[...])
    v_ref[0] = v
    bonus_ref[0] = _split_dot(r * k * rk_ref[...], ones_ref[...]) * v
    _wkv_chunk_terms(r, log_decay, k, kk, kk * a, v, tri_ref[...], q_ref.at[0], y0_ref.at[0], phi_ref.at[0], dlt_ref.at[0],
                     tm // WKV_CHUNK)


def rw_pack_weights(w2, a2, v1=None, v2=None):
    zeros = lambda n, m: jnp.zeros((n, m), F32)
    half = LANES // 2
    w2p = jnp.concatenate([w2, zeros(half, GROUP_WIDTH)], axis=0).astype(BF16)
    a2p = jnp.concatenate([zeros(half, GROUP_WIDTH), a2], axis=0).astype(BF16)
    if v1 is None:
        return w2p, a2p
    v1p = jnp.concatenate([v1, zeros(GROUP_WIDTH, LANES - v1.shape[1])], axis=1).astype(BF16)
    v2p = jnp.concatenate([v2, zeros(LANES - v2.shape[0], GROUP_WIDTH)], axis=0).astype(BF16)
    return w2p, a2p, v1p, v2p


def head_ones(width, head, scale):
    hd = np.arange(width) // head
    return jnp.asarray((hd[:, None] == hd[None, :]).astype(np.float32) * scale, BF16)


def rw_prep(h, mu, w0, w2p, a0, a2p, g2, k_k, k_a, r_k, vmix=None, tm=512):
    bsz, seq, wh = h.shape
    w = GROUP_WIDTH
    first = vmix is None
    row = lambda b, i: (b, i, 0)
    c2 = lambda b, i: (0, 0)
    vec = lambda t: t.reshape(1, -1)
    prev_map = lambda b, i: (b, jnp.maximum(i * (tm // SUBLANES) - 1, 0), 0)
    c = WKV_CHUNK
    npair = w // LANES
    tri = jnp.asarray(np.tril(np.ones((c, c), np.float32)), BF16)
    args = [h, h, vec(mu), vec(w0), w2p, vec(a0), a2p, g2.astype(BF16), vec(k_k), vec(k_a), vec(r_k),
            head_ones(w, RW_HEAD, 1.0), tri]
    specs = [pl.BlockSpec((1, tm, wh), row), pl.BlockSpec((1, SUBLANES, wh), prev_map),
             pl.BlockSpec((1, wh), c2), pl.BlockSpec((1, w), c2), pl.BlockSpec((LANES, w), c2),
             pl.BlockSpec((1, w), c2), pl.BlockSpec((LANES, w), c2), pl.BlockSpec((LANES, w), c2),
             pl.BlockSpec((1, w), c2), pl.BlockSpec((1, w), c2), pl.BlockSpec((1, w), c2),
             pl.BlockSpec((w, w), c2), pl.BlockSpec((c, c), c2)]
    if not first:
        v_first, v0, v1p, v2p = vmix
        args += [v_first, vec(v0), v1p, v2p]
        specs += [pl.BlockSpec((1, tm, w), row), pl.BlockSpec((1, w), c2),
                  pl.BlockSpec((w, LANES), c2), pl.BlockSpec((LANES, w), c2)]
    act = pl.BlockSpec((1, tm, w), row)
    mat = pl.BlockSpec((1, tm // c, npair, LANES, LANES), lambda b, i: (b, i, 0, 0, 0))
    ashape = jax.ShapeDtypeStruct((bsz, seq, w), F32)
    mshape = jax.ShapeDtypeStruct((bsz, seq // c, npair, LANES, LANES), F32)
    return pl.pallas_call(
        functools.partial(_rw_prep_kernel, first=first),
        grid=(bsz, seq // tm),
        in_specs=specs,
        out_specs=[act, act, mat, mat, act, act, act],
        out_shape=[ashape, ashape, mshape, mshape, ashape, ashape, ashape],
        compiler_params=_params("parallel", "parallel"),
        name="rwkv_prep",
    )(*args)


WKV_CHUNK = 64
_NN = (((1,), (0,)), ((), ()))
_NT = (((1,), (1,)), ((), ()))
_TN = (((0,), (0,)), ((), ()))


def _split(a):
    hi = a.astype(BF16)
    return hi, (a - hi.astype(F32)).astype(BF16)


def _mm(a, b, dims=_NN, passes=1):
    if passes == 1:
        return lax.dot_general(a.astype(BF16), b.astype(BF16), dims, preferred_element_type=F32)
    ah, al = _split(a)
    bh, bl = _split(b)
    d = lambda x, y: lax.dot_general(x, y, dims, preferred_element_type=F32)
    return d(ah, bh) + (d(ah, bl) + d(al, bh))


def _wkv_chunk_terms(r_all, lw_all, k_all, kk_all, b_all, v_all, tri, q_ref, y0_ref, phi_ref, dlt_ref, nchunk):
    c = WKV_CHUNK
    n2 = 2 * c
    lane = lax.broadcasted_iota(jnp.int32, (c, LANES), 1)
    head0 = lane < RW_HEAD
    row = lax.broadcasted_iota(jnp.int32, (n2, n2), 0)
    col = lax.broadcasted_iota(jnp.int32, (n2, n2), 1)
    strict = col < row
    incl = col <= row
    eye = (row == col).astype(F32)
    stack = lambda x: jnp.concatenate([jnp.where(head0, x, 0.0), jnp.where(head0, 0.0, x)], axis=0)
    fold = lambda x: x[:c] + x[c:]
    chains = [(ci, p) for ci in range(nchunk) for p in range(GROUP_WIDTH // LANES)]
    st = []
    for ci, p in chains:
        sl = slice(p * LANES, (p + 1) * LANES)
        rows = slice(ci * c, (ci + 1) * c)
        lw = lw_all[rows, sl]
        lw_hi, lw_lo = _split(lw)
        cum = (jnp.dot(tri, lw_hi, preferred_element_type=F32)
               + jnp.dot(tri, lw_lo, preferred_element_type=F32))
        last = cum[c - 1:c, :]
        e_in = jnp.exp(cum)
        e_ex = jnp.exp(cum - lw)
        e_neg = jnp.exp(-cum)
        e_end = jnp.exp(last - cum)
        kk, r, k, b, v = kk_all[rows, sl], r_all[rows, sl], k_all[rows, sl], b_all[rows, sl], v_all[rows, sl]
        kt2 = stack(kk * e_ex)
        rt2 = stack(r * e_in)
        lhs = jnp.concatenate([kt2, rt2], axis=0)
        gb = _mm(lhs, stack(b * e_neg), _NT)
        gk = _mm(lhs, stack(k * e_neg), _NT)
        x = -jnp.where(strict, gb[:n2], 0.0)
        st.append(dict(sl=sl, rows=rows, kt2=kt2, rt2=rt2, kp2=stack(k * e_end), bp2=stack(b * e_end), v2=stack(v),
                       g2=jnp.where(strict, gk[:n2], 0.0), rb2=jnp.where(incl, gb[n2:], 0.0),
                       rk2=jnp.where(incl, gk[n2:], 0.0), last=last, x=x, t=eye + x))
    for _ in range(int(np.log2(c)) - 1):
        for d in st:
            d["x"] = _mm(d["x"], d["x"])
        for d in st:
            d["t"] = d["t"] + _mm(d["t"], d["x"])
    for d in st:
        d["gv2"] = _mm(d["g2"], d["v2"])
    for d in st:
        d["tku"] = _mm(d["t"], jnp.concatenate([d["kt2"], d["gv2"]], axis=1))
    for d in st:
        d["rbz"] = _mm(d["rb2"], d["tku"])
    for (ci, p), d in zip(chains, st):
        sl, rows, tku, rbz = d["sl"], d["rows"], d["tku"], d["rbz"]
        q_ref[rows, sl] = fold(d["rt2"] - rbz[:, :LANES])
        y0_ref[rows, sl] = fold(_mm(d["rk2"], d["v2"]) - rbz[:, LANES:])
        decay = jnp.where(row == col, jnp.broadcast_to(jnp.exp(d["last"]), (n2, LANES)), 0.0)
        phi_ref[ci, p] = decay - _mm(d["bp2"], tku[:, :LANES], _TN)
        dlt_ref[ci, p] = _mm(d["kp2"], d["v2"], _TN) - _mm(d["bp2"], tku[:, LANES:], _TN)


def _wkv_seq_kernel(q_ref, y0_ref, phi_ref, dlt_ref, y_ref, st_ref, *, nchunk):
    @pl.when(pl.program_id(1) == 0)
    def _():
        st_ref[...] = jnp.zeros_like(st_ref)

    c = WKV_CHUNK
    npair = GROUP_WIDTH // LANES
    m = [st_ref[p] for p in range(npair)]
    for i in range(nchunk):
        rows = slice(i * c, (i + 1) * c)
        for p in range(npair):
            sl = slice(p * LANES, (p + 1) * LANES)
            y_ref[0, rows, sl] = _mm(q_ref[0, rows, sl], m[p], _NN, 3) + y0_ref[0, rows, sl]
        m = [_mm(phi_ref[0, i, p], m[p], _NN, 3) + dlt_ref[0, i, p] for p in range(npair)]
    for p in range(npair):
        st_ref[p] = m[p]


def wkv_sequential(q, y0, phi, dlt, block=512):
    bsz, seq, w = q.shape
    c = WKV_CHUNK
    nchunk = block // c
    npair = w // LANES
    row = lambda bb, i: (bb, i, 0)
    blk = pl.BlockSpec((1, block, w), row)
    mat = pl.BlockSpec((1, nchunk, npair, LANES, LANES), lambda bb, i: (bb, i, 0, 0, 0))
    return pl.pallas_call(
        functools.partial(_wkv_seq_kernel, nchunk=nchunk),
        grid=(bsz, seq // block),
        in_specs=[blk, blk, mat, mat],
        out_specs=blk,
        out_shape=jax.ShapeDtypeStruct((bsz, seq, w), F32),
        scratch_shapes=[pltpu.VMEM((npair, LANES, LANES), F32)],
        compiler_params=_params("parallel", "arbitrary"),
        name="wkv7_chunk_scan",
    )(q, y0, phi, dlt)


def _outproj_kernel(x_ref, ys5_ref, yrw_ref, bonus_ref, grw_ref, ymla_ref, yret_ref, lng_ref, lnb_ref, ones_ref,
                    wout_ref, g1_ref, ng_ref, sc_ref, sh_ref, rhi_ref, rlo_ref, rb_ref,
                    xo_ref, hn_ref, idx_ref, tw_ref, cnt_ref):
    y = yrw_ref[0]
    mean = _split_dot(y, ones_ref[...])
    yc = y - mean
    var = _split_dot(yc * yc, ones_ref[...])
    yrw = (yc * lax.rsqrt(var + RW_GN_EPS) * lng_ref[...] + lnb_ref[...] + bonus_ref[0]) * grw_ref[0]
    w = GROUP_WIDTH
    mixed = (_bdot(ys5_ref[0], wout_ref[0:w, :]) + _bdot(yrw, wout_ref[w:2 * w, :])
             + _bdot(ymla_ref[0], wout_ref[2 * w:3 * w, :]) + _bdot(yret_ref[0], wout_ref[3 * w:4 * w, :]))
    x = x_ref[0] + g1_ref[0] * mixed
    xo_ref[0] = x
    ms = jnp.mean(x * x, axis=-1, keepdims=True)
    hn = x * lax.rsqrt(ms + NORM_EPS) * ng_ref[...]
    hn = hn * (1.0 + sc_ref[0]) + sh_ref[0]
    hi = hn.astype(BF16)
    hn_ref[0] = hi
    lo = (hn - hi.astype(F32)).astype(BF16)
    logits = (jnp.dot(hi, rhi_ref[...], preferred_element_type=F32)
              + jnp.dot(lo, rhi_ref[...], preferred_element_type=F32)
              + jnp.dot(hi, rlo_ref[...], preferred_element_type=F32) + rb_ref[...])
    lane = lax.broadcasted_iota(jnp.int32, logits.shape, 1)
    lane_f = lane.astype(F32)
    cur = jnp.where(lane < N_EXPERTS, logits, -jnp.inf)
    idx_out = jnp.zeros(logits.shape, F32)
    val_out = jnp.zeros(logits.shape, F32)
    picked = jnp.zeros(logits.shape, F32)
    top = None
    denom = None
    for j in range(TOP_K):
        m = jnp.max(cur, axis=-1, keepdims=True)
        sel = jnp.min(jnp.where(cur == m, lane_f, float(LANES)), axis=-1, keepdims=True)
        hit = lane_f == sel
        cur = jnp.where(hit, -jnp.inf, cur)
        picked = picked + jnp.where(hit, 1.0, 0.0)
        top = m if top is None else top
        e = jnp.exp(m - top)
        denom = e if denom is None else denom + e
        idx_out = jnp.where(lane == j, sel, idx_out)
        val_out = jnp.where(lane == j, e, val_out)
    idx_ref[0] = idx_out.astype(jnp.int32)
    tw_ref[0] = val_out / denom
    cnt_ref[0, 0] = jnp.broadcast_to(jnp.sum(picked, axis=0, keepdims=True), (SUBLANES, LANES))


def out_proj(x, y_s5, y_rw, bonus, g_rw, y_mla, y_ret, ln_g, ln_b, w_out, g1, norm_g, sc2, sh2,
             router_w, router_b, tm=512):
    bsz, seq, d = x.shape
    w = GROUP_WIDTH
    row = lambda b, i: (b, i, 0)
    per_b = lambda b, i: (b, 0, 0)
    c2 = lambda b, i: (0, 0)
    vec = lambda t: t.reshape(1, -1)
    pad = LANES - router_w.shape[1]
    rw_pad = jnp.concatenate([router_w, jnp.zeros((d, pad), F32)], axis=1)
    r_hi = rw_pad.astype(BF16)
    r_lo = (rw_pad - r_hi.astype(F32)).astype(BF16)
    rb = jnp.concatenate([router_b, jnp.zeros((pad,), F32)]).reshape(1, LANES)
    mixer = pl.BlockSpec((1, tm, w), row)
    return pl.pallas_call(
        _outproj_kernel,
        grid=(bsz, seq // tm),
        in_specs=[pl.BlockSpec((1, tm, d), row), mixer, mixer, mixer, mixer, mixer, mixer,
                  pl.BlockSpec((1, w), c2), pl.BlockSpec((1, w), c2), pl.BlockSpec((w, w), c2),
                  pl.BlockSpec((4 * w, d), c2),
                  pl.BlockSpec((1, 1, d), per_b), pl.BlockSpec((1, d), c2),
                  pl.BlockSpec((1, 1, d), per_b), pl.BlockSpec((1, 1, d), per_b),
                  pl.BlockSpec((d, LANES), c2), pl.BlockSpec((d, LANES), c2), pl.BlockSpec((1, LANES), c2)],
        out_specs=[pl.BlockSpec((1, tm, d), row), pl.BlockSpec((1, tm, d), row), pl.BlockSpec((1, tm, LANES), row),
                   pl.BlockSpec((1, tm, LANES), row), pl.BlockSpec((1, 1, SUBLANES, LANES), lambda b, i: (b, i, 0, 0))],
        out_shape=[jax.ShapeDtypeStruct((bsz, seq, d), F32), jax.ShapeDtypeStruct((bsz, seq, d), BF16),
                   jax.ShapeDtypeStruct((bsz, seq, LANES), jnp.int32), jax.ShapeDtypeStruct((bsz, seq, LANES), F32),
                   jax.ShapeDtypeStruct((bsz, seq // tm, SUBLANES, LANES), F32)],
        compiler_params=_params("parallel", "parallel"),
        name="out_proj",
    )(x, y_s5, y_rw, bonus, g_rw, y_mla, y_ret, vec(ln_g), vec(ln_b), head_ones(w, RW_HEAD, 1.0 / RW_HEAD),
      w_out.astype(BF16), g1, vec(norm_g), sc2, sh2, r_hi, r_lo, rb)


MOE_ROWS = 512


def _moe_kernel(blk_e_ref, blk_on_ref, x_ref, wg_ref, bg_ref, wu_ref, bu_ref, wd_ref, bd_ref, o_ref, wb_ref):
    i = pl.program_id(0)
    changed = jnp.logical_or(i == 0, blk_e_ref[i] != blk_e_ref[jnp.maximum(i - 1, 0)])

    @pl.when(changed)
    def _():
        wb_ref[0] = wg_ref[0, 0].astype(BF16)
        wb_ref[1] = wu_ref[0, 0].astype(BF16)
        wb_ref[2] = wd_ref[0, 0].astype(BF16)

    @pl.when(blk_on_ref[i] > 0)
    def _():
        x = x_ref[...]
        gt = jnp.minimum(jnp.dot(x, wb_ref[0], preferred_element_type=F32) + bg_ref[0, 0], SWIGLU_LIMIT)
        up = jnp.clip(jnp.dot(x, wb_ref[1], preferred_element_type=F32) + bu_ref[0, 0], -SWIGLU_LIMIT, SWIGLU_LIMIT)
        act = gt * _sigmoid(SWIGLU_ALPHA * gt) * (up + 1.0)
        o_ref[...] = (jnp.dot(act.astype(BF16), wb_ref[2], preferred_element_type=F32) + bd_ref[0, 0]).astype(BF16)


def moe_experts(xb, blk_e, blk_on, layer, w_gate, b_gate, w_up, b_up, w_down, b_down):
    p_rows, d = xb.shape
    depth, n_e, _, de = w_gate.shape
    wmap = lambda i, e, on: (layer, e[i], 0, 0)
    rows = lambda i, e, on: (i, 0)
    return pl.pallas_call(
        _moe_kernel,
        grid_spec=pltpu.PrefetchScalarGridSpec(
            num_scalar_prefetch=2,
            grid=(p_rows // MOE_ROWS,),
            in_specs=[pl.BlockSpec((MOE_ROWS, d), rows),
                      pl.BlockSpec((1, 1, d, de), wmap), pl.BlockSpec((1, 1, 1, de), wmap),
                      pl.BlockSpec((1, 1, d, de), wmap), pl.BlockSpec((1, 1, 1, de), wmap),
                      pl.BlockSpec((1, 1, de, d), wmap), pl.BlockSpec((1, 1, 1, d), wmap)],
            out_specs=pl.BlockSpec((MOE_ROWS, d), rows),
            scratch_shapes=[pltpu.VMEM((3, d, de), BF16)]),
        out_shape=jax.ShapeDtypeStruct((p_rows, d), BF16),
        compiler_params=_params("arbitrary"),
        name="moe_experts",
    )(blk_e, blk_on, xb, w_gate, b_gate.reshape(depth, n_e, 1, de), w_up, b_up.reshape(depth, n_e, 1, de),
      w_down, b_down.reshape(depth, n_e, 1, d))


def moe_route(top_idx, counts):
    t = top_idx.shape[0]
    n_assign = t * TOP_K
    flat_e = top_idx.reshape(-1)
    iota = jnp.arange(n_assign, dtype=jnp.int32)
    _, order = lax.sort((flat_e, iota), num_keys=1)
    _, rank = lax.sort((order, iota), num_keys=1)
    start = jnp.cumsum(counts) - counts
    padded = (counts + MOE_ROWS - 1) // MOE_ROWS * MOE_ROWS
    pad_end = jnp.cumsum(padded)
    pad_start = pad_end - padded
    pos = rank + (pad_start - start)[flat_e]
    p_rows = n_assign + N_EXPERTS * MOE_ROWS
    n_blocks = p_rows // MOE_ROWS
    blk_first = jnp.arange(n_blocks, dtype=jnp.int32) * MOE_ROWS
    blk_e = jnp.minimum(jnp.sum(pad_end[None, :] <= blk_first[:, None], axis=1, dtype=jnp.int32), N_EXPERTS - 1)
    blk_within = blk_first - pad_start[blk_e]
    blk_left = counts[blk_e] - blk_within
    blk_on = (blk_left > 0).astype(jnp.int32)
    r = jnp.arange(MOE_ROWS, dtype=jnp.int32)[None, :]
    src = jnp.clip((start[blk_e] + blk_within)[:, None] + r, 0, n_assign - 1)
    row = blk_first[:, None] + r
    buf_tok = jnp.where(r < blk_left[:, None], order[src.reshape(-1)].reshape(n_blocks, MOE_ROWS) // TOP_K, row % t)
    return buf_tok.reshape(-1), pos.reshape(t, TOP_K), blk_e, blk_on


def moe_ffn(hn_bf16, top_idx, counts, layer, w_gate, b_gate, w_up, b_up, w_down, b_down):
    bsz, seq, d = hn_bf16.shape
    t = bsz * seq
    buf_tok, pos, blk_e, blk_on = moe_route(top_idx.reshape(t, LANES)[:, :TOP_K], counts)
    xb = hn_bf16.reshape(t, d).at[buf_tok].get(mode='promise_in_bounds')
    yb = moe_experts(xb, blk_e, blk_on, layer, w_gate, b_gate, w_up, b_up, w_down, b_down)
    yg = yb.at[pos.T.reshape(-1)].get(mode='promise_in_bounds')
    return yg.reshape(TOP_K, bsz, seq, d)


def _final_kernel(x_ref, y_ref, w_ref, g_ref, ng_ref, o_ref):
    x = x_ref[0] + g_ref[0] * _routed_sum(y_ref, w_ref[0])
    o_ref[0] = x * lax.rsqrt(jnp.mean(x * x, axis=-1, keepdims=True) + NORM_EPS) * ng_ref[...]


def final_combine_norm(x, yg, top_w, gate, final_g, tm=512):
    bsz, seq, d = x.shape
    row = lambda b, i: (b, i, 0)
    return pl.pallas_call(
        _final_kernel,
        grid=(bsz, seq // tm),
        in_specs=[pl.BlockSpec((1, tm, d), row),
                  pl.BlockSpec((TOP_K, 1, tm, d), lambda b, i: (0, b, i, 0)),
                  pl.BlockSpec((1, tm, LANES), row),
                  pl.BlockSpec((1, 1, d), lambda b, i: (b, 0, 0)),
                  pl.BlockSpec((1, d), lambda b, i: (0, 0))],
        out_specs=pl.BlockSpec((1, tm, d), row),
        out_shape=jax.ShapeDtypeStruct((bsz, seq, d), F32),
        compiler_params=_params("parallel", "parallel"),
        name="final_combine_norm",
    )(x, yg, top_w, gate, final_g.reshape(1, d))


S5_CHUNK = 512


def kernel(x, c, positions, ada_w, ada_b, norm_mix_g, norm_ffn_g, w_in, w_out,
           s5_lambda_re, s5_lambda_im, s5_log_step, s5_b_re, s5_b_im, s5_c_re, s5_c_im,
           s5_d, s5_glu_w, s5_glu_b,
           rw_mu, rw_w0, rw_w2, rw_a0, rw_a2, rw_g2, rw_k_k, rw_k_a, rw_r_k, rw_ln_g, rw_ln_b,
           rw_v0, rw_v1, rw_v2,
           mla_q_norm_g, mla_kv_norm_g, mla_w_q_up, mla_w_kv_up,
           router_w, router_b, ex_w_gate, ex_b_gate, ex_w_up, ex_b_up, ex_w_down, ex_b_down,
           final_norm_g):
    depth = w_in.shape[0]
    mod = adaln_mod(c, ada_w, ada_b)
    cos_t, sin_t = rope_tables(positions)
    v_first = None
    ffn = None
    for l in range(depth):
        sh1, sc1, g1, sh2, sc2, g2 = [m[:, None, :] for m in jnp.split(mod[l], N_MOD, axis=-1)]
        proj = in_proj(x, sc1, sh1, norm_mix_g[l], pack_w_in(w_in[l]), ffn)
        if ffn is not None:
            x = proj[0]
        s5_u, rw_in, mla_in, ret_in = proj[-4:]

        prep = s5_prepare(s5_lambda_re[l], s5_lambda_im[l], s5_log_step[l], s5_b_re[l], s5_b_im[l],
                          s5_c_re[l], s5_c_im[l], S5_CHUNK // SUBLANES)
        y_s5 = s5_mixer(s5_u, prep, s5_d[l], s5_glu_w[l], s5_glu_b[l], chunk=S5_CHUNK)

        if l == 0:
            w2p, a2p = rw_pack_weights(rw_w2[l], rw_a2[l])
            vmix = None
        else:
            w2p, a2p, v1p, v2p = rw_pack_weights(rw_w2[l], rw_a2[l], rw_v1[l - 1], rw_v2[l - 1])
            vmix = (v_first, rw_v0[l - 1], v1p, v2p)
        q_rw, y0_rw, phi_rw, dlt_rw, v_rw, bonus, g_rw = rw_prep(
            rw_in, rw_mu[l], rw_w0[l], w2p, rw_a0[l], a2p, rw_g2[l], rw_k_k[l], rw_k_a[l],
            rw_r_k[l].reshape(-1), vmix)
        if l == 0:
            v_first = v_rw
        y_rw = wkv_sequential(q_rw, y0_rw, phi_rw, dlt_rw)

        q, k, v = mla_prep(mla_in, cos_t, sin_t, mla_q_norm_g[l], mla_kv_norm_g[l],
                           mla_pack_weights(mla_w_q_up[l], mla_w_kv_up[l]))
        y_mla = causal_attention(q, k, v)

        y_ret = retention_mixer(ret_in, cos_t, sin_t)

        x, hn, top_idx, top_w, cnt = out_proj(x, y_s5, y_rw, bonus, g_rw, y_mla, y_ret, rw_ln_g[l], rw_ln_b[l],
                                              w_out[l], g1, norm_ffn_g[l], sc2, sh2, router_w[l], router_b[l])
        counts = jnp.sum(cnt[:, :, 0, :N_EXPERTS], axis=(0, 1)).astype(jnp.int32)
        yg = moe_ffn(hn, top_idx, counts, l, ex_w_gate, ex_b_gate, ex_w_up, ex_b_up, ex_w_down, ex_b_down)
        ffn = (yg, top_w, g2)
    return final_combine_norm(x, *ffn, final_norm_g)
```

```python
import functools
import math

import numpy as np
import jax
import jax.numpy as jnp
from jax import lax
from jax.experimental import pallas as pl
from jax.experimental.pallas import tpu as pltpu

F32 = jnp.float32
BF16 = jnp.bfloat16

D_MODEL = 1024
GROUP_WIDTH = 256
S5_CH = 16
S5_GROUPS = 16
S5_STATE = 64
S5_FLAT = S5_GROUPS * S5_STATE
RW_HEADS = 4
RW_HEAD = 64
RW_GN_EPS = 64e-5
MLA_HEADS = 4
MLA_NOPE = 64
MLA_ROPE = 32
MLA_V = 64
MLA_Q_RANK = 256
MLA_KV_RANK = 128
RET_HEADS = 4
RET_QK = 32
RET_V = 64
ROPE_BASE = 10000.0
N_EXPERTS = 32
TOP_K = 4
SWIGLU_ALPHA = 1.702
SWIGLU_LIMIT = 7.0
NORM_EPS = 1e-5
N_MOD = 6

LANES = 128
SUBLANES = 8
VMEM_LIMIT_BYTES = 56 * 1024 * 1024

IN_S5 = (0, 256)
IN_RW = (256, 1280)
IN_MLA = (1280, 1920)
IN_RET = (1920, 2944)
IN_PACKED = 2944


def _params(*sem):
    return pltpu.CompilerParams(dimension_semantics=sem, vmem_limit_bytes=VMEM_LIMIT_BYTES)


def _bdot(a, b):
    return jnp.dot(a.astype(BF16), b.astype(BF16), preferred_element_type=F32)


def _split_dot(a, b_bf16):
    hi = a.astype(BF16)
    lo = (a - hi.astype(F32)).astype(BF16)
    return (jnp.dot(hi, b_bf16, preferred_element_type=F32)
            + jnp.dot(lo, b_bf16, preferred_element_type=F32))


def _sigmoid(x):
    return 1.0 / (1.0 + jnp.exp(-x))


def _adaln_kernel(c_ref, w_ref, b_ref, o_ref):
    c = c_ref[...]
    cond = c * _sigmoid(c)
    o_ref[0] = _bdot(cond, w_ref[0]) + b_ref[0]


def adaln_mod(c, ada_w, ada_b):
    depth, d, n = ada_w.shape
    bsz = c.shape[0]
    tn = 1536
    return pl.pallas_call(
        _adaln_kernel,
        grid=(depth, n // tn),
        in_specs=[pl.BlockSpec((bsz, d), lambda l, j: (0, 0)),
                  pl.BlockSpec((1, d, tn), lambda l, j: (l, 0, j)),
                  pl.BlockSpec((1, 1, tn), lambda l, j: (l, 0, j))],
        out_specs=pl.BlockSpec((1, bsz, tn), lambda l, j: (l, 0, j)),
        out_shape=jax.ShapeDtypeStruct((depth, bsz, n), F32),
        compiler_params=_params("parallel", "parallel"),
        name="adaln_mod",
    )(c, ada_w, ada_b.reshape(depth, 1, n))


def _routed_sum(y_ref, w):
    y = y_ref[0, 0].astype(F32) * w[:, 0:1]
    for j in range(1, TOP_K):
        y = y + y_ref[j, 0].astype(F32) * w[:, j:j + 1]
    return y


def _inproj_kernel(*refs, ffn):
    if ffn:
        (x_ref, y_ref, tw_ref, g2_ref, sc_ref, sh_ref, g_ref, w_ref,
         xo_ref, s5_ref, rw_ref, mla_ref, ret_ref) = refs
        x = x_ref[0] + g2_ref[0] * _routed_sum(y_ref, tw_ref[0])
        xo_ref[0] = x
    else:
        x_ref, sc_ref, sh_ref, g_ref, w_ref, s5_ref, rw_ref, mla_ref, ret_ref = refs
        x = x_ref[0]
    ms = jnp.mean(x * x, axis=-1, keepdims=True)
    hn = x * lax.rsqrt(ms + NORM_EPS) * g_ref[...]
    hn = hn * (1.0 + sc_ref[0]) + sh_ref[0]
    p = jnp.dot(hn.astype(BF16), w_ref[...], preferred_element_type=F32)
    s5_ref[0] = p[:, IN_S5[0]:IN_S5[1]]
    rw_ref[0] = p[:, IN_RW[0]:IN_RW[1]]
    mla_ref[0] = p[:, IN_MLA[0]:IN_MLA[1]].astype(BF16)
    ret_ref[0] = p[:, IN_RET[0]:IN_RET[1]].astype(BF16)


def in_proj(x, sc, sh, g, w_packed, ffn=None, tm=512):
    bsz, seq, d = x.shape
    widths = [b - a for a, b in (IN_S5, IN_RW, IN_MLA, IN_RET)]
    row = lambda b, i: (b, i, 0)
    per_b = lambda b, i: (b, 0, 0)
    specs = [pl.BlockSpec((1, tm, d), row)]
    args = [x]
    out_specs = [pl.BlockSpec((1, tm, w), row) for w in widths]
    out_shape = [jax.ShapeDtypeStruct((bsz, seq, w), dt) for w, dt in zip(widths, (F32, F32, BF16, BF16))]
    if ffn is not None:
        yg, top_w, gate = ffn
        specs += [pl.BlockSpec((TOP_K, 1, tm, d), lambda b, i: (0, b, i, 0)), pl.BlockSpec((1, tm, LANES), row),
                  pl.BlockSpec((1, 1, d), per_b)]
        args += [yg, top_w, gate]
        out_specs = [pl.BlockSpec((1, tm, d), row)] + out_specs
        out_shape = [jax.ShapeDtypeStruct((bsz, seq, d), F32)] + out_shape
    specs += [pl.BlockSpec((1, 1, d), per_b), pl.BlockSpec((1, 1, d), per_b),
              pl.BlockSpec((1, d), lambda b, i: (0, 0)), pl.BlockSpec((d, IN_PACKED), lambda b, i: (0, 0))]
    args += [sc, sh, g.reshape(1, d), w_packed]
    return pl.pallas_call(
        functools.partial(_inproj_kernel, ffn=ffn is not None),
        grid=(bsz, seq // tm),
        in_specs=specs,
        out_specs=out_specs,
        out_shape=out_shape,
        compiler_params=_params("parallel", "parallel"),
        name="in_proj",
    )(*args)


def _swap_halves(cols, block):
    cols = np.asarray(cols).reshape(-1, 2, block // 2)
    return cols[:, ::-1, :].reshape(-1)


def pack_w_in(w_in_l):
    zero = w_in_l.shape[1]
    s5 = np.arange(0, 256)
    rw = np.arange(256, 1280)
    qc = np.arange(1280, 1536)
    kvc = np.arange(1536, 1664)
    kpe = np.arange(1664, 1696)
    z = lambda n: np.full((n,), zero)
    kpe_slot = np.concatenate([z(MLA_NOPE), kpe, z(LANES - MLA_NOPE - MLA_ROPE)])
    kpe_sw_slot = np.concatenate([z(MLA_NOPE), _swap_halves(kpe, MLA_ROPE), z(LANES - MLA_NOPE - MLA_ROPE)])
    rq = np.arange(1696, 1824)
    rk = np.arange(1824, 1952)
    rv = np.arange(1952, 2208)
    rg = np.arange(2208, 2464)
    idx = np.concatenate([s5, rw, qc, kvc, kpe_slot, kpe_sw_slot,
                          rq, rk, _swap_halves(rq, RET_QK), _swap_halves(rk, RET_QK), rv, rg])
    assert idx.shape[0] == IN_PACKED
    w_ext = jnp.concatenate([w_in_l, jnp.zeros((w_in_l.shape[0], 1), w_in_l.dtype)], axis=1)
    return jnp.take(w_ext, jnp.asarray(idx, jnp.int32), axis=1).astype(BF16)


def _rope_kernel(pos_ref, inv_ref, sgn_ref, cos_ref, sin_ref):
    ang = pos_ref[0].astype(F32) * inv_ref[...]
    cos_ref[0] = jnp.cos(ang)
    sin_ref[0] = jnp.sin(ang) * sgn_ref[...]


def rope_tables(positions, tm=1024):
    bsz, seq = positions.shape
    half = MLA_ROPE // 2
    inv = ROPE_BASE ** (-jnp.arange(0, MLA_ROPE, 2, dtype=F32) / MLA_ROPE)
    inv_t = jnp.tile(inv, LANES // half).reshape(1, LANES)
    sgn = jnp.asarray(np.where((np.arange(LANES) % MLA_ROPE) < half, -1.0, 1.0), F32).reshape(1, LANES)
    row = lambda b, i: (b, i, 0)
    c2 = lambda b, i: (0, 0)
    return pl.pallas_call(
        _rope_kernel,
        grid=(bsz, seq // tm),
        in_specs=[pl.BlockSpec((1, tm, 1), row), pl.BlockSpec((1, LANES), c2), pl.BlockSpec((1, LANES), c2)],
        out_specs=[pl.BlockSpec((1, tm, LANES), row)] * 2,
        out_shape=[jax.ShapeDtypeStruct((bsz, seq, LANES), F32)] * 2,
        compiler_params=_params("parallel", "parallel"),
        name="rope_tables",
    )(positions.reshape(bsz, seq, 1), inv_t, sgn)


S5_UNROLL = 8


def _s5_kernel(u_ref, bre_ref, bim_ref, cre_ref, cim_ref, lam_ref, lamq_ref, lamseg_ref,
               d_ref, gw_ref, gb_ref, o_ref, sre_ref, sim_ref, st_ref, *, nq):
    @pl.when(pl.program_id(1) == 0)
    def _():
        st_ref[...] = jnp.zeros_like(st_ref)

    u = u_ref[0]
    ub = u.astype(BF16)
    sre_ref[...] = jnp.dot(ub, bre_ref[...], preferred_element_type=F32)
    sim_ref[...] = jnp.dot(ub, bim_ref[...], preferred_element_type=F32)
    lam_re = lam_ref[0:1, :]
    lam_im = lam_ref[1:2, :]

    def scan_body(q, carry):
        cr, ci = carry
        rows = pl.ds(pl.multiple_of(q * SUBLANES, SUBLANES), SUBLANES)
        nr = lam_re * cr - lam_im * ci + sre_ref[rows, :]
        ni = lam_re * ci + lam_im * cr + sim_ref[rows, :]
        sre_ref[rows, :] = nr
        sim_ref[rows, :] = ni
        return nr, ni

    zero = jnp.zeros((SUBLANES, S5_FLAT), F32)
    end_re, end_im = lax.fori_loop(0, nq, scan_body, (zero, zero), unroll=S5_UNROLL)

    seg_re = lamseg_ref[0:1, :]
    seg_im = lamseg_ref[1:2, :]
    cr, ci = st_ref[0:1, :], st_ref[1:2, :]
    in_re, in_im = [], []
    for r in range(SUBLANES):
        in_re.append(cr)
        in_im.append(ci)
        er, ei = end_re[r:r + 1, :], end_im[r:r + 1, :]
        cr, ci = seg_re * cr - seg_im * ci + er, seg_re * ci + seg_im * cr + ei
    st_ref[0:1, :] = cr
    st_ref[1:2, :] = ci
    car_re = jnp.concatenate(in_re, axis=0)
    car_im = jnp.concatenate(in_im, axis=0)

    def fix_body(q, _):
        rows = pl.ds(pl.multiple_of(q * SUBLANES, SUBLANES), SUBLANES)
        pr = lamq_ref[0, pl.ds(q, 1), :]
        pi = lamq_ref[1, pl.ds(q, 1), :]
        sre_ref[rows, :] = sre_ref[rows, :] + (pr * car_re - pi * car_im)
        sim_ref[rows, :] = sim_ref[rows, :] + (pr * car_im + pi * car_re)
        return 0

    lax.fori_loop(0, nq, fix_body, 0, unroll=S5_UNROLL)

    y = (jnp.dot(sre_ref[...].astype(BF16), cre_ref[...], preferred_element_type=F32)
         - jnp.dot(sim_ref[...].astype(BF16), cim_ref[...], preferred_element_type=F32))
    y = y + d_ref[...] * u
    y = jax.nn.gelu(y)
    gate = jnp.dot(y.astype(BF16), gw_ref[...], preferred_element_type=F32) + gb_ref[...]
    o_ref[0] = y * _sigmoid(gate)


def s5_prepare(lam_re, lam_im, log_step, b_re, b_im, c_re, c_im, nq):
    dt = jnp.exp(log_step.astype(F32))[:, None]
    mag = jnp.exp(lam_re * dt)
    lb_re = mag * jnp.cos(lam_im * dt)
    lb_im = mag * jnp.sin(lam_im * dt)
    den = lam_re * lam_re + lam_im * lam_im
    n_re = lb_re - 1.0
    f_re = (n_re * lam_re + lb_im * lam_im) / den
    f_im = (lb_im * lam_re - n_re * lam_im) / den
    bb_re = f_re[..., None] * b_re - f_im[..., None] * b_im
    bb_im = f_re[..., None] * b_im + f_im[..., None] * b_re
    eye = jnp.eye(S5_GROUPS, dtype=F32)
    bd_in = lambda t: jnp.einsum('gph,gk->ghkp', t, eye).reshape(GROUP_WIDTH, S5_FLAT).astype(BF16)
    bd_out = lambda t: jnp.einsum('ghp,gk->gpkh', t, eye).reshape(S5_FLAT, GROUP_WIDTH).astype(BF16)
    lam = jnp.stack([lb_re.reshape(-1), lb_im.reshape(-1)])

    def power(n):
        n = n[:, None, None]
        m = jnp.exp(n * (lam_re * dt)[None])
        a = n * (lam_im * dt)[None]
        return jnp.stack([(m * jnp.cos(a)).reshape(-1, S5_FLAT), (m * jnp.sin(a)).reshape(-1, S5_FLAT)])

    lam_q = power(jnp.arange(1, nq + 1, dtype=F32))
    lam_seg = power(jnp.full((1,), float(nq), F32))[:, 0, :]
    return bd_in(bb_re), bd_in(bb_im), bd_out(c_re), bd_out(c_im), lam, lam_q, lam_seg


def s5_mixer(u, prep, d_skip, glu_w, glu_b, chunk=512):
    bsz, seq, w = u.shape
    nq = chunk // SUBLANES
    nchunk = seq // chunk
    bre, bim, cre, cim, lam, lam_q, lam_seg = prep
    up = u.reshape(bsz, nchunk, SUBLANES, nq, w).transpose(0, 1, 3, 2, 4).reshape(bsz, seq, w)
    const2 = lambda b, i: (0, 0)
    out = pl.pallas_call(
        functools.partial(_s5_kernel, nq=nq),
        grid=(bsz, nchunk),
        in_specs=[pl.BlockSpec((1, chunk, w), lambda b, i: (b, i, 0)),
                  pl.BlockSpec((w, S5_FLAT), const2),
                  pl.BlockSpec((w, S5_FLAT), const2),
                  pl.BlockSpec((S5_FLAT, w), const2),
                  pl.BlockSpec((S5_FLAT, w), const2),
                  pl.BlockSpec((2, S5_FLAT), const2),
                  pl.BlockSpec((2, nq, S5_FLAT), lambda b, i: (0, 0, 0)),
                  pl.BlockSpec((2, S5_FLAT), const2),
                  pl.BlockSpec((1, w), const2),
                  pl.BlockSpec((w, w), const2),
                  pl.BlockSpec((1, w), const2)],
        out_specs=pl.BlockSpec((1, chunk, w), lambda b, i: (b, i, 0)),
        out_shape=jax.ShapeDtypeStruct((bsz, seq, w), F32),
        scratch_shapes=[pltpu.VMEM((chunk, S5_FLAT), F32),
                        pltpu.VMEM((chunk, S5_FLAT), F32),
                        pltpu.VMEM((2, S5_FLAT), F32)],
        compiler_params=_params("parallel", "arbitrary"),
        name="s5_mixer",
    )(up, bre, bim, cre, cim, lam, lam_q, lam_seg,
      d_skip.reshape(1, w), glu_w.astype(BF16), glu_b.reshape(1, w))
    return out.reshape(bsz, nchunk, nq, SUBLANES, w).transpose(0, 1, 3, 2, 4).reshape(bsz, seq, w)


def _ret_kernel(h_ref, cos_ref, sin_ref, intra_ref, qw_ref, kw_ref, dec_ref, ones_ref,
                o_ref, st_ref, *, chunk):
    @pl.when(pl.program_id(1) == 0)
    def _():
        st_ref[...] = jnp.zeros_like(st_ref)

    h = h_ref[0].astype(F32)
    cos = cos_ref[0]
    sin = sin_ref[0]
    nqk = RET_HEADS * RET_QK
    q = h[:, 0:nqk] * cos + h[:, 2 * nqk:3 * nqk] * sin
    k = (h[:, nqk:2 * nqk] * cos + h[:, 3 * nqk:4 * nqk] * sin) * (RET_QK ** -0.5)
    v = h[:, 4 * nqk:4 * nqk + GROUP_WIDTH]
    g = h[:, 4 * nqk + GROUP_WIDTH:]
    kb = k.astype(BF16)
    lane_qk = lax.broadcasted_iota(jnp.int32, (chunk, nqk), 1) // RET_QK
    lane_v = lax.broadcasted_iota(jnp.int32, (chunk, GROUP_WIDTH), 1) // RET_V
    state = st_ref[...]
    o = _bdot(q * qw_ref[...], state)
    for hd in range(RET_HEADS):
        qh = jnp.where(lane_qk == hd, q, 0.0).astype(BF16)
        s = lax.dot_general(qh, kb, (((1,), (1,)), ((), ())), preferred_element_type=F32)
        s = s * intra_ref[hd]
        vh = jnp.where(lane_v == hd, v, 0.0).astype(BF16)
        o = o + jnp.dot(s.astype(BF16), vh, preferred_element_type=F32)
    kv = lax.dot_general((k * kw_ref[...]).astype(BF16), v.astype(BF16),
                         (((0,), (0,)), ((), ())), preferred_element_type=F32)
    dec = dec_ref[...]
    st_ref[...] = state * dec + jnp.where(dec > 0.0, kv, 0.0)
    ms = _split_dot(o * o, ones_ref[...])
    o = o * lax.rsqrt(ms + NORM_EPS)
    o_ref[0] = o * (g * _sigmoid(g))


def retention_tables(chunk):
    log_gamma = np.log1p(-np.exp2(-5.0 - np.arange(RET_HEADS, dtype=np.float64)))
    idx = np.arange(chunk, dtype=np.float64)
    diff = idx[:, None] - idx[None, :]
    intra = np.where(diff >= 0, np.exp(np.maximum(diff, 0.0)[None] * log_gamma[:, None, None]), 0.0)
    q_w = np.repeat(np.exp((idx + 1.0)[:, None] * log_gamma[None, :]), RET_QK, axis=1)
    k_w = np.repeat(np.exp((chunk - 1.0 - idx)[:, None] * log_gamma[None, :]), RET_QK, axis=1)
    head_q = np.arange(RET_HEADS * RET_QK) // RET_QK
    head_v = np.arange(GROUP_WIDTH) // RET_V
    same = head_q[:, None] == head_v[None, :]
    dec = np.where(same, np.exp(chunk * log_gamma)[head_q][:, None], 0.0)
    ones = (head_v[:, None] == head_v[None, :]).astype(np.float64) / RET_V
    f = lambda a: jnp.asarray(a, F32)
    return f(intra), f(q_w), f(k_w), f(dec), jnp.asarray(ones, BF16)


def retention_mixer(h, cos_t, sin_t, chunk=256):
    bsz, seq, wh = h.shape
    intra, q_w, k_w, dec, ones = retention_tables(chunk)
    nqk = RET_HEADS * RET_QK
    row = lambda b, i: (b, i, 0)
    c2 = lambda b, i: (0, 0)
    return pl.pallas_call(
        functools.partial(_ret_kernel, chunk=chunk),
        grid=(bsz, seq // chunk),
        in_specs=[pl.BlockSpec((1, chunk, wh), row),
                  pl.BlockSpec((1, chunk, LANES), row),
                  pl.BlockSpec((1, chunk, LANES), row),
                  pl.BlockSpec((RET_HEADS, chunk, chunk), lambda b, i: (0, 0, 0)),
                  pl.BlockSpec((chunk, nqk), c2),
                  pl.BlockSpec((chunk, nqk), c2),
                  pl.BlockSpec((nqk, GROUP_WIDTH), c2),
                  pl.BlockSpec((GROUP_WIDTH, GROUP_WIDTH), c2)],
        out_specs=pl.BlockSpec((1, chunk, GROUP_WIDTH), row),
        out_shape=jax.ShapeDtypeStruct((bsz, seq, GROUP_WIDTH), F32),
        scratch_shapes=[pltpu.VMEM((nqk, GROUP_WIDTH), F32)],
        compiler_params=_params("parallel", "arbitrary"),
        name="retention",
    )(h, cos_t, sin_t, intra, q_w, k_w, dec, ones)


def _mla_prep_kernel(h_ref, cos_ref, sin_ref, qg_ref, kvg_ref, wqa_ref, wqb_ref, wk_ref, wv_ref,
                     q_ref, k_ref, v_ref, *, scale):
    h = h_ref[0].astype(F32)
    tm = h.shape[0]
    lane = lax.broadcasted_iota(jnp.int32, (tm, LANES), 1)
    is_nope = lane < MLA_NOPE
    is_rope = jnp.logical_and(lane >= MLA_NOPE, lane < MLA_NOPE + MLA_ROPE)
    cm = jnp.where(is_nope, 1.0, jnp.where(is_rope, cos_ref[0], 0.0))
    sm = jnp.where(is_rope, sin_ref[0], 0.0)

    qc = h[:, 0:MLA_Q_RANK]
    qn = (qc * lax.rsqrt(jnp.mean(qc * qc, axis=-1, keepdims=True) + NORM_EPS) * qg_ref[...]).astype(BF16)
    kvc = h[:, MLA_Q_RANK:MLA_Q_RANK + MLA_KV_RANK]
    kvn = (kvc * lax.rsqrt(jnp.mean(kvc * kvc, axis=-1, keepdims=True) + NORM_EPS) * kvg_ref[...]).astype(BF16)
    off = MLA_Q_RANK + MLA_KV_RANK
    kpe = h[:, off:off + LANES] * cm + h[:, off + LANES:off + 2 * LANES] * sm

    qa = jnp.dot(qn, wqa_ref[...], preferred_element_type=F32)
    qb = jnp.dot(qn, wqb_ref[...], preferred_element_type=F32)
    kn = jnp.dot(kvn, wk_ref[...], preferred_element_type=F32)
    v_ref[0] = jnp.dot(kvn, wv_ref[...], preferred_element_type=F32).astype(BF16)
    for hd in range(MLA_HEADS):
        sl = slice(hd * LANES, (hd + 1) * LANES)
        q_ref[0, hd] = ((qa[:, sl] * cm + qb[:, sl] * sm) * scale).astype(BF16)
        k_ref[0, hd] = (kn[:, sl] + kpe).astype(BF16)


def mla_pack_weights(w_q_up, w_kv_up):
    dq = MLA_NOPE + MLA_ROPE
    zq = w_q_up.shape[1]
    zk = w_kv_up.shape[1]
    z = lambda n, zero: np.full((n,), zero)
    ia, ib, ik, iv = [], [], [], []
    for hd in range(MLA_HEADS):
        nope = np.arange(hd * dq, hd * dq + MLA_NOPE)
        pe = np.arange(hd * dq + MLA_NOPE, (hd + 1) * dq)
        pad = LANES - dq
        ia += [nope, pe, z(pad, zq)]
        ib += [z(MLA_NOPE, zq), _swap_halves(pe, MLA_ROPE), z(pad, zq)]
        kv0 = hd * (MLA_NOPE + MLA_V)
        ik += [np.arange(kv0, kv0 + MLA_NOPE), z(LANES - MLA_NOPE, zk)]
        iv += [np.arange(kv0 + MLA_NOPE, kv0 + MLA_NOPE + MLA_V)]
    ext = lambda w: jnp.concatenate([w, jnp.zeros((w.shape[0], 1), w.dtype)], axis=1)
    take = lambda w, idx: jnp.take(ext(w), jnp.asarray(np.concatenate(idx), jnp.int32), axis=1).astype(BF16)
    return take(w_q_up, ia), take(w_q_up, ib), take(w_kv_up, ik), take(w_kv_up, iv)


def mla_prep(h, cos_t, sin_t, q_norm_g, kv_norm_g, packed, tm=512):
    bsz, seq, wh = h.shape
    wqa, wqb, wk, wv = packed
    row = lambda b, i: (b, i, 0)
    c2 = lambda b, i: (0, 0)
    hrow = lambda b, i: (b, 0, i, 0)
    scale = (MLA_NOPE + MLA_ROPE) ** -0.5
    return pl.pallas_call(
        functools.partial(_mla_prep_kernel, scale=scale),
        grid=(bsz, seq // tm),
        in_specs=[pl.BlockSpec((1, tm, wh), row),
                  pl.BlockSpec((1, tm, LANES), row),
                  pl.BlockSpec((1, tm, LANES), row),
                  pl.BlockSpec((1, MLA_Q_RANK), c2),
                  pl.BlockSpec((1, MLA_KV_RANK), c2),
                  pl.BlockSpec(wqa.shape, c2),
                  pl.BlockSpec(wqb.shape, c2),
                  pl.BlockSpec(wk.shape, c2),
                  pl.BlockSpec(wv.shape, c2)],
        out_specs=[pl.BlockSpec((1, MLA_HEADS, tm, LANES), hrow),
                   pl.BlockSpec((1, MLA_HEADS, tm, LANES), hrow),
                   pl.BlockSpec((1, tm, GROUP_WIDTH), row)],
        out_shape=[jax.ShapeDtypeStruct((bsz, MLA_HEADS, seq, LANES), BF16),
                   jax.ShapeDtypeStruct((bsz, MLA_HEADS, seq, LANES), BF16),
                   jax.ShapeDtypeStruct((bsz, seq, GROUP_WIDTH), BF16)],
        compiler_params=_params("parallel", "parallel"),
        name="mla_prep",
    )(h, cos_t, sin_t, q_norm_g.reshape(1, -1), kv_norm_g.reshape(1, -1), wqa, wqb, wk, wv)


def _attn_kernel(q_ref, k_ref, v_ref, o_ref, m_ref, l_ref, acc_ref, *, blk):
    qi = pl.program_id(2)
    heads = range(2)

    def step(q_rows, key0, size, masked):
        keys = pl.ds(pl.multiple_of(key0, size), size)
        vs = v_ref[0, keys, :]
        s = [lax.dot_general(q_ref[0, hh, q_rows, :], k_ref[0, hh, keys, :], (((1,), (1,)), ((), ())),
                             preferred_element_type=F32) for hh in heads]
        if masked:
            r = lax.broadcasted_iota(jnp.int32, (size, size), 0)
            c = lax.broadcasted_iota(jnp.int32, (size, size), 1)
            s = [jnp.where(c <= r, t, -jnp.inf) for t in s]
        m_prev = [m_ref[hh, q_rows, :] for hh in heads]
        m_new = [jnp.maximum(m_prev[hh], jnp.max(s[hh], axis=-1, keepdims=True)) for hh in heads]
        p = [jnp.exp(s[hh] - jnp.concatenate([m_new[hh]] * (size // LANES), axis=1)) for hh in heads]
        alpha = [jnp.exp(m_prev[hh] - m_new[hh]) for hh in heads]
        for hh in heads:
            l_ref[hh, q_rows, :] = alpha[hh] * l_ref[hh, q_rows, :] + jnp.sum(p[hh], axis=-1, keepdims=True)
            acc_ref[hh, q_rows, :] = (alpha[hh] * acc_ref[hh, q_rows, :]
                                      + jnp.dot(p[hh].astype(BF16), vs, preferred_element_type=F32))
            m_ref[hh, q_rows, :] = m_new[hh]

    for hh in heads:
        m_ref[hh] = jnp.full((blk, LANES), -jnp.inf, F32)
        l_ref[hh] = jnp.zeros((blk, LANES), F32)
        acc_ref[hh] = jnp.zeros((blk, LANES), F32)

    def body(j, _):
        step(slice(0, blk), j * blk, blk, False)
        return 0

    lax.fori_loop(0, qi, body, 0)
    half = blk // 2
    step(slice(0, half), qi * blk, half, True)
    step(slice(half, blk), qi * blk, half, False)
    step(slice(half, blk), qi * blk + half, half, True)

    lane = lax.broadcasted_iota(jnp.int32, (blk, LANES), 1)
    o0 = acc_ref[0] / l_ref[0]
    o1 = acc_ref[1] / l_ref[1]
    o_ref[0] = jnp.where(lane < MLA_V, o0, o1)


def causal_attention(q, k, v, blk=1024):
    bsz, nh, seq, dk = q.shape
    return pl.pallas_call(
        functools.partial(_attn_kernel, blk=blk),
        grid=(bsz, nh // 2, seq // blk),
        in_specs=[pl.BlockSpec((1, 2, blk, dk), lambda b, p, i: (b, p, i, 0)),
                  pl.BlockSpec((1, 2, seq, dk), lambda b, p, i: (b, p, 0, 0)),
                  pl.BlockSpec((1, seq, LANES), lambda b, p, i: (b, 0, p))],
        out_specs=pl.BlockSpec((1, blk, LANES), lambda b, p, i: (b, i, p)),
        out_shape=jax.ShapeDtypeStruct((bsz, seq, nh // 2 * LANES), F32),
        scratch_shapes=[pltpu.VMEM((2, blk, LANES), F32)] * 3,
        compiler_params=_params("parallel", "parallel", "arbitrary"),
        name="mla_attention",
    )(q, k, v)


def _rw_prep_kernel(*refs, first):
    if first:
        (h_ref, prev_ref, mu_ref, w0_ref, w2_ref, a0_ref, a2_ref, g2_ref, kk_ref, ka_ref, rk_ref, ones_ref, tri_ref,
         q_ref, y0_ref, phi_ref, dlt_ref, v_ref, bonus_ref, g_ref) = refs
    else:
        (h_ref, prev_ref, mu_ref, w0_ref, w2_ref, a0_ref, a2_ref, g2_ref, kk_ref, ka_ref, rk_ref, ones_ref, tri_ref,
         vf_ref, v0_ref, v1_ref, v2_ref,
         q_ref, y0_ref, phi_ref, dlt_ref, v_ref, bonus_ref, g_ref) = refs
    h = h_ref[0]
    tm = h.shape[0]
    last = prev_ref[0, SUBLANES - 1:SUBLANES, :]
    last = jnp.where(pl.program_id(1) == 0, 0.0, last)
    row = lax.broadcasted_iota(jnp.int32, h.shape, 0)
    h_prev = jnp.where(row == 0, last, pltpu.roll(h, 1, axis=0))
    h = h + (h_prev - h) * mu_ref[...]
    w = GROUP_WIDTH
    r = h[:, 0:w]
    k = h[:, w:2 * w]
    v = h[:, 2 * w:3 * w]
    wa = h[:, 3 * w:3 * w + LANES]
    gd = h[:, 3 * w + LANES:]
    z = w0_ref[...] + _bdot(jnp.tanh(wa), w2_ref[...])
    nz = -z
    softplus = jnp.maximum(nz, 0.0) + jnp.log(1.0 + jnp.exp(-jnp.abs(nz)))
    log_decay = -jnp.exp(-softplus - 0.5)
    a = _sigmoid(a0_ref[...] + _bdot(wa, a2_ref[...]))
    g_ref[0] = _bdot(_sigmoid(gd), g2_ref[...])
    if not first:
        mix = _sigmoid(v0_ref[...] + _bdot(_bdot(v, v1_ref[...]), v2_ref[...]))
        v = v + (vf_ref[0] - v) * mix
    kk = k * kk_ref[...]
    norm = jnp.sqrt(_split_dot(kk * kk, ones_ref[...]))
    kk = kk / jnp.maximum(norm, 1e-12)
    k = k * (1.0 + (a - 1.0) * ka_ref[...])
    v_ref[0] = v
    bonus_ref[0] = _split_dot(r * k * rk_ref[...], ones_ref[...]) * v
    _wkv_chunk_terms(r, log_decay, k, kk, kk * a, v, tri_ref[...], q_ref.at[0], y0_ref.at[0], phi_ref.at[0], dlt_ref.at[0],
                     tm // WKV_CHUNK)


def rw_pack_weights(w2, a2, v1=None, v2=None):
    zeros = lambda n, m: jnp.zeros((n, m), F32)
    half = LANES // 2
    w2p = jnp.concatenate([w2, zeros(half, GROUP_WIDTH)], axis=0).astype(BF16)
    a2p = jnp.concatenate([zeros(half, GROUP_WIDTH), a2], axis=0).astype(BF16)
    if v1 is None:
        return w2p, a2p
    v1p = jnp.concatenate([v1, zeros(GROUP_WIDTH, LANES - v1.shape[1])], axis=1).astype(BF16)
    v2p = jnp.concatenate([v2, zeros(LANES - v2.shape[0], GROUP_WIDTH)], axis=0).astype(BF16)
    return w2p, a2p, v1p, v2p


def head_ones(width, head, scale):
    hd = np.arange(width) // head
    return jnp.asarray((hd[:, None] == hd[None, :]).astype(np.float32) * scale, BF16)


def rw_prep(h, mu, w0, w2p, a0, a2p, g2, k_k, k_a, r_k, vmix=None, tm=512):
    bsz, seq, wh = h.shape
    w = GROUP_WIDTH
    first = vmix is None
    row = lambda b, i: (b, i, 0)
    c2 = lambda b, i: (0, 0)
    vec = lambda t: t.reshape(1, -1)
    prev_map = lambda b, i: (b, jnp.maximum(i * (tm // SUBLANES) - 1, 0), 0)
    c = WKV_CHUNK
    npair = w // LANES
    tri = jnp.asarray(np.tril(np.ones((c, c), np.float32)), BF16)
    args = [h, h, vec(mu), vec(w0), w2p, vec(a0), a2p, g2.astype(BF16), vec(k_k), vec(k_a), vec(r_k),
            head_ones(w, RW_HEAD, 1.0), tri]
    specs = [pl.BlockSpec((1, tm, wh), row), pl.BlockSpec((1, SUBLANES, wh), prev_map),
             pl.BlockSpec((1, wh), c2), pl.BlockSpec((1, w), c2), pl.BlockSpec((LANES, w), c2),
             pl.BlockSpec((1, w), c2), pl.BlockSpec((LANES, w), c2), pl.BlockSpec((LANES, w), c2),
             pl.BlockSpec((1, w), c2), pl.BlockSpec((1, w), c2), pl.BlockSpec((1, w), c2),
             pl.BlockSpec((w, w), c2), pl.BlockSpec((c, c), c2)]
    if not first:
        v_first, v0, v1p, v2p = vmix
        args += [v_first, vec(v0), v1p, v2p]
        specs += [pl.BlockSpec((1, tm, w), row), pl.BlockSpec((1, w), c2),
                  pl.BlockSpec((w, LANES), c2), pl.BlockSpec((LANES, w), c2)]
    act = pl.BlockSpec((1, tm, w), row)
    mat = pl.BlockSpec((1, tm // c, npair, LANES, LANES), lambda b, i: (b, i, 0, 0, 0))
    ashape = jax.ShapeDtypeStruct((bsz, seq, w), F32)
    mshape = jax.ShapeDtypeStruct((bsz, seq // c, npair, LANES, LANES), F32)
    return pl.pallas_call(
        functools.partial(_rw_prep_kernel, first=first),
        grid=(bsz, seq // tm),
        in_specs=specs,
        out_specs=[act, act, mat, mat, act, act, act],
        out_shape=[ashape, ashape, mshape, mshape, ashape, ashape, ashape],
        compiler_params=_params("parallel", "parallel"),
        name="rwkv_prep",
    )(*args)


WKV_CHUNK = 64
_NN = (((1,), (0,)), ((), ()))
_NT = (((1,), (1,)), ((), ()))
_TN = (((0,), (0,)), ((), ()))


def _split(a):
    hi = a.astype(BF16)
    return hi, (a - hi.astype(F32)).astype(BF16)


def _mm(a, b, dims=_NN, passes=1):
    if passes == 1:
        return lax.dot_general(a.astype(BF16), b.astype(BF16), dims, preferred_element_type=F32)
    ah, al = _split(a)
    bh, bl = _split(b)
    d = lambda x, y: lax.dot_general(x, y, dims, preferred_element_type=F32)
    return d(ah, bh) + (d(ah, bl) + d(al, bh))


def _wkv_chunk_terms(r_all, lw_all, k_all, kk_all, b_all, v_all, tri, q_ref, y0_ref, phi_ref, dlt_ref, nchunk):
    c = WKV_CHUNK
    n2 = 2 * c
    lane = lax.broadcasted_iota(jnp.int32, (c, LANES), 1)
    head0 = lane < RW_HEAD
    row = lax.broadcasted_iota(jnp.int32, (n2, n2), 0)
    col = lax.broadcasted_iota(jnp.int32, (n2, n2), 1)
    strict = col < row
    incl = col <= row
    eye = (row == col).astype(F32)
    stack = lambda x: jnp.concatenate([jnp.where(head0, x, 0.0), jnp.where(head0, 0.0, x)], axis=0)
    fold = lambda x: x[:c] + x[c:]
    chains = [(ci, p) for ci in range(nchunk) for p in range(GROUP_WIDTH // LANES)]
    st = []
    for ci, p in chains:
        sl = slice(p * LANES, (p + 1) * LANES)
        rows = slice(ci * c, (ci + 1) * c)
        lw = lw_all[rows, sl]
        lw_hi, lw_lo = _split(lw)
        cum = (jnp.dot(tri, lw_hi, preferred_element_type=F32)
               + jnp.dot(tri, lw_lo, preferred_element_type=F32))
        last = cum[c - 1:c, :]
        e_in = jnp.exp(cum)
        e_ex = jnp.exp(cum - lw)
        e_neg = jnp.exp(-cum)
        e_end = jnp.exp(last - cum)
        kk, r, k, b, v = kk_all[rows, sl], r_all[rows, sl], k_all[rows, sl], b_all[rows, sl], v_all[rows, sl]
        kt2 = stack(kk * e_ex)
        rt2 = stack(r * e_in)
        lhs = jnp.concatenate([kt2, rt2], axis=0)
        gb = _mm(lhs, stack(b * e_neg), _NT)
        gk = _mm(lhs, stack(k * e_neg), _NT)
        x = -jnp.where(strict, gb[:n2], 0.0)
        st.append(dict(sl=sl, rows=rows, kt2=kt2, rt2=rt2, kp2=stack(k * e_end), bp2=stack(b * e_end), v2=stack(v),
                       g2=jnp.where(strict, gk[:n2], 0.0), rb2=jnp.where(incl, gb[n2:], 0.0),
                       rk2=jnp.where(incl, gk[n2:], 0.0), last=last, x=x, t=eye + x))
    for _ in range(int(np.log2(c)) - 1):
        for d in st:
            d["x"] = _mm(d["x"], d["x"])
        for d in st:
            d["t"] = d["t"] + _mm(d["t"], d["x"])
    for d in st:
        d["gv2"] = _mm(d["g2"], d["v2"])
    for d in st:
        d["tku"] = _mm(d["t"], jnp.concatenate([d["kt2"], d["gv2"]], axis=1))
    for d in st:
        d["rbz"] = _mm(d["rb2"], d["tku"])
    for (ci, p), d in zip(chains, st):
        sl, rows, tku, rbz = d["sl"], d["rows"], d["tku"], d["rbz"]
        q_ref[rows, sl] = fold(d["rt2"] - rbz[:, :LANES])
        y0_ref[rows, sl] = fold(_mm(d["rk2"], d["v2"]) - rbz[:, LANES:])
        decay = jnp.where(row == col, jnp.broadcast_to(jnp.exp(d["last"]), (n2, LANES)), 0.0)
        phi_ref[ci, p] = decay - _mm(d["bp2"], tku[:, :LANES], _TN)
        dlt_ref[ci, p] = _mm(d["kp2"], d["v2"], _TN) - _mm(d["bp2"], tku[:, LANES:], _TN)


def _wkv_seq_kernel(q_ref, y0_ref, phi_ref, dlt_ref, y_ref, st_ref, *, nchunk):
    @pl.when(pl.program_id(1) == 0)
    def _():
        st_ref[...] = jnp.zeros_like(st_ref)

    c = WKV_CHUNK
    npair = GROUP_WIDTH // LANES
    m = [st_ref[p] for p in range(npair)]
    for i in range(nchunk):
        rows = slice(i * c, (i + 1) * c)
        for p in range(npair):
            sl = slice(p * LANES, (p + 1) * LANES)
            y_ref[0, rows, sl] = _mm(q_ref[0, rows, sl], m[p], _NN, 3) + y0_ref[0, rows, sl]
        m = [_mm(phi_ref[0, i, p], m[p], _NN, 3) + dlt_ref[0, i, p] for p in range(npair)]
    for p in range(npair):
        st_ref[p] = m[p]


def wkv_sequential(q, y0, phi, dlt, block=512):
    bsz, seq, w = q.shape
    c = WKV_CHUNK
    nchunk = block // c
    npair = w // LANES
    row = lambda bb, i: (bb, i, 0)
    blk = pl.BlockSpec((1, block, w), row)
    mat = pl.BlockSpec((1, nchunk, npair, LANES, LANES), lambda bb, i: (bb, i, 0, 0, 0))
    return pl.pallas_call(
        functools.partial(_wkv_seq_kernel, nchunk=nchunk),
        grid=(bsz, seq // block),
        in_specs=[blk, blk, mat, mat],
        out_specs=blk,
        out_shape=jax.ShapeDtypeStruct((bsz, seq, w), F32),
        scratch_shapes=[pltpu.VMEM((npair, LANES, LANES), F32)],
        compiler_params=_params("parallel", "arbitrary"),
        name="wkv7_chunk_scan",
    )(q, y0, phi, dlt)


def _outproj_kernel(x_ref, ys5_ref, yrw_ref, bonus_ref, grw_ref, ymla_ref, yret_ref, lng_ref, lnb_ref, ones_ref,
                    wout_ref, g1_ref, ng_ref, sc_ref, sh_ref, rhi_ref, rlo_ref, rb_ref,
                    xo_ref, hn_ref, idx_ref, tw_ref, cnt_ref):
    y = yrw_ref[0]
    mean = _split_dot(y, ones_ref[...])
    yc = y - mean
    var = _split_dot(yc * yc, ones_ref[...])
    yrw = (yc * lax.rsqrt(var + RW_GN_EPS) * lng_ref[...] + lnb_ref[...] + bonus_ref[0]) * grw_ref[0]
    w = GROUP_WIDTH
    mixed = (_bdot(ys5_ref[0], wout_ref[0:w, :]) + _bdot(yrw, wout_ref[w:2 * w, :])
             + _bdot(ymla_ref[0], wout_ref[2 * w:3 * w, :]) + _bdot(yret_ref[0], wout_ref[3 * w:4 * w, :]))
    x = x_ref[0] + g1_ref[0] * mixed
    xo_ref[0] = x
    ms = jnp.mean(x * x, axis=-1, keepdims=True)
    hn = x * lax.rsqrt(ms + NORM_EPS) * ng_ref[...]
    hn = hn * (1.0 + sc_ref[0]) + sh_ref[0]
    hi = hn.astype(BF16)
    hn_ref[0] = hi
    lo = (hn - hi.astype(F32)).astype(BF16)
    logits = (jnp.dot(hi, rhi_ref[...], preferred_element_type=F32)
              + jnp.dot(lo, rhi_ref[...], preferred_element_type=F32)
              + jnp.dot(hi, rlo_ref[...], preferred_element_type=F32) + rb_ref[...])
    lane = lax.broadcasted_iota(jnp.int32, logits.shape, 1)
    lane_f = lane.astype(F32)
    cur = jnp.where(lane < N_EXPERTS, logits, -jnp.inf)
    idx_out = jnp.zeros(logits.shape, F32)
    val_out = jnp.zeros(logits.shape, F32)
    picked = jnp.zeros(logits.shape, F32)
    top = None
    denom = None
    for j in range(TOP_K):
        m = jnp.max(cur, axis=-1, keepdims=True)
        sel = jnp.min(jnp.where(cur == m, lane_f, float(LANES)), axis=-1, keepdims=True)
        hit = lane_f == sel
        cur = jnp.where(hit, -jnp.inf, cur)
        picked = picked + jnp.where(hit, 1.0, 0.0)
        top = m if top is None else top
        e = jnp.exp(m - top)
        denom = e if denom is None else denom + e
        idx_out = jnp.where(lane == j, sel, idx_out)
        val_out = jnp.where(lane == j, e, val_out)
    idx_ref[0] = idx_out.astype(jnp.int32)
    tw_ref[0] = val_out / denom
    cnt_ref[0, 0] = jnp.broadcast_to(jnp.sum(picked, axis=0, keepdims=True), (SUBLANES, LANES))


def out_proj(x, y_s5, y_rw, bonus, g_rw, y_mla, y_ret, ln_g, ln_b, w_out, g1, norm_g, sc2, sh2,
             router_w, router_b, tm=512):
    bsz, seq, d = x.shape
    w = GROUP_WIDTH
    row = lambda b, i: (b, i, 0)
    per_b = lambda b, i: (b, 0, 0)
    c2 = lambda b, i: (0, 0)
    vec = lambda t: t.reshape(1, -1)
    pad = LANES - router_w.shape[1]
    rw_pad = jnp.concatenate([router_w, jnp.zeros((d, pad), F32)], axis=1)
    r_hi = rw_pad.astype(BF16)
    r_lo = (rw_pad - r_hi.astype(F32)).astype(BF16)
    rb = jnp.concatenate([router_b, jnp.zeros((pad,), F32)]).reshape(1, LANES)
    mixer = pl.BlockSpec((1, tm, w), row)
    return pl.pallas_call(
        _outproj_kernel,
        grid=(bsz, seq // tm),
        in_specs=[pl.BlockSpec((1, tm, d), row), mixer, mixer, mixer, mixer, mixer, mixer,
                  pl.BlockSpec((1, w), c2), pl.BlockSpec((1, w), c2), pl.BlockSpec((w, w), c2),
                  pl.BlockSpec((4 * w, d), c2),
                  pl.BlockSpec((1, 1, d), per_b), pl.BlockSpec((1, d), c2),
                  pl.BlockSpec((1, 1, d), per_b), pl.BlockSpec((1, 1, d), per_b),
                  pl.BlockSpec((d, LANES), c2), pl.BlockSpec((d, LANES), c2), pl.BlockSpec((1, LANES), c2)],
        out_specs=[pl.BlockSpec((1, tm, d), row), pl.BlockSpec((1, tm, d), row), pl.BlockSpec((1, tm, LANES), row),
                   pl.BlockSpec((1, tm, LANES), row), pl.BlockSpec((1, 1, SUBLANES, LANES), lambda b, i: (b, i, 0, 0))],
        out_shape=[jax.ShapeDtypeStruct((bsz, seq, d), F32), jax.ShapeDtypeStruct((bsz, seq, d), BF16),
                   jax.ShapeDtypeStruct((bsz, seq, LANES), jnp.int32), jax.ShapeDtypeStruct((bsz, seq, LANES), F32),
                   jax.ShapeDtypeStruct((bsz, seq // tm, SUBLANES, LANES), F32)],
        compiler_params=_params("parallel", "parallel"),
        name="out_proj",
    )(x, y_s5, y_rw, bonus, g_rw, y_mla, y_ret, vec(ln_g), vec(ln_b), head_ones(w, RW_HEAD, 1.0 / RW_HEAD),
      w_out.astype(BF16), g1, vec(norm_g), sc2, sh2, r_hi, r_lo, rb)


MOE_ROWS = 512


def _moe_kernel(blk_e_ref, blk_on_ref, x_ref, wg_ref, bg_ref, wu_ref, bu_ref, wd_ref, bd_ref, o_ref, wb_ref):
    i = pl.program_id(0)
    changed = jnp.logical_or(i == 0, blk_e_ref[i] != blk_e_ref[jnp.maximum(i - 1, 0)])

    @pl.when(changed)
    def _():
        wb_ref[0] = wg_ref[0, 0].astype(BF16)
        wb_ref[1] = wu_ref[0, 0].astype(BF16)
        wb_ref[2] = wd_ref[0, 0].astype(BF16)

    @pl.when(blk_on_ref[i] > 0)
    def _():
        x = x_ref[...]
        gt = jnp.minimum(jnp.dot(x, wb_ref[0], preferred_element_type=F32) + bg_ref[0, 0], SWIGLU_LIMIT)
        up = jnp.clip(jnp.dot(x, wb_ref[1], preferred_element_type=F32) + bu_ref[0, 0], -SWIGLU_LIMIT, SWIGLU_LIMIT)
        act = gt * _sigmoid(SWIGLU_ALPHA * gt) * (up + 1.0)
        o_ref[...] = (jnp.dot(act.astype(BF16), wb_ref[2], preferred_element_type=F32) + bd_ref[0, 0]).astype(BF16)


def moe_experts(xb, blk_e, blk_on, layer, w_gate, b_gate, w_up, b_up, w_down, b_down):
    p_rows, d = xb.shape
    depth, n_e, _, de = w_gate.shape
    wmap = lambda i, e, on: (layer, e[i], 0, 0)
    rows = lambda i, e, on: (i, 0)
    return pl.pallas_call(
        _moe_kernel,
        grid_spec=pltpu.PrefetchScalarGridSpec(
            num_scalar_prefetch=2,
            grid=(p_rows // MOE_ROWS,),
            in_specs=[pl.BlockSpec((MOE_ROWS, d), rows),
                      pl.BlockSpec((1, 1, d, de), wmap), pl.BlockSpec((1, 1, 1, de), wmap),
                      pl.BlockSpec((1, 1, d, de), wmap), pl.BlockSpec((1, 1, 1, de), wmap),
                      pl.BlockSpec((1, 1, de, d), wmap), pl.BlockSpec((1, 1, 1, d), wmap)],
            out_specs=pl.BlockSpec((MOE_ROWS, d), rows),
            scratch_shapes=[pltpu.VMEM((3, d, de), BF16)]),
        out_shape=jax.ShapeDtypeStruct((p_rows, d), BF16),
        compiler_params=_params("arbitrary"),
        name="moe_experts",
    )(blk_e, blk_on, xb, w_gate, b_gate.reshape(depth, n_e, 1, de), w_up, b_up.reshape(depth, n_e, 1, de),
      w_down, b_down.reshape(depth, n_e, 1, d))


def moe_route(top_idx, counts):
    t = top_idx.shape[0]
    n_assign = t * TOP_K
    flat_e = top_idx.reshape(-1)
    iota = jnp.arange(n_assign, dtype=jnp.int32)
    _, order = lax.sort((flat_e, iota), num_keys=1)
    _, rank = lax.sort((order, iota), num_keys=1)
    start = jnp.cumsum(counts) - counts
    padded = (counts + MOE_ROWS - 1) // MOE_ROWS * MOE_ROWS
    pad_end = jnp.cumsum(padded)
    pad_start = pad_end - padded
    pos = rank + (pad_start - start)[flat_e]
    p_rows = n_assign + N_EXPERTS * MOE_ROWS
    n_blocks = p_rows // MOE_ROWS
    blk_first = jnp.arange(n_blocks, dtype=jnp.int32) * MOE_ROWS
    blk_e = jnp.minimum(jnp.sum(pad_end[None, :] <= blk_first[:, None], axis=1, dtype=jnp.int32), N_EXPERTS - 1)
    blk_within = blk_first - pad_start[blk_e]
    blk_left = counts[blk_e] - blk_within
    blk_on = (blk_left > 0).astype(jnp.int32)
    r = jnp.arange(MOE_ROWS, dtype=jnp.int32)[None, :]
    src = jnp.clip((start[blk_e] + blk_within)[:, None] + r, 0, n_assign - 1)
    row = blk_first[:, None] + r
    buf_tok = jnp.where(r < blk_left[:, None], order[src.reshape(-1)].reshape(n_blocks, MOE_ROWS) // TOP_K, row % t)
    return buf_tok.reshape(-1), pos.reshape(t, TOP_K), blk_e, blk_on


def moe_ffn(hn_bf16, top_idx, counts, layer, w_gate, b_gate, w_up, b_up, w_down, b_down):
    bsz, seq, d = hn_bf16.shape
    t = bsz * seq
    buf_tok, pos, blk_e, blk_on = moe_route(top_idx.reshape(t, LANES)[:, :TOP_K], counts)
    xb = hn_bf16.reshape(t, d).at[buf_tok].get(mode='promise_in_bounds')
    yb = moe_experts(xb, blk_e, blk_on, layer, w_gate, b_gate, w_up, b_up, w_down, b_down)
    yg = yb.at[pos.T.reshape(-1)].get(mode='promise_in_bounds')
    return yg.reshape(TOP_K, bsz, seq, d)


def _final_kernel(x_ref, y_ref, w_ref, g_ref, ng_ref, o_ref):
    x = x_ref[0] + g_ref[0] * _routed_sum(y_ref, w_ref[0])
    o_ref[0] = x * lax.rsqrt(jnp.mean(x * x, axis=-1, keepdims=True) + NORM_EPS) * ng_ref[...]


def final_combine_norm(x, yg, top_w, gate, final_g, tm=512):
    bsz, seq, d = x.shape
    row = lambda b, i: (b, i, 0)
    return pl.pallas_call(
        _final_kernel,
        grid=(bsz, seq // tm),
        in_specs=[pl.BlockSpec((1, tm, d), row),
                  pl.BlockSpec((TOP_K, 1, tm, d), lambda b, i: (0, b, i, 0)),
                  pl.BlockSpec((1, tm, LANES), row),
                  pl.BlockSpec((1, 1, d), lambda b, i: (b, 0, 0)),
                  pl.BlockSpec((1, d), lambda b, i: (0, 0))],
        out_specs=pl.BlockSpec((1, tm, d), row),
        out_shape=jax.ShapeDtypeStruct((bsz, seq, d), F32),
        compiler_params=_params("parallel", "parallel"),
        name="final_combine_norm",
    )(x, yg, top_w, gate, final_g.reshape(1, d))


S5_CHUNK = 512


def kernel(x, c, positions, ada_w, ada_b, norm_mix_g, norm_ffn_g, w_in, w_out,
           s5_lambda_re, s5_lambda_im, s5_log_step, s5_b_re, s5_b_im, s5_c_re, s5_c_im,
           s5_d, s5_glu_w, s5_glu_b,
           rw_mu, rw_w0, rw_w2, rw_a0, rw_a2, rw_g2, rw_k_k, rw_k_a, rw_r_k, rw_ln_g, rw_ln_b,
           rw_v0, rw_v1, rw_v2,
           mla_q_norm_g, mla_kv_norm_g, mla_w_q_up, mla_w_kv_up,
           router_w, router_b, ex_w_gate, ex_b_gate, ex_w_up, ex_b_up, ex_w_down, ex_b_down,
           final_norm_g):
    depth = w_in.shape[0]
    mod = adaln_mod(c, ada_w, ada_b)
    cos_t, sin_t = rope_tables(positions)
    v_first = None
    ffn = None
    for l in range(depth):
        sh1, sc1, g1, sh2, sc2, g2 = [m[:, None, :] for m in jnp.split(mod[l], N_MOD, axis=-1)]
        proj = in_proj(x, sc1, sh1, norm_mix_g[l], pack_w_in(w_in[l]), ffn)
        if ffn is not None:
            x = proj[0]
        s5_u, rw_in, mla_in, ret_in = proj[-4:]

        prep = s5_prepare(s5_lambda_re[l], s5_lambda_im[l], s5_log_step[l], s5_b_re[l], s5_b_im[l],
                          s5_c_re[l], s5_c_im[l], S5_CHUNK // SUBLANES)
        y_s5 = s5_mixer(s5_u, prep, s5_d[l], s5_glu_w[l], s5_glu_b[l], chunk=S5_CHUNK)

        if l == 0:
            w2p, a2p = rw_pack_weights(rw_w2[l], rw_a2[l])
            vmix = None
        else:
            w2p, a2p, v1p, v2p = rw_pack_weights(rw_w2[l], rw_a2[l], rw_v1[l - 1], rw_v2[l - 1])
            vmix = (v_first, rw_v0[l - 1], v1p, v2p)
        q_rw, y0_rw, phi_rw, dlt_rw, v_rw, bonus, g_rw = rw_prep(
            rw_in, rw_mu[l], rw_w0[l], w2p, rw_a0[l], a2p, rw_g2[l], rw_k_k[l], rw_k_a[l],
            rw_r_k[l].reshape(-1), vmix)
        if l == 0:
            v_first = v_rw
        y_rw = wkv_sequential(q_rw, y0_rw, phi_rw, dlt_rw)

        q, k, v = mla_prep(mla_in, cos_t, sin_t, mla_q_norm_g[l], mla_kv_norm_g[l],
                           mla_pack_weights(mla_w_q_up[l], mla_w_kv_up[l]))
        y_mla = causal_attention(q, k, v)

        y_ret = retention_mixer(ret_in, cos_t, sin_t)

        x, hn, top_idx, top_w, cnt = out_proj(x, y_s5, y_rw, bonus, g_rw, y_mla, y_ret, rw_ln_g[l], rw_ln_b[l],
                                              w_out[l], g1, norm_ffn_g[l], sc2, sh2, router_w[l], router_b[l])
        counts = jnp.sum(cnt[:, :, 0, :N_EXPERTS], axis=(0, 1)).astype(jnp.int32)
        yg = moe_ffn(hn, top_idx, counts, l, ex_w_gate, ex_b_gate, ex_w_up, ex_b_up, ex_w_down, ex_b_down)
        ffn = (yg, top_w, g2)
    return final_combine_norm(x, *ffn, final_norm_g)
```

```python
import functools

import numpy as np
import jax
import jax.numpy as jnp
from jax import lax
from jax.experimental import pallas as pl
from jax.experimental.pallas import tpu as pltpu

F32 = jnp.float32
BF16 = jnp.bfloat16

GROUP_WIDTH = 256
S5_GROUPS = 16
S5_STATE = 64
S5_FLAT = S5_GROUPS * S5_STATE
RW_HEAD = 64
RW_GN_EPS = 64e-5
MLA_HEADS = 4
MLA_NOPE = 64
MLA_ROPE = 32
MLA_V = 64
MLA_Q_RANK = 256
MLA_KV_RANK = 128
RET_HEADS = 4
RET_QK = 32
RET_V = 64
ROPE_BASE = 10000.0
N_EXPERTS = 32
TOP_K = 4
SWIGLU_ALPHA = 1.702
SWIGLU_LIMIT = 7.0
NORM_EPS = 1e-5
N_MOD = 6

LANES = 128
SUBLANES = 8
VMEM_LIMIT_BYTES = 56 * 1024 * 1024

ROW_TILE = 512
ROPE_TILE = 1024
ATTN_BLOCK = 1024
RET_CHUNK = 256
ADALN_COLS = 1536

IN_S5 = (0, 256)
IN_RW = (256, 1280)
IN_MLA = (1280, 1920)
IN_RET = (1920, 2944)
IN_PACKED = 2944


def _params(*sem):
    return pltpu.CompilerParams(dimension_semantics=sem, vmem_limit_bytes=VMEM_LIMIT_BYTES)


def _bdot(a, b):
    return jnp.dot(a.astype(BF16), b.astype(BF16), preferred_element_type=F32)


def _split_dot(a, b_bf16):
    hi = a.astype(BF16)
    lo = (a - hi.astype(F32)).astype(BF16)
    return (jnp.dot(hi, b_bf16, preferred_element_type=F32)
            + jnp.dot(lo, b_bf16, preferred_element_type=F32))


def _sigmoid(x):
    return 1.0 / (1.0 + jnp.exp(-x))


def _adaln_kernel(c_ref, w_ref, b_ref, o_ref):
    c = c_ref[...]
    cond = c * _sigmoid(c)
    o_ref[0] = _bdot(cond, w_ref[0]) + b_ref[0]


def adaln_mod(c, ada_w, ada_b):
    depth, d, n = ada_w.shape
    bsz = c.shape[0]
    tn = ADALN_COLS
    return pl.pallas_call(
        _adaln_kernel,
        grid=(depth, n // tn),
        in_specs=[pl.BlockSpec((bsz, d), lambda l, j: (0, 0)),
                  pl.BlockSpec((1, d, tn), lambda l, j: (l, 0, j)),
                  pl.BlockSpec((1, 1, tn), lambda l, j: (l, 0, j))],
        out_specs=pl.BlockSpec((1, bsz, tn), lambda l, j: (l, 0, j)),
        out_shape=jax.ShapeDtypeStruct((depth, bsz, n), F32),
        compiler_params=_params("parallel", "parallel"),
        name="adaln_mod",
    )(c, ada_w, ada_b.reshape(depth, 1, n))


def _routed_sum(y_ref, w):
    y = y_ref[0, 0].astype(F32) * w[:, 0:1]
    for j in range(1, TOP_K):
        y = y + y_ref[j, 0].astype(F32) * w[:, j:j + 1]
    return y


def _inproj_kernel(*refs, ffn):
    if ffn:
        (x_ref, y_ref, tw_ref, g2_ref, sc_ref, sh_ref, g_ref, w_ref,
         xo_ref, s5_ref, rw_ref, mla_ref, ret_ref) = refs
        x = x_ref[0] + g2_ref[0] * _routed_sum(y_ref, tw_ref[0])
        xo_ref[0] = x
    else:
        x_ref, sc_ref, sh_ref, g_ref, w_ref, s5_ref, rw_ref, mla_ref, ret_ref = refs
        x = x_ref[0]
    ms = jnp.mean(x * x, axis=-1, keepdims=True)
    hn = x * lax.rsqrt(ms + NORM_EPS) * g_ref[...]
    hn = hn * (1.0 + sc_ref[0]) + sh_ref[0]
    p = jnp.dot(hn.astype(BF16), w_ref[...], preferred_element_type=F32)
    s5_ref[0] = p[:, IN_S5[0]:IN_S5[1]]
    rw_ref[0] = p[:, IN_RW[0]:IN_RW[1]]
    mla_ref[0] = p[:, IN_MLA[0]:IN_MLA[1]].astype(BF16)
    ret_ref[0] = p[:, IN_RET[0]:IN_RET[1]].astype(BF16)


def in_proj(x, sc, sh, g, w_packed, ffn=None, tm=ROW_TILE):
    bsz, seq, d = x.shape
    widths = [b - a for a, b in (IN_S5, IN_RW, IN_MLA, IN_RET)]
    row = lambda b, i: (b, i, 0)
    per_b = lambda b, i: (b, 0, 0)
    specs = [pl.BlockSpec((1, tm, d), row)]
    args = [x]
    out_specs = [pl.BlockSpec((1, tm, w), row) for w in widths]
    out_shape = [jax.ShapeDtypeStruct((bsz, seq, w), dt) for w, dt in zip(widths, (F32, F32, BF16, BF16))]
    if ffn is not None:
        yg, top_w, gate = ffn
        specs += [pl.BlockSpec((TOP_K, 1, tm, d), lambda b, i: (0, b, i, 0)), pl.BlockSpec((1, tm, LANES), row),
                  pl.BlockSpec((1, 1, d), per_b)]
        args += [yg, top_w, gate]
        out_specs = [pl.BlockSpec((1, tm, d), row)] + out_specs
        out_shape = [jax.ShapeDtypeStruct((bsz, seq, d), F32)] + out_shape
    specs += [pl.BlockSpec((1, 1, d), per_b), pl.BlockSpec((1, 1, d), per_b),
              pl.BlockSpec((1, d), lambda b, i: (0, 0)), pl.BlockSpec((d, IN_PACKED), lambda b, i: (0, 0))]
    args += [sc, sh, g.reshape(1, d), w_packed]
    return pl.pallas_call(
        functools.partial(_inproj_kernel, ffn=ffn is not None),
        grid=(bsz, seq // tm),
        in_specs=specs,
        out_specs=out_specs,
        out_shape=out_shape,
        compiler_params=_params("parallel", "parallel"),
        name="in_proj",
    )(*args)


def _swap_halves(cols, block):
    cols = np.asarray(cols).reshape(-1, 2, block // 2)
    return cols[:, ::-1, :].reshape(-1)


def pack_w_in(w_in_l):
    zero = w_in_l.shape[1]
    s5 = np.arange(0, 256)
    rw = np.arange(256, 1280)
    qc = np.arange(1280, 1536)
    kvc = np.arange(1536, 1664)
    kpe = np.arange(1664, 1696)
    z = lambda n: np.full((n,), zero)
    kpe_slot = np.concatenate([z(MLA_NOPE), kpe, z(LANES - MLA_NOPE - MLA_ROPE)])
    kpe_sw_slot = np.concatenate([z(MLA_NOPE), _swap_halves(kpe, MLA_ROPE), z(LANES - MLA_NOPE - MLA_ROPE)])
    rq = np.arange(1696, 1824)
    rk = np.arange(1824, 1952)
    rv = np.arange(1952, 2208)
    rg = np.arange(2208, 2464)
    idx = np.concatenate([s5, rw, qc, kvc, kpe_slot, kpe_sw_slot,
                          rq, rk, _swap_halves(rq, RET_QK), _swap_halves(rk, RET_QK), rv, rg])
    assert idx.shape[0] == IN_PACKED
    w_ext = jnp.concatenate([w_in_l, jnp.zeros((w_in_l.shape[0], 1), w_in_l.dtype)], axis=1)
    return jnp.take(w_ext, jnp.asarray(idx, jnp.int32), axis=1).astype(BF16)


def _rope_kernel(pos_ref, inv_ref, sgn_ref, cos_ref, sin_ref):
    ang = pos_ref[0].astype(F32) * inv_ref[...]
    cos_ref[0] = jnp.cos(ang)
    sin_ref[0] = jnp.sin(ang) * sgn_ref[...]


def rope_tables(positions, tm=ROPE_TILE):
    bsz, seq = positions.shape
    half = MLA_ROPE // 2
    inv = ROPE_BASE ** (-jnp.arange(0, MLA_ROPE, 2, dtype=F32) / MLA_ROPE)
    inv_t = jnp.tile(inv, LANES // half).reshape(1, LANES)
    sgn = jnp.asarray(np.where((np.arange(LANES) % MLA_ROPE) < half, -1.0, 1.0), F32).reshape(1, LANES)
    row = lambda b, i: (b, i, 0)
    c2 = lambda b, i: (0, 0)
    return pl.pallas_call(
        _rope_kernel,
        grid=(bsz, seq // tm),
        in_specs=[pl.BlockSpec((1, tm, 1), row), pl.BlockSpec((1, LANES), c2), pl.BlockSpec((1, LANES), c2)],
        out_specs=[pl.BlockSpec((1, tm, LANES), row)] * 2,
        out_shape=[jax.ShapeDtypeStruct((bsz, seq, LANES), F32)] * 2,
        compiler_params=_params("parallel", "parallel"),
        name="rope_tables",
    )(positions.reshape(bsz, seq, 1), inv_t, sgn)


S5_UNROLL = 8


def _s5_kernel(u_ref, bre_ref, bim_ref, cre_ref, cim_ref, lam_ref, lamq_ref, lamseg_ref,
               d_ref, gw_ref, gb_ref, o_ref, sre_ref, sim_ref, st_ref, *, nq):
    @pl.when(pl.program_id(1) == 0)
    def _():
        st_ref[...] = jnp.zeros_like(st_ref)

    u = u_ref[0]
    ub = u.astype(BF16)
    sre_ref[...] = jnp.dot(ub, bre_ref[...], preferred_element_type=F32)
    sim_ref[...] = jnp.dot(ub, bim_ref[...], preferred_element_type=F32)
    lam_re = lam_ref[0:1, :]
    lam_im = lam_ref[1:2, :]

    def scan_body(q, carry):
        cr, ci = carry
        rows = pl.ds(pl.multiple_of(q * SUBLANES, SUBLANES), SUBLANES)
        nr = lam_re * cr - lam_im * ci + sre_ref[rows, :]
        ni = lam_re * ci + lam_im * cr + sim_ref[rows, :]
        sre_ref[rows, :] = nr
        sim_ref[rows, :] = ni
        return nr, ni

    zero = jnp.zeros((SUBLANES, S5_FLAT), F32)
    end_re, end_im = lax.fori_loop(0, nq, scan_body, (zero, zero), unroll=S5_UNROLL)

    seg_re = lamseg_ref[0:1, :]
    seg_im = lamseg_ref[1:2, :]
    cr, ci = st_ref[0:1, :], st_ref[1:2, :]
    in_re, in_im = [], []
    for r in range(SUBLANES):
        in_re.append(cr)
        in_im.append(ci)
        er, ei = end_re[r:r + 1, :], end_im[r:r + 1, :]
        cr, ci = seg_re * cr - seg_im * ci + er, seg_re * ci + seg_im * cr + ei
    st_ref[0:1, :] = cr
    st_ref[1:2, :] = ci
    car_re = jnp.concatenate(in_re, axis=0)
    car_im = jnp.concatenate(in_im, axis=0)

    def fix_body(q, _):
        rows = pl.ds(pl.multiple_of(q * SUBLANES, SUBLANES), SUBLANES)
        pr = lamq_ref[0, pl.ds(q, 1), :]
        pi = lamq_ref[1, pl.ds(q, 1), :]
        sre_ref[rows, :] = sre_ref[rows, :] + (pr * car_re - pi * car_im)
        sim_ref[rows, :] = sim_ref[rows, :] + (pr * car_im + pi * car_re)
        return 0

    lax.fori_loop(0, nq, fix_body, 0, unroll=S5_UNROLL)

    y = (jnp.dot(sre_ref[...].astype(BF16), cre_ref[...], preferred_element_type=F32)
         - jnp.dot(sim_ref[...].astype(BF16), cim_ref[...], preferred_element_type=F32))
    y = y + d_ref[...] * u
    y = jax.nn.gelu(y)
    gate = jnp.dot(y.astype(BF16), gw_ref[...], preferred_element_type=F32) + gb_ref[...]
    o_ref[0] = y * _sigmoid(gate)


def s5_prepare(lam_re, lam_im, log_step, b_re, b_im, c_re, c_im, nq):
    dt = jnp.exp(log_step.astype(F32))[:, None]
    mag = jnp.exp(lam_re * dt)
    lb_re = mag * jnp.cos(lam_im * dt)
    lb_im = mag * jnp.sin(lam_im * dt)
    den = lam_re * lam_re + lam_im * lam_im
    n_re = lb_re - 1.0
    f_re = (n_re * lam_re + lb_im * lam_im) / den
    f_im = (lb_im * lam_re - n_re * lam_im) / den
    bb_re = f_re[..., None] * b_re - f_im[..., None] * b_im
    bb_im = f_re[..., None] * b_im + f_im[..., None] * b_re
    eye = jnp.eye(S5_GROUPS, dtype=F32)
    bd_in = lambda t: jnp.einsum('gph,gk->ghkp', t, eye).reshape(GROUP_WIDTH, S5_FLAT).astype(BF16)
    bd_out = lambda t: jnp.einsum('ghp,gk->gpkh', t, eye).reshape(S5_FLAT, GROUP_WIDTH).astype(BF16)
    lam = jnp.stack([lb_re.reshape(-1), lb_im.reshape(-1)])

    def power(n):
        n = n[:, None, None]
        m = jnp.exp(n * (lam_re * dt)[None])
        a = n * (lam_im * dt)[None]
        return jnp.stack([(m * jnp.cos(a)).reshape(-1, S5_FLAT), (m * jnp.sin(a)).reshape(-1, S5_FLAT)])

    lam_q = power(jnp.arange(1, nq + 1, dtype=F32))
    lam_seg = power(jnp.full((1,), float(nq), F32))[:, 0, :]
    return bd_in(bb_re), bd_in(bb_im), bd_out(c_re), bd_out(c_im), lam, lam_q, lam_seg


def s5_mixer(u, prep, d_skip, glu_w, glu_b, chunk=ROW_TILE):
    bsz, seq, w = u.shape
    nq = chunk // SUBLANES
    nchunk = seq // chunk
    bre, bim, cre, cim, lam, lam_q, lam_seg = prep
    up = u.reshape(bsz, nchunk, SUBLANES, nq, w).transpose(0, 1, 3, 2, 4).reshape(bsz, seq, w)
    const2 = lambda b, i: (0, 0)
    out = pl.pallas_call(
        functools.partial(_s5_kernel, nq=nq),
        grid=(bsz, nchunk),
        in_specs=[pl.BlockSpec((1, chunk, w), lambda b, i: (b, i, 0)),
                  pl.BlockSpec((w, S5_FLAT), const2),
                  pl.BlockSpec((w, S5_FLAT), const2),
                  pl.BlockSpec((S5_FLAT, w), const2),
                  pl.BlockSpec((S5_FLAT, w), const2),
                  pl.BlockSpec((2, S5_FLAT), const2),
                  pl.BlockSpec((2, nq, S5_FLAT), lambda b, i: (0, 0, 0)),
                  pl.BlockSpec((2, S5_FLAT), const2),
                  pl.BlockSpec((1, w), const2),
                  pl.BlockSpec((w, w), const2),
                  pl.BlockSpec((1, w), const2)],
        out_specs=pl.BlockSpec((1, chunk, w), lambda b, i: (b, i, 0)),
        out_shape=jax.ShapeDtypeStruct((bsz, seq, w), F32),
        scratch_shapes=[pltpu.VMEM((chunk, S5_FLAT), F32),
                        pltpu.VMEM((chunk, S5_FLAT), F32),
                        pltpu.VMEM((2, S5_FLAT), F32)],
        compiler_params=_params("parallel", "arbitrary"),
        name="s5_mixer",
    )(up, bre, bim, cre, cim, lam, lam_q, lam_seg,
      d_skip.reshape(1, w), glu_w.astype(BF16), glu_b.reshape(1, w))
    return out.reshape(bsz, nchunk, nq, SUBLANES, w).transpose(0, 1, 3, 2, 4).reshape(bsz, seq, w)


def _ret_kernel(h_ref, cos_ref, sin_ref, intra_ref, qw_ref, kw_ref, dec_ref, ones_ref,
                o_ref, st_ref, *, chunk):
    @pl.when(pl.program_id(1) == 0)
    def _():
        st_ref[...] = jnp.zeros_like(st_ref)

    h = h_ref[0].astype(F32)
    cos = cos_ref[0]
    sin = sin_ref[0]
    nqk = RET_HEADS * RET_QK
    q = h[:, 0:nqk] * cos + h[:, 2 * nqk:3 * nqk] * sin
    k = (h[:, nqk:2 * nqk] * cos + h[:, 3 * nqk:4 * nqk] * sin) * (RET_QK ** -0.5)
    v = h[:, 4 * nqk:4 * nqk + GROUP_WIDTH]
    g = h[:, 4 * nqk + GROUP_WIDTH:]
    kb = k.astype(BF16)
    lane_qk = lax.broadcasted_iota(jnp.int32, (chunk, nqk), 1) // RET_QK
    lane_v = lax.broadcasted_iota(jnp.int32, (chunk, GROUP_WIDTH), 1) // RET_V
    state = st_ref[...]
    o = _bdot(q * qw_ref[...], state)
    for hd in range(RET_HEADS):
        qh = jnp.where(lane_qk == hd, q, 0.0).astype(BF16)
        s = lax.dot_general(qh, kb, (((1,), (1,)), ((), ())), preferred_element_type=F32)
        s = s * intra_ref[hd]
        vh = jnp.where(lane_v == hd, v, 0.0).astype(BF16)
        o = o + jnp.dot(s.astype(BF16), vh, preferred_element_type=F32)
    kv = lax.dot_general((k * kw_ref[...]).astype(BF16), v.astype(BF16),
                         (((0,), (0,)), ((), ())), preferred_element_type=F32)
    dec = dec_ref[...]
    st_ref[...] = state * dec + jnp.where(dec > 0.0, kv, 0.0)
    ms = _split_dot(o * o, ones_ref[...])
    o = o * lax.rsqrt(ms + NORM_EPS)
    o_ref[0] = o * (g * _sigmoid(g))


def retention_tables(chunk):
    log_gamma = np.log1p(-np.exp2(-5.0 - np.arange(RET_HEADS, dtype=np.float64)))
    idx = np.arange(chunk, dtype=np.float64)
    diff = idx[:, None] - idx[None, :]
    intra = np.where(diff >= 0, np.exp(np.maximum(diff, 0.0)[None] * log_gamma[:, None, None]), 0.0)
    q_w = np.repeat(np.exp((idx + 1.0)[:, None] * log_gamma[None, :]), RET_QK, axis=1)
    k_w = np.repeat(np.exp((chunk - 1.0 - idx)[:, None] * log_gamma[None, :]), RET_QK, axis=1)
    head_q = np.arange(RET_HEADS * RET_QK) // RET_QK
    head_v = np.arange(GROUP_WIDTH) // RET_V
    same = head_q[:, None] == head_v[None, :]
    dec = np.where(same, np.exp(chunk * log_gamma)[head_q][:, None], 0.0)
    ones = (head_v[:, None] == head_v[None, :]).astype(np.float64) / RET_V
    f = lambda a: jnp.asarray(a, F32)
    return f(intra), f(q_w), f(k_w), f(dec), jnp.asarray(ones, BF16)


def retention_mixer(h, cos_t, sin_t, chunk=RET_CHUNK):
    bsz, seq, wh = h.shape
    intra, q_w, k_w, dec, ones = retention_tables(chunk)
    nqk = RET_HEADS * RET_QK
    row = lambda b, i: (b, i, 0)
    c2 = lambda b, i: (0, 0)
    return pl.pallas_call(
        functools.partial(_ret_kernel, chunk=chunk),
        grid=(bsz, seq // chunk),
        in_specs=[pl.BlockSpec((1, chunk, wh), row),
                  pl.BlockSpec((1, chunk, LANES), row),
                  pl.BlockSpec((1, chunk, LANES), row),
                  pl.BlockSpec((RET_HEADS, chunk, chunk), lambda b, i: (0, 0, 0)),
                  pl.BlockSpec((chunk, nqk), c2),
                  pl.BlockSpec((chunk, nqk), c2),
                  pl.BlockSpec((nqk, GROUP_WIDTH), c2),
                  pl.BlockSpec((GROUP_WIDTH, GROUP_WIDTH), c2)],
        out_specs=pl.BlockSpec((1, chunk, GROUP_WIDTH), row),
        out_shape=jax.ShapeDtypeStruct((bsz, seq, GROUP_WIDTH), F32),
        scratch_shapes=[pltpu.VMEM((nqk, GROUP_WIDTH), F32)],
        compiler_params=_params("parallel", "arbitrary"),
        name="retention",
    )(h, cos_t, sin_t, intra, q_w, k_w, dec, ones)


def _mla_prep_kernel(h_ref, cos_ref, sin_ref, qg_ref, kvg_ref, wqa_ref, wqb_ref, wk_ref, wv_ref,
                     q_ref, k_ref, v_ref, *, scale):
    h = h_ref[0].astype(F32)
    tm = h.shape[0]
    lane = lax.broadcasted_iota(jnp.int32, (tm, LANES), 1)
    is_nope = lane < MLA_NOPE
    is_rope = jnp.logical_and(lane >= MLA_NOPE, lane < MLA_NOPE + MLA_ROPE)
    cm = jnp.where(is_nope, 1.0, jnp.where(is_rope, cos_ref[0], 0.0))
    sm = jnp.where(is_rope, sin_ref[0], 0.0)

    qc = h[:, 0:MLA_Q_RANK]
    qn = (qc * lax.rsqrt(jnp.mean(qc * qc, axis=-1, keepdims=True) + NORM_EPS) * qg_ref[...]).astype(BF16)
    kvc = h[:, MLA_Q_RANK:MLA_Q_RANK + MLA_KV_RANK]
    kvn = (kvc * lax.rsqrt(jnp.mean(kvc * kvc, axis=-1, keepdims=True) + NORM_EPS) * kvg_ref[...]).astype(BF16)
    off = MLA_Q_RANK + MLA_KV_RANK
    kpe = h[:, off:off + LANES] * cm + h[:, off + LANES:off + 2 * LANES] * sm

    qa = jnp.dot(qn, wqa_ref[...], preferred_element_type=F32)
    qb = jnp.dot(qn, wqb_ref[...], preferred_element_type=F32)
    kn = jnp.dot(kvn, wk_ref[...], preferred_element_type=F32)
    v_ref[0] = jnp.dot(kvn, wv_ref[...], preferred_element_type=F32).astype(BF16)
    for hd in range(MLA_HEADS):
        sl = slice(hd * LANES, (hd + 1) * LANES)
        q_ref[0, hd] = ((qa[:, sl] * cm + qb[:, sl] * sm) * scale).astype(BF16)
        k_ref[0, hd] = (kn[:, sl] + kpe).astype(BF16)


def mla_pack_weights(w_q_up, w_kv_up):
    dq = MLA_NOPE + MLA_ROPE
    zq = w_q_up.shape[1]
    zk = w_kv_up.shape[1]
    z = lambda n, zero: np.full((n,), zero)
    ia, ib, ik, iv = [], [], [], []
    for hd in range(MLA_HEADS):
        nope = np.arange(hd * dq, hd * dq + MLA_NOPE)
        pe = np.arange(hd * dq + MLA_NOPE, (hd + 1) * dq)
        pad = LANES - dq
        ia += [nope, pe, z(pad, zq)]
        ib += [z(MLA_NOPE, zq), _swap_halves(pe, MLA_ROPE), z(pad, zq)]
        kv0 = hd * (MLA_NOPE + MLA_V)
        ik += [np.arange(kv0, kv0 + MLA_NOPE), z(LANES - MLA_NOPE, zk)]
        iv += [np.arange(kv0 + MLA_NOPE, kv0 + MLA_NOPE + MLA_V)]
    ext = lambda w: jnp.concatenate([w, jnp.zeros((w.shape[0], 1), w.dtype)], axis=1)
    take = lambda w, idx: jnp.take(ext(w), jnp.asarray(np.concatenate(idx), jnp.int32), axis=1).astype(BF16)
    return take(w_q_up, ia), take(w_q_up, ib), take(w_kv_up, ik), take(w_kv_up, iv)


def mla_prep(h, cos_t, sin_t, q_norm_g, kv_norm_g, packed, tm=ROW_TILE):
    bsz, seq, wh = h.shape
    wqa, wqb, wk, wv = packed
    row = lambda b, i: (b, i, 0)
    c2 = lambda b, i: (0, 0)
    hrow = lambda b, i: (b, 0, i, 0)
    scale = (MLA_NOPE + MLA_ROPE) ** -0.5
    return pl.pallas_call(
        functools.partial(_mla_prep_kernel, scale=scale),
        grid=(bsz, seq // tm),
        in_specs=[pl.BlockSpec((1, tm, wh), row),
                  pl.BlockSpec((1, tm, LANES), row),
                  pl.BlockSpec((1, tm, LANES), row),
                  pl.BlockSpec((1, MLA_Q_RANK), c2),
                  pl.BlockSpec((1, MLA_KV_RANK), c2),
                  pl.BlockSpec(wqa.shape, c2),
                  pl.BlockSpec(wqb.shape, c2),
                  pl.BlockSpec(wk.shape, c2),
                  pl.BlockSpec(wv.shape, c2)],
        out_specs=[pl.BlockSpec((1, MLA_HEADS, tm, LANES), hrow),
                   pl.BlockSpec((1, MLA_HEADS, tm, LANES), hrow),
                   pl.BlockSpec((1, tm, GROUP_WIDTH), row)],
        out_shape=[jax.ShapeDtypeStruct((bsz, MLA_HEADS, seq, LANES), BF16),
                   jax.ShapeDtypeStruct((bsz, MLA_HEADS, seq, LANES), BF16),
                   jax.ShapeDtypeStruct((bsz, seq, GROUP_WIDTH), BF16)],
        compiler_params=_params("parallel", "parallel"),
        name="mla_prep",
    )(h, cos_t, sin_t, q_norm_g.reshape(1, -1), kv_norm_g.reshape(1, -1), wqa, wqb, wk, wv)


def _attn_kernel(q_ref, k_ref, v_ref, o_ref, m_ref, l_ref, acc_ref, *, blk):
    qi = pl.program_id(2)
    heads = range(2)

    def step(q_rows, key0, size, masked):
        keys = pl.ds(pl.multiple_of(key0, size), size)
        vs = v_ref[0, keys, :]
        s = [lax.dot_general(q_ref[0, hh, q_rows, :], k_ref[0, hh, keys, :], (((1,), (1,)), ((), ())),
                             preferred_element_type=F32) for hh in heads]
        if masked:
            r = lax.broadcasted_iota(jnp.int32, (size, size), 0)
            c = lax.broadcasted_iota(jnp.int32, (size, size), 1)
            s = [jnp.where(c <= r, t, -jnp.inf) for t in s]
        m_prev = [m_ref[hh, q_rows, :] for hh in heads]
        m_new = [jnp.maximum(m_prev[hh], jnp.max(s[hh], axis=-1, keepdims=True)) for hh in heads]
        p = [jnp.exp(s[hh] - jnp.concatenate([m_new[hh]] * (size // LANES), axis=1)) for hh in heads]
        alpha = [jnp.exp(m_prev[hh] - m_new[hh]) for hh in heads]
        for hh in heads:
            l_ref[hh, q_rows, :] = alpha[hh] * l_ref[hh, q_rows, :] + jnp.sum(p[hh], axis=-1, keepdims=True)
            acc_ref[hh, q_rows, :] = (alpha[hh] * acc_ref[hh, q_rows, :]
                                      + jnp.dot(p[hh].astype(BF16), vs, preferred_element_type=F32))
            m_ref[hh, q_rows, :] = m_new[hh]

    for hh in heads:
        m_ref[hh] = jnp.full((blk, LANES), -jnp.inf, F32)
        l_ref[hh] = jnp.zeros((blk, LANES), F32)
        acc_ref[hh] = jnp.zeros((blk, LANES), F32)

    def body(j, _):
        step(slice(0, blk), j * blk, blk, False)
        return 0

    lax.fori_loop(0, qi, body, 0)
    half = blk // 2
    step(slice(0, half), qi * blk, half, True)
    step(slice(half, blk), qi * blk, half, False)
    step(slice(half, blk), qi * blk + half, half, True)

    lane = lax.broadcasted_iota(jnp.int32, (blk, LANES), 1)
    o0 = acc_ref[0] / l_ref[0]
    o1 = acc_ref[1] / l_ref[1]
    o_ref[0] = jnp.where(lane < MLA_V, o0, o1)


def causal_attention(q, k, v, blk=ATTN_BLOCK):
    bsz, nh, seq, dk = q.shape
    return pl.pallas_call(
        functools.partial(_attn_kernel, blk=blk),
        grid=(bsz, nh // 2, seq // blk),
        in_specs=[pl.BlockSpec((1, 2, blk, dk), lambda b, p, i: (b, p, i, 0)),
                  pl.BlockSpec((1, 2, seq, dk), lambda b, p, i: (b, p, 0, 0)),
                  pl.BlockSpec((1, seq, LANES), lambda b, p, i: (b, 0, p))],
        out_specs=pl.BlockSpec((1, blk, LANES), lambda b, p, i: (b, i, p)),
        out_shape=jax.ShapeDtypeStruct((bsz, seq, nh // 2 * LANES), F32),
        scratch_shapes=[pltpu.VMEM((2, blk, LANES), F32)] * 3,
        compiler_params=_params("parallel", "parallel", "arbitrary"),
        name="mla_attention",
    )(q, k, v)


def _rw_prep_kernel(*refs, first):
    if first:
        (h_ref, prev_ref, mu_ref, w0_ref, w2_ref, a0_ref, a2_ref, g2_ref, kk_ref, ka_ref, rk_ref, ones_ref, tri_ref,
         q_ref, y0_ref, phi_ref, dlt_ref, v_ref, bonus_ref, g_ref) = refs
    else:
        (h_ref, prev_ref, mu_ref, w0_ref, w2_ref, a0_ref, a2_ref, g2_ref, kk_ref, ka_ref, rk_ref, ones_ref, tri_ref,
         vf_ref, v0_ref, v1_ref, v2_ref,
         q_ref, y0_ref, phi_ref, dlt_ref, v_ref, bonus_ref, g_ref) = refs
    h = h_ref[0]
    tm = h.shape[0]
    last = prev_ref[0, SUBLANES - 1:SUBLANES, :]
    last = jnp.where(pl.program_id(1) == 0, 0.0, last)
    row = lax.broadcasted_iota(jnp.int32, h.shape, 0)
    h_prev = jnp.where(row == 0, last, pltpu.roll(h, 1, axis=0))
    h = h + (h_prev - h) * mu_ref[...]
    w = GROUP_WIDTH
    r = h[:, 0:w]
    k = h[:, w:2 * w]
    v = h[:, 2 * w:3 * w]
    wa = h[:, 3 * w:3 * w + LANES]
    gd = h[:, 3 * w + LANES:]
    z = w0_ref[...] + _bdot(jnp.tanh(wa), w2_ref[...])
    nz = -z
    softplus = jnp.maximum(nz, 0.0) + jnp.log(1.0 + jnp.exp(-jnp.abs(nz)))
    log_decay = -jnp.exp(-softplus - 0.5)
    a = _sigmoid(a0_ref[...] + _bdot(wa, a2_ref[...]))
    g_ref[0] = _bdot(_sigmoid(gd), g2_ref[...])
    if not first:
        mix = _sigmoid(v0_ref[...] + _bdot(_bdot(v, v1_ref[...]), v2_ref[...]))
        v = v + (vf_ref[0] - v) * mix
    kk = k * kk_ref[...]
    norm = jnp.sqrt(_split_dot(kk * kk, ones_ref[...]))
    kk = kk / jnp.maximum(norm, 1e-12)
    k = k * (1.0 + (a - 1.0) * ka_ref[...])
    v_ref[0] = v
    bonus_ref[0] = _split_dot(r * k * rk_ref[...], ones_ref[...]) * v
    _wkv_chunk_terms(r, log_decay, k, kk, kk * a, v, tri_ref[...], q_ref.at[0], y0_ref.at[0], phi_ref.at[0], dlt_ref.at[0],
                     tm // WKV_CHUNK)


def rw_pack_weights(w2, a2, v1=None, v2=None):
    zeros = lambda n, m: jnp.zeros((n, m), F32)
    half = LANES // 2
    w2p = jnp.concatenate([w2, zeros(half, GROUP_WIDTH)], axis=0).astype(BF16)
    a2p = jnp.concatenate([zeros(half, GROUP_WIDTH), a2], axis=0).astype(BF16)
    if v1 is None:
        return w2p, a2p
    v1p = jnp.concatenate([v1, zeros(GROUP_WIDTH, LANES - v1.shape[1])], axis=1).astype(BF16)
    v2p = jnp.concatenate([v2, zeros(LANES - v2.shape[0], GROUP_WIDTH)], axis=0).astype(BF16)
    return w2p, a2p, v1p, v2p


def head_ones(width, head, scale):
    hd = np.arange(width) // head
    return jnp.asarray((hd[:, None] == hd[None, :]).astype(np.float32) * scale, BF16)


def rw_prep(h, mu, w0, w2p, a0, a2p, g2, k_k, k_a, r_k, vmix=None, tm=ROW_TILE):
    bsz, seq, wh = h.shape
    w = GROUP_WIDTH
    first = vmix is None
    row = lambda b, i: (b, i, 0)
    c2 = lambda b, i: (0, 0)
    vec = lambda t: t.reshape(1, -1)
    prev_map = lambda b, i: (b, jnp.maximum(i * (tm // SUBLANES) - 1, 0), 0)
    c = WKV_CHUNK
    npair = w // LANES
    tri = jnp.asarray(np.tril(np.ones((c, c), np.float32)), BF16)
    args = [h, h, vec(mu), vec(w0), w2p, vec(a0), a2p, g2.astype(BF16), vec(k_k), vec(k_a), vec(r_k),
            head_ones(w, RW_HEAD, 1.0), tri]
    specs = [pl.BlockSpec((1, tm, wh), row), pl.BlockSpec((1, SUBLANES, wh), prev_map),
             pl.BlockSpec((1, wh), c2), pl.BlockSpec((1, w), c2), pl.BlockSpec((LANES, w), c2),
             pl.BlockSpec((1, w), c2), pl.BlockSpec((LANES, w), c2), pl.BlockSpec((LANES, w), c2),
             pl.BlockSpec((1, w), c2), pl.BlockSpec((1, w), c2), pl.BlockSpec((1, w), c2),
             pl.BlockSpec((w, w), c2), pl.BlockSpec((c, c), c2)]
    if not first:
        v_first, v0, v1p, v2p = vmix
        args += [v_first, vec(v0), v1p, v2p]
        specs += [pl.BlockSpec((1, tm, w), row), pl.BlockSpec((1, w), c2),
                  pl.BlockSpec((w, LANES), c2), pl.BlockSpec((LANES, w), c2)]
    act = pl.BlockSpec((1, tm, w), row)
    mat = pl.BlockSpec((1, tm // c, npair, LANES, LANES), lambda b, i: (b, i, 0, 0, 0))
    ashape = jax.ShapeDtypeStruct((bsz, seq, w), F32)
    mshape = jax.ShapeDtypeStruct((bsz, seq // c, npair, LANES, LANES), F32)
    return pl.pallas_call(
        functools.partial(_rw_prep_kernel, first=first),
        grid=(bsz, seq // tm),
        in_specs=specs,
        out_specs=[act, act, mat, mat, act, act, act],
        out_shape=[ashape, ashape, mshape, mshape, ashape, ashape, ashape],
        compiler_params=_params("parallel", "parallel"),
        name="rwkv_prep",
    )(*args)


WKV_CHUNK = 64
_NN = (((1,), (0,)), ((), ()))
_NT = (((1,), (1,)), ((), ()))
_TN = (((0,), (0,)), ((), ()))


def _split(a):
    hi = a.astype(BF16)
    return hi, (a - hi.astype(F32)).astype(BF16)


def _mm(a, b, dims=_NN, passes=1):
    if passes == 1:
        return lax.dot_general(a.astype(BF16), b.astype(BF16), dims, preferred_element_type=F32)
    ah, al = _split(a)
    bh, bl = _split(b)
    d = lambda x, y: lax.dot_general(x, y, dims, preferred_element_type=F32)
    return d(ah, bh) + (d(ah, bl) + d(al, bh))


def _wkv_chunk_terms(r_all, lw_all, k_all, kk_all, b_all, v_all, tri, q_ref, y0_ref, phi_ref, dlt_ref, nchunk):
    c = WKV_CHUNK
    n2 = 2 * c
    lane = lax.broadcasted_iota(jnp.int32, (c, LANES), 1)
    head0 = lane < RW_HEAD
    row = lax.broadcasted_iota(jnp.int32, (n2, n2), 0)
    col = lax.broadcasted_iota(jnp.int32, (n2, n2), 1)
    strict = col < row
    incl = col <= row
    eye = (row == col).astype(F32)
    stack = lambda x: jnp.concatenate([jnp.where(head0, x, 0.0), jnp.where(head0, 0.0, x)], axis=0)
    fold = lambda x: x[:c] + x[c:]
    chains = [(ci, p) for ci in range(nchunk) for p in range(GROUP_WIDTH // LANES)]
    st = []
    for ci, p in chains:
        sl = slice(p * LANES, (p + 1) * LANES)
        rows = slice(ci * c, (ci + 1) * c)
        lw = lw_all[rows, sl]
        lw_hi, lw_lo = _split(lw)
        cum = (jnp.dot(tri, lw_hi, preferred_element_type=F32)
               + jnp.dot(tri, lw_lo, preferred_element_type=F32))
        last = cum[c - 1:c, :]
        e_in = jnp.exp(cum)
        e_ex = jnp.exp(cum - lw)
        e_neg = jnp.exp(-cum)
        e_end = jnp.exp(last - cum)
        kk, r, k, b, v = kk_all[rows, sl], r_all[rows, sl], k_all[rows, sl], b_all[rows, sl], v_all[rows, sl]
        kt2 = stack(kk * e_ex)
        rt2 = stack(r * e_in)
        lhs = jnp.concatenate([kt2, rt2], axis=0)
        gb = _mm(lhs, stack(b * e_neg), _NT)
        gk = _mm(lhs, stack(k * e_neg), _NT)
        x = -jnp.where(strict, gb[:n2], 0.0)
        st.append(dict(sl=sl, rows=rows, kt2=kt2, rt2=rt2, kp2=stack(k * e_end), bp2=stack(b * e_end), v2=stack(v),
                       g2=jnp.where(strict, gk[:n2], 0.0), rb2=jnp.where(incl, gb[n2:], 0.0),
                       rk2=jnp.where(incl, gk[n2:], 0.0), last=last, x=x, t=eye + x))
    for _ in range(int(np.log2(c)) - 1):
        for d in st:
            d["x"] = _mm(d["x"], d["x"])
        for d in st:
            d["t"] = d["t"] + _mm(d["t"], d["x"])
    for d in st:
        d["gv2"] = _mm(d["g2"], d["v2"])
    for d in st:
        d["tku"] = _mm(d["t"], jnp.concatenate([d["kt2"], d["gv2"]], axis=1))
    for d in st:
        d["rbz"] = _mm(d["rb2"], d["tku"])
    for (ci, p), d in zip(chains, st):
        sl, rows, tku, rbz = d["sl"], d["rows"], d["tku"], d["rbz"]
        q_ref[rows, sl] = fold(d["rt2"] - rbz[:, :LANES])
        y0_ref[rows, sl] = fold(_mm(d["rk2"], d["v2"]) - rbz[:, LANES:])
        decay = jnp.where(row == col, jnp.broadcast_to(jnp.exp(d["last"]), (n2, LANES)), 0.0)
        phi_ref[ci, p] = decay - _mm(d["bp2"], tku[:, :LANES], _TN)
        dlt_ref[ci, p] = _mm(d["kp2"], d["v2"], _TN) - _mm(d["bp2"], tku[:, LANES:], _TN)


def _wkv_seq_kernel(q_ref, y0_ref, phi_ref, dlt_ref, y_ref, st_ref, *, nchunk):
    @pl.when(pl.program_id(1) == 0)
    def _():
        st_ref[...] = jnp.zeros_like(st_ref)

    c = WKV_CHUNK
    npair = GROUP_WIDTH // LANES
    m = [st_ref[p] for p in range(npair)]
    for i in range(nchunk):
        rows = slice(i * c, (i + 1) * c)
        for p in range(npair):
            sl = slice(p * LANES, (p + 1) * LANES)
            y_ref[0, rows, sl] = _mm(q_ref[0, rows, sl], m[p]) + y0_ref[0, rows, sl]
        m = [_mm(phi_ref[0, i, p], m[p], _NN, 3) + dlt_ref[0, i, p] for p in range(npair)]
    for p in range(npair):
        st_ref[p] = m[p]


def wkv_sequential(q, y0, phi, dlt, block=ROW_TILE):
    bsz, seq, w = q.shape
    c = WKV_CHUNK
    nchunk = block // c
    npair = w // LANES
    row = lambda bb, i: (bb, i, 0)
    blk = pl.BlockSpec((1, block, w), row)
    mat = pl.BlockSpec((1, nchunk, npair, LANES, LANES), lambda bb, i: (bb, i, 0, 0, 0))
    return pl.pallas_call(
        functools.partial(_wkv_seq_kernel, nchunk=nchunk),
        grid=(bsz, seq // block),
        in_specs=[blk, blk, mat, mat],
        out_specs=blk,
        out_shape=jax.ShapeDtypeStruct((bsz, seq, w), F32),
        scratch_shapes=[pltpu.VMEM((npair, LANES, LANES), F32)],
        compiler_params=_params("parallel", "arbitrary"),
        name="wkv7_chunk_scan",
    )(q, y0, phi, dlt)


def _outproj_kernel(x_ref, ys5_ref, yrw_ref, bonus_ref, grw_ref, ymla_ref, yret_ref, lng_ref, lnb_ref, ones_ref,
                    wout_ref, g1_ref, ng_ref, sc_ref, sh_ref, rhi_ref, rlo_ref, rb_ref,
                    xo_ref, hn_ref, idx_ref, tw_ref, cnt_ref):
    y = yrw_ref[0]
    mean = _split_dot(y, ones_ref[...])
    yc = y - mean
    var = _split_dot(yc * yc, ones_ref[...])
    yrw = (yc * lax.rsqrt(var + RW_GN_EPS) * lng_ref[...] + lnb_ref[...] + bonus_ref[0]) * grw_ref[0]
    w = GROUP_WIDTH
    mixed = (_bdot(ys5_ref[0], wout_ref[0:w, :]) + _bdot(yrw, wout_ref[w:2 * w, :])
             + _bdot(ymla_ref[0], wout_ref[2 * w:3 * w, :]) + _bdot(yret_ref[0], wout_ref[3 * w:4 * w, :]))
    x = x_ref[0] + g1_ref[0] * mixed
    xo_ref[0] = x
    ms = jnp.mean(x * x, axis=-1, keepdims=True)
    hn = x * lax.rsqrt(ms + NORM_EPS) * ng_ref[...]
    hn = hn * (1.0 + sc_ref[0]) + sh_ref[0]
    hi = hn.astype(BF16)
    hn_ref[0] = hi
    lo = (hn - hi.astype(F32)).astype(BF16)
    logits = (jnp.dot(hi, rhi_ref[...], preferred_element_type=F32)
              + jnp.dot(lo, rhi_ref[...], preferred_element_type=F32)
              + jnp.dot(hi, rlo_ref[...], preferred_element_type=F32) + rb_ref[...])
    lane = lax.broadcasted_iota(jnp.int32, logits.shape, 1)
    lane_f = lane.astype(F32)
    cur = jnp.where(lane < N_EXPERTS, logits, -jnp.inf)
    idx_out = jnp.zeros(logits.shape, F32)
    val_out = jnp.zeros(logits.shape, F32)
    picked = jnp.zeros(logits.shape, F32)
    top = None
    denom = None
    for j in range(TOP_K):
        m = jnp.max(cur, axis=-1, keepdims=True)
        sel = jnp.min(jnp.where(cur == m, lane_f, float(LANES)), axis=-1, keepdims=True)
        hit = lane_f == sel
        cur = jnp.where(hit, -jnp.inf, cur)
        picked = picked + jnp.where(hit, 1.0, 0.0)
        top = m if top is None else top
        e = jnp.exp(m - top)
        denom = e if denom is None else denom + e
        idx_out = jnp.where(lane == j, sel, idx_out)
        val_out = jnp.where(lane == j, e, val_out)
    idx_ref[0] = idx_out.astype(jnp.int32)
    tw_ref[0] = val_out / denom
    cnt_ref[0, 0] = jnp.broadcast_to(jnp.sum(picked, axis=0, keepdims=True), (SUBLANES, LANES))


def out_proj(x, y_s5, y_rw, bonus, g_rw, y_mla, y_ret, ln_g, ln_b, w_out, g1, norm_g, sc2, sh2,
             router_w, router_b, tm=ROW_TILE):
    bsz, seq, d = x.shape
    w = GROUP_WIDTH
    row = lambda b, i: (b, i, 0)
    per_b = lambda b, i: (b, 0, 0)
    c2 = lambda b, i: (0, 0)
    vec = lambda t: t.reshape(1, -1)
    pad = LANES - router_w.shape[1]
    rw_pad = jnp.concatenate([router_w, jnp.zeros((d, pad), F32)], axis=1)
    r_hi = rw_pad.astype(BF16)
    r_lo = (rw_pad - r_hi.astype(F32)).astype(BF16)
    rb = jnp.concatenate([router_b, jnp.zeros((pad,), F32)]).reshape(1, LANES)
    mixer = pl.BlockSpec((1, tm, w), row)
    return pl.pallas_call(
        _outproj_kernel,
        grid=(bsz, seq // tm),
        in_specs=[pl.BlockSpec((1, tm, d), row), mixer, mixer, mixer, mixer, mixer, mixer,
                  pl.BlockSpec((1, w), c2), pl.BlockSpec((1, w), c2), pl.BlockSpec((w, w), c2),
                  pl.BlockSpec((4 * w, d), c2),
                  pl.BlockSpec((1, 1, d), per_b), pl.BlockSpec((1, d), c2),
                  pl.BlockSpec((1, 1, d), per_b), pl.BlockSpec((1, 1, d), per_b),
                  pl.BlockSpec((d, LANES), c2), pl.BlockSpec((d, LANES), c2), pl.BlockSpec((1, LANES), c2)],
        out_specs=[pl.BlockSpec((1, tm, d), row), pl.BlockSpec((1, tm, d), row), pl.BlockSpec((1, tm, LANES), row),
                   pl.BlockSpec((1, tm, LANES), row), pl.BlockSpec((1, 1, SUBLANES, LANES), lambda b, i: (b, i, 0, 0))],
        out_shape=[jax.ShapeDtypeStruct((bsz, seq, d), F32), jax.ShapeDtypeStruct((bsz, seq, d), BF16),
                   jax.ShapeDtypeStruct((bsz, seq, LANES), jnp.int32), jax.ShapeDtypeStruct((bsz, seq, LANES), F32),
                   jax.ShapeDtypeStruct((bsz, seq // tm, SUBLANES, LANES), F32)],
        compiler_params=_params("parallel", "parallel"),
        name="out_proj",
    )(x, y_s5, y_rw, bonus, g_rw, y_mla, y_ret, vec(ln_g), vec(ln_b), head_ones(w, RW_HEAD, 1.0 / RW_HEAD),
      w_out.astype(BF16), g1, vec(norm_g), sc2, sh2, r_hi, r_lo, rb)


MOE_ROWS = 512


def _moe_kernel(blk_e_ref, blk_on_ref, x_ref, wg_ref, bg_ref, wu_ref, bu_ref, wd_ref, bd_ref, o_ref, wb_ref):
    i = pl.program_id(0)
    changed = jnp.logical_or(i == 0, blk_e_ref[i] != blk_e_ref[jnp.maximum(i - 1, 0)])

    @pl.when(changed)
    def _():
        wb_ref[0] = wg_ref[0, 0].astype(BF16)
        wb_ref[1] = wu_ref[0, 0].astype(BF16)
        wb_ref[2] = wd_ref[0, 0].astype(BF16)

    @pl.when(blk_on_ref[i] > 0)
    def _():
        x = x_ref[...]
        gt = jnp.minimum(jnp.dot(x, wb_ref[0], preferred_element_type=F32) + bg_ref[0, 0], SWIGLU_LIMIT)
        up = jnp.clip(jnp.dot(x, wb_ref[1], preferred_element_type=F32) + bu_ref[0, 0], -SWIGLU_LIMIT, SWIGLU_LIMIT)
        act = gt * _sigmoid(SWIGLU_ALPHA * gt) * (up + 1.0)
        o_ref[...] = (jnp.dot(act.astype(BF16), wb_ref[2], preferred_element_type=F32) + bd_ref[0, 0]).astype(BF16)


def moe_experts(xb, blk_e, blk_on, layer, w_gate, b_gate, w_up, b_up, w_down, b_down):
    p_rows, d = xb.shape
    depth, n_e, _, de = w_gate.shape
    wmap = lambda i, e, on: (layer, e[i], 0, 0)
    rows = lambda i, e, on: (i, 0)
    return pl.pallas_call(
        _moe_kernel,
        grid_spec=pltpu.PrefetchScalarGridSpec(
            num_scalar_prefetch=2,
            grid=(p_rows // MOE_ROWS,),
            in_specs=[pl.BlockSpec((MOE_ROWS, d), rows),
                      pl.BlockSpec((1, 1, d, de), wmap), pl.BlockSpec((1, 1, 1, de), wmap),
                      pl.BlockSpec((1, 1, d, de), wmap), pl.BlockSpec((1, 1, 1, de), wmap),
                      pl.BlockSpec((1, 1, de, d), wmap), pl.BlockSpec((1, 1, 1, d), wmap)],
            out_specs=pl.BlockSpec((MOE_ROWS, d), rows),
            scratch_shapes=[pltpu.VMEM((3, d, de), BF16)]),
        out_shape=jax.ShapeDtypeStruct((p_rows, d), BF16),
        compiler_params=_params("arbitrary"),
        name="moe_experts",
    )(blk_e, blk_on, xb, w_gate, b_gate.reshape(depth, n_e, 1, de), w_up, b_up.reshape(depth, n_e, 1, de),
      w_down, b_down.reshape(depth, n_e, 1, d))


def moe_route(top_idx, counts):
    t = top_idx.shape[0]
    n_assign = t * TOP_K
    flat_e = top_idx.reshape(-1)
    iota = jnp.arange(n_assign, dtype=jnp.int32)
    _, order = lax.sort((flat_e, iota), num_keys=1)
    _, rank = lax.sort((order, iota), num_keys=1)
    start = jnp.cumsum(counts) - counts
    padded = (counts + MOE_ROWS - 1) // MOE_ROWS * MOE_ROWS
    pad_end = jnp.cumsum(padded)
    pad_start = pad_end - padded
    pos = rank + (pad_start - start)[flat_e]
    p_rows = n_assign + N_EXPERTS * MOE_ROWS
    n_blocks = p_rows // MOE_ROWS
    blk_first = jnp.arange(n_blocks, dtype=jnp.int32) * MOE_ROWS
    blk_e = jnp.minimum(jnp.sum(pad_end[None, :] <= blk_first[:, None], axis=1, dtype=jnp.int32), N_EXPERTS - 1)
    blk_within = blk_first - pad_start[blk_e]
    blk_left = counts[blk_e] - blk_within
    blk_on = (blk_left > 0).astype(jnp.int32)
    r = jnp.arange(MOE_ROWS, dtype=jnp.int32)[None, :]
    src = jnp.clip((start[blk_e] + blk_within)[:, None] + r, 0, n_assign - 1)
    row = blk_first[:, None] + r
    buf_tok = jnp.where(r < blk_left[:, None], order[src.reshape(-1)].reshape(n_blocks, MOE_ROWS) // TOP_K, row % t)
    return buf_tok.reshape(-1), pos.reshape(t, TOP_K), blk_e, blk_on


def moe_ffn(hn_bf16, top_idx, counts, layer, w_gate, b_gate, w_up, b_up, w_down, b_down):
    bsz, seq, d = hn_bf16.shape
    t = bsz * seq
    buf_tok, pos, blk_e, blk_on = moe_route(top_idx.reshape(t, LANES)[:, :TOP_K], counts)
    xb = hn_bf16.reshape(t, d).at[buf_tok].get(mode='promise_in_bounds')
    yb = moe_experts(xb, blk_e, blk_on, layer, w_gate, b_gate, w_up, b_up, w_down, b_down)
    yg = yb.at[pos.T.reshape(-1)].get(mode='promise_in_bounds')
    return yg.reshape(TOP_K, bsz, seq, d)


def _final_kernel(x_ref, y_ref, w_ref, g_ref, ng_ref, o_ref):
    x = x_ref[0] + g_ref[0] * _routed_sum(y_ref, w_ref[0])
    o_ref[0] = x * lax.rsqrt(jnp.mean(x * x, axis=-1, keepdims=True) + NORM_EPS) * ng_ref[...]


def final_combine_norm(x, yg, top_w, gate, final_g, tm=ROW_TILE):
    bsz, seq, d = x.shape
    row = lambda b, i: (b, i, 0)
    return pl.pallas_call(
        _final_kernel,
        grid=(bsz, seq // tm),
        in_specs=[pl.BlockSpec((1, tm, d), row),
                  pl.BlockSpec((TOP_K, 1, tm, d), lambda b, i: (0, b, i, 0)),
                  pl.BlockSpec((1, tm, LANES), row),
                  pl.BlockSpec((1, 1, d), lambda b, i: (b, 0, 0)),
                  pl.BlockSpec((1, d), lambda b, i: (0, 0))],
        out_specs=pl.BlockSpec((1, tm, d), row),
        out_shape=jax.ShapeDtypeStruct((bsz, seq, d), F32),
        compiler_params=_params("parallel", "parallel"),
        name="final_combine_norm",
    )(x, yg, top_w, gate, final_g.reshape(1, d))


def kernel(x, c, positions, ada_w, ada_b, norm_mix_g, norm_ffn_g, w_in, w_out,
           s5_lambda_re, s5_lambda_im, s5_log_step, s5_b_re, s5_b_im, s5_c_re, s5_c_im,
           s5_d, s5_glu_w, s5_glu_b,
           rw_mu, rw_w0, rw_w2, rw_a0, rw_a2, rw_g2, rw_k_k, rw_k_a, rw_r_k, rw_ln_g, rw_ln_b,
           rw_v0, rw_v1, rw_v2,
           mla_q_norm_g, mla_kv_norm_g, mla_w_q_up, mla_w_kv_up,
           router_w, router_b, ex_w_gate, ex_b_gate, ex_w_up, ex_b_up, ex_w_down, ex_b_down,
           final_norm_g):
    depth = w_in.shape[0]
    bsz, seq, _ = x.shape
    assert seq % max(ROW_TILE, ROPE_TILE, ATTN_BLOCK) == 0 and (bsz * seq * TOP_K) % MOE_ROWS == 0
    mod = adaln_mod(c, ada_w, ada_b)
    cos_t, sin_t = rope_tables(positions)
    v_first = None
    ffn = None
    for l in range(depth):
        sh1, sc1, g1, sh2, sc2, g2 = [m[:, None, :] for m in jnp.split(mod[l], N_MOD, axis=-1)]
        proj = in_proj(x, sc1, sh1, norm_mix_g[l], pack_w_in(w_in[l]), ffn)
        if ffn is not None:
            x = proj[0]
        s5_u, rw_in, mla_in, ret_in = proj[-4:]

        prep = s5_prepare(s5_lambda_re[l], s5_lambda_im[l], s5_log_step[l], s5_b_re[l], s5_b_im[l],
                          s5_c_re[l], s5_c_im[l], ROW_TILE // SUBLANES)
        y_s5 = s5_mixer(s5_u, prep, s5_d[l], s5_glu_w[l], s5_glu_b[l])

        if l == 0:
            w2p, a2p = rw_pack_weights(rw_w2[l], rw_a2[l])
            vmix = None
        else:
            w2p, a2p, v1p, v2p = rw_pack_weights(rw_w2[l], rw_a2[l], rw_v1[l - 1], rw_v2[l - 1])
            vmix = (v_first, rw_v0[l - 1], v1p, v2p)
        q_rw, y0_rw, phi_rw, dlt_rw, v_rw, bonus, g_rw = rw_prep(
            rw_in, rw_mu[l], rw_w0[l], w2p, rw_a0[l], a2p, rw_g2[l], rw_k_k[l], rw_k_a[l],
            rw_r_k[l].reshape(-1), vmix)
        if l == 0:
            v_first = v_rw
        y_rw = wkv_sequential(q_rw, y0_rw, phi_rw, dlt_rw)

        q, k, v = mla_prep(mla_in, cos_t, sin_t, mla_q_norm_g[l], mla_kv_norm_g[l],
                           mla_pack_weights(mla_w_q_up[l], mla_w_kv_up[l]))
        y_mla = causal_attention(q, k, v)

        y_ret = retention_mixer(ret_in, cos_t, sin_t)

        x, hn, top_idx, top_w, cnt = out_proj(x, y_s5, y_rw, bonus, g_rw, y_mla, y_ret, rw_ln_g[l], rw_ln_b[l],
                                              w_out[l], g1, norm_ffn_g[l], sc2, sh2, router_w[l], router_b[l])
        counts = jnp.sum(cnt[:, :, 0, :N_EXPERTS], axis=(0, 1)).astype(jnp.int32)
        yg = moe_ffn(hn, top_idx, counts, l, ex_w_gate, ex_b_gate, ex_w_up, ex_b_up, ex_w_down, ex_b_down)
        ffn = (yg, top_w, g2)
    return final_combine_norm(x, *ffn, final_norm_g)
```

```python
import functools

import numpy as np
import jax
import jax.numpy as jnp
from jax import lax
from jax.experimental import pallas as pl
from jax.experimental.pallas import tpu as pltpu

F32 = jnp.float32
BF16 = jnp.bfloat16

GROUP_WIDTH = 256
S5_GROUPS = 16
S5_STATE = 64
S5_FLAT = S5_GROUPS * S5_STATE
RW_HEAD = 64
RW_GN_EPS = 64e-5
MLA_HEADS = 4
MLA_NOPE = 64
MLA_ROPE = 32
MLA_V = 64
MLA_Q_RANK = 256
MLA_KV_RANK = 128
RET_HEADS = 4
RET_QK = 32
RET_V = 64
ROPE_BASE = 10000.0
N_EXPERTS = 32
TOP_K = 4
SWIGLU_ALPHA = 1.702
SWIGLU_LIMIT = 7.0
NORM_EPS = 1e-5
N_MOD = 6

LANES = 128
SUBLANES = 8
VMEM_LIMIT_BYTES = 56 * 1024 * 1024

ROW_TILE = 512
ROPE_TILE = 1024
ATTN_BLOCK = 1024
RET_CHUNK = 256
ADALN_COLS = 1536

IN_S5 = (0, 256)
IN_RW = (256, 1280)
IN_MLA = (1280, 1920)
IN_RET = (1920, 2944)
IN_PACKED = 2944


def _params(*sem):
    return pltpu.CompilerParams(dimension_semantics=sem, vmem_limit_bytes=VMEM_LIMIT_BYTES)


def _bdot(a, b):
    return jnp.dot(a.astype(BF16), b.astype(BF16), preferred_element_type=F32)


def _split_dot(a, b_bf16):
    hi = a.astype(BF16)
    lo = (a - hi.astype(F32)).astype(BF16)
    return (jnp.dot(hi, b_bf16, preferred_element_type=F32)
            + jnp.dot(lo, b_bf16, preferred_element_type=F32))


def _sigmoid(x):
    return 1.0 / (1.0 + jnp.exp(-x))


def _adaln_kernel(c_ref, w_ref, b_ref, o_ref):
    c = c_ref[...]
    cond = c * _sigmoid(c)
    o_ref[0] = _bdot(cond, w_ref[0]) + b_ref[0]


def adaln_mod(c, ada_w, ada_b):
    depth, d, n = ada_w.shape
    bsz = c.shape[0]
    tn = ADALN_COLS
    return pl.pallas_call(
        _adaln_kernel,
        grid=(depth, n // tn),
        in_specs=[pl.BlockSpec((bsz, d), lambda l, j: (0, 0)),
                  pl.BlockSpec((1, d, tn), lambda l, j: (l, 0, j)),
                  pl.BlockSpec((1, 1, tn), lambda l, j: (l, 0, j))],
        out_specs=pl.BlockSpec((1, bsz, tn), lambda l, j: (l, 0, j)),
        out_shape=jax.ShapeDtypeStruct((depth, bsz, n), F32),
        compiler_params=_params("parallel", "parallel"),
        name="adaln_mod",
    )(c, ada_w, ada_b.reshape(depth, 1, n))


def _routed_sum(y_ref, w):
    y = y_ref[0, 0].astype(F32) * w[:, 0:1]
    for j in range(1, TOP_K):
        y = y + y_ref[j, 0].astype(F32) * w[:, j:j + 1]
    return y


def _inproj_kernel(*refs, ffn):
    if ffn:
        (x_ref, y_ref, tw_ref, g2_ref, sc_ref, sh_ref, g_ref, w_ref,
         xo_ref, s5_ref, rw_ref, mla_ref, ret_ref) = refs
        x = x_ref[0] + g2_ref[0] * _routed_sum(y_ref, tw_ref[0])
        xo_ref[0] = x
    else:
        x_ref, sc_ref, sh_ref, g_ref, w_ref, s5_ref, rw_ref, mla_ref, ret_ref = refs
        x = x_ref[0]
    ms = jnp.mean(x * x, axis=-1, keepdims=True)
    hn = x * lax.rsqrt(ms + NORM_EPS) * g_ref[...]
    hn = hn * (1.0 + sc_ref[0]) + sh_ref[0]
    p = jnp.dot(hn.astype(BF16), w_ref[...], preferred_element_type=F32)
    s5_ref[0] = p[:, IN_S5[0]:IN_S5[1]]
    rw_ref[0] = p[:, IN_RW[0]:IN_RW[1]]
    mla_ref[0] = p[:, IN_MLA[0]:IN_MLA[1]].astype(BF16)
    ret_ref[0] = p[:, IN_RET[0]:IN_RET[1]].astype(BF16)


def in_proj(x, sc, sh, g, w_packed, ffn=None, tm=ROW_TILE):
    bsz, seq, d = x.shape
    widths = [b - a for a, b in (IN_S5, IN_RW, IN_MLA, IN_RET)]
    row = lambda b, i: (b, i, 0)
    per_b = lambda b, i: (b, 0, 0)
    specs = [pl.BlockSpec((1, tm, d), row)]
    args = [x]
    out_specs = [pl.BlockSpec((1, tm, w), row) for w in widths]
    out_shape = [jax.ShapeDtypeStruct((bsz, seq, w), dt) for w, dt in zip(widths, (F32, F32, BF16, BF16))]
    if ffn is not None:
        yg, top_w, gate = ffn
        specs += [pl.BlockSpec((TOP_K, 1, tm, d), lambda b, i: (0, b, i, 0)), pl.BlockSpec((1, tm, LANES), row),
                  pl.BlockSpec((1, 1, d), per_b)]
        args += [yg, top_w, gate]
        out_specs = [pl.BlockSpec((1, tm, d), row)] + out_specs
        out_shape = [jax.ShapeDtypeStruct((bsz, seq, d), F32)] + out_shape
    specs += [pl.BlockSpec((1, 1, d), per_b), pl.BlockSpec((1, 1, d), per_b),
              pl.BlockSpec((1, d), lambda b, i: (0, 0)), pl.BlockSpec((d, IN_PACKED), lambda b, i: (0, 0))]
    args += [sc, sh, g.reshape(1, d), w_packed]
    return pl.pallas_call(
        functools.partial(_inproj_kernel, ffn=ffn is not None),
        grid=(bsz, seq // tm),
        in_specs=specs,
        out_specs=out_specs,
        out_shape=out_shape,
        compiler_params=_params("parallel", "parallel"),
        name="in_proj",
    )(*args)


def _swap_halves(cols, block):
    cols = np.asarray(cols).reshape(-1, 2, block // 2)
    return cols[:, ::-1, :].reshape(-1)


def pack_w_in(w_in_l):
    zero = w_in_l.shape[1]
    s5 = np.arange(0, 256)
    rw = np.arange(256, 1280)
    qc = np.arange(1280, 1536)
    kvc = np.arange(1536, 1664)
    kpe = np.arange(1664, 1696)
    z = lambda n: np.full((n,), zero)
    kpe_slot = np.concatenate([z(MLA_NOPE), kpe, z(LANES - MLA_NOPE - MLA_ROPE)])
    kpe_sw_slot = np.concatenate([z(MLA_NOPE), _swap_halves(kpe, MLA_ROPE), z(LANES - MLA_NOPE - MLA_ROPE)])
    rq = np.arange(1696, 1824)
    rk = np.arange(1824, 1952)
    rv = np.arange(1952, 2208)
    rg = np.arange(2208, 2464)
    idx = np.concatenate([s5, rw, qc, kvc, kpe_slot, kpe_sw_slot,
                          rq, rk, _swap_halves(rq, RET_QK), _swap_halves(rk, RET_QK), rv, rg])
    assert idx.shape[0] == IN_PACKED
    w_ext = jnp.concatenate([w_in_l, jnp.zeros((w_in_l.shape[0], 1), w_in_l.dtype)], axis=1)
    return jnp.take(w_ext, jnp.asarray(idx, jnp.int32), axis=1).astype(BF16)


def _rope_kernel(pos_ref, inv_ref, sgn_ref, cos_ref, sin_ref):
    ang = pos_ref[0].astype(F32) * inv_ref[...]
    cos_ref[0] = jnp.cos(ang)
    sin_ref[0] = jnp.sin(ang) * sgn_ref[...]


def rope_tables(positions, tm=ROPE_TILE):
    bsz, seq = positions.shape
    half = MLA_ROPE // 2
    inv = ROPE_BASE ** (-jnp.arange(0, MLA_ROPE, 2, dtype=F32) / MLA_ROPE)
    inv_t = jnp.tile(inv, LANES // half).reshape(1, LANES)
    sgn = jnp.asarray(np.where((np.arange(LANES) % MLA_ROPE) < half, -1.0, 1.0), F32).reshape(1, LANES)
    row = lambda b, i: (b, i, 0)
    c2 = lambda b, i: (0, 0)
    return pl.pallas_call(
        _rope_kernel,
        grid=(bsz, seq // tm),
        in_specs=[pl.BlockSpec((1, tm, 1), row), pl.BlockSpec((1, LANES), c2), pl.BlockSpec((1, LANES), c2)],
        out_specs=[pl.BlockSpec((1, tm, LANES), row)] * 2,
        out_shape=[jax.ShapeDtypeStruct((bsz, seq, LANES), F32)] * 2,
        compiler_params=_params("parallel", "parallel"),
        name="rope_tables",
    )(positions.reshape(bsz, seq, 1), inv_t, sgn)


S5_UNROLL = 8


def _s5_kernel(u_ref, bre_ref, bim_ref, cre_ref, cim_ref, lam_ref, lamq_ref, lamseg_ref,
               d_ref, gw_ref, gb_ref, o_ref, sre_ref, sim_ref, st_ref, *, nq):
    @pl.when(pl.program_id(1) == 0)
    def _():
        st_ref[...] = jnp.zeros_like(st_ref)

    u = u_ref[0]
    ub = u.astype(BF16)
    sre_ref[...] = jnp.dot(ub, bre_ref[...], preferred_element_type=F32)
    sim_ref[...] = jnp.dot(ub, bim_ref[...], preferred_element_type=F32)
    lam_re = lam_ref[0:1, :]
    lam_im = lam_ref[1:2, :]

    def scan_body(q, carry):
        cr, ci = carry
        rows = pl.ds(pl.multiple_of(q * SUBLANES, SUBLANES), SUBLANES)
        nr = lam_re * cr - lam_im * ci + sre_ref[rows, :]
        ni = lam_re * ci + lam_im * cr + sim_ref[rows, :]
        sre_ref[rows, :] = nr
        sim_ref[rows, :] = ni
        return nr, ni

    zero = jnp.zeros((SUBLANES, S5_FLAT), F32)
    end_re, end_im = lax.fori_loop(0, nq, scan_body, (zero, zero), unroll=S5_UNROLL)

    seg_re = lamseg_ref[0:1, :]
    seg_im = lamseg_ref[1:2, :]
    cr, ci = st_ref[0:1, :], st_ref[1:2, :]
    in_re, in_im = [], []
    for r in range(SUBLANES):
        in_re.append(cr)
        in_im.append(ci)
        er, ei = end_re[r:r + 1, :], end_im[r:r + 1, :]
        cr, ci = seg_re * cr - seg_im * ci + er, seg_re * ci + seg_im * cr + ei
    st_ref[0:1, :] = cr
    st_ref[1:2, :] = ci
    car_re = jnp.concatenate(in_re, axis=0)
    car_im = jnp.concatenate(in_im, axis=0)

    def fix_body(q, _):
        rows = pl.ds(pl.multiple_of(q * SUBLANES, SUBLANES), SUBLANES)
        pr = lamq_ref[0, pl.ds(q, 1), :]
        pi = lamq_ref[1, pl.ds(q, 1), :]
        sre_ref[rows, :] = sre_ref[rows, :] + (pr * car_re - pi * car_im)
        sim_ref[rows, :] = sim_ref[rows, :] + (pr * car_im + pi * car_re)
        return 0

    lax.fori_loop(0, nq, fix_body, 0, unroll=S5_UNROLL)

    y = (jnp.dot(sre_ref[...].astype(BF16), cre_ref[...], preferred_element_type=F32)
         - jnp.dot(sim_ref[...].astype(BF16), cim_ref[...], preferred_element_type=F32))
    y = y + d_ref[...] * u
    y = jax.nn.gelu(y)
    gate = jnp.dot(y.astype(BF16), gw_ref[...], preferred_element_type=F32) + gb_ref[...]
    o_ref[0] = y * _sigmoid(gate)


def s5_prepare(lam_re, lam_im, log_step, b_re, b_im, c_re, c_im, nq):
    dt = jnp.exp(log_step.astype(F32))[:, None]
    mag = jnp.exp(lam_re * dt)
    lb_re = mag * jnp.cos(lam_im * dt)
    lb_im = mag * jnp.sin(lam_im * dt)
    den = lam_re * lam_re + lam_im * lam_im
    n_re = lb_re - 1.0
    f_re = (n_re * lam_re + lb_im * lam_im) / den
    f_im = (lb_im * lam_re - n_re * lam_im) / den
    bb_re = f_re[..., None] * b_re - f_im[..., None] * b_im
    bb_im = f_re[..., None] * b_im + f_im[..., None] * b_re
    eye = jnp.eye(S5_GROUPS, dtype=F32)
    bd_in = lambda t: jnp.einsum('gph,gk->ghkp', t, eye).reshape(GROUP_WIDTH, S5_FLAT).astype(BF16)
    bd_out = lambda t: jnp.einsum('ghp,gk->gpkh', t, eye).reshape(S5_FLAT, GROUP_WIDTH).astype(BF16)
    lam = jnp.stack([lb_re.reshape(-1), lb_im.reshape(-1)])

    def power(n):
        n = n[:, None, None]
        m = jnp.exp(n * (lam_re * dt)[None])
        a = n * (lam_im * dt)[None]
        return jnp.stack([(m * jnp.cos(a)).reshape(-1, S5_FLAT), (m * jnp.sin(a)).reshape(-1, S5_FLAT)])

    lam_q = power(jnp.arange(1, nq + 1, dtype=F32))
    lam_seg = power(jnp.full((1,), float(nq), F32))[:, 0, :]
    return bd_in(bb_re), bd_in(bb_im), bd_out(c_re), bd_out(c_im), lam, lam_q, lam_seg


def s5_mixer(u, prep, d_skip, glu_w, glu_b, chunk=ROW_TILE):
    bsz, seq, w = u.shape
    nq = chunk // SUBLANES
    nchunk = seq // chunk
    bre, bim, cre, cim, lam, lam_q, lam_seg = prep
    up = u.reshape(bsz, nchunk, SUBLANES, nq, w).transpose(0, 1, 3, 2, 4).reshape(bsz, seq, w)
    const2 = lambda b, i: (0, 0)
    out = pl.pallas_call(
        functools.partial(_s5_kernel, nq=nq),
        grid=(bsz, nchunk),
        in_specs=[pl.BlockSpec((1, chunk, w), lambda b, i: (b, i, 0)),
                  pl.BlockSpec((w, S5_FLAT), const2),
                  pl.BlockSpec((w, S5_FLAT), const2),
                  pl.BlockSpec((S5_FLAT, w), const2),
                  pl.BlockSpec((S5_FLAT, w), const2),
                  pl.BlockSpec((2, S5_FLAT), const2),
                  pl.BlockSpec((2, nq, S5_FLAT), lambda b, i: (0, 0, 0)),
                  pl.BlockSpec((2, S5_FLAT), const2),
                  pl.BlockSpec((1, w), const2),
                  pl.BlockSpec((w, w), const2),
                  pl.BlockSpec((1, w), const2)],
        out_specs=pl.BlockSpec((1, chunk, w), lambda b, i: (b, i, 0)),
        out_shape=jax.ShapeDtypeStruct((bsz, seq, w), F32),
        scratch_shapes=[pltpu.VMEM((chunk, S5_FLAT), F32),
                        pltpu.VMEM((chunk, S5_FLAT), F32),
                        pltpu.VMEM((2, S5_FLAT), F32)],
        compiler_params=_params("parallel", "arbitrary"),
        name="s5_mixer",
    )(up, bre, bim, cre, cim, lam, lam_q, lam_seg,
      d_skip.reshape(1, w), glu_w.astype(BF16), glu_b.reshape(1, w))
    return out.reshape(bsz, nchunk, nq, SUBLANES, w).transpose(0, 1, 3, 2, 4).reshape(bsz, seq, w)


def _ret_kernel(h_ref, cos_ref, sin_ref, intra_ref, qw_ref, kw_ref, dec_ref, ones_ref,
                o_ref, st_ref, *, chunk):
    @pl.when(pl.program_id(1) == 0)
    def _():
        st_ref[...] = jnp.zeros_like(st_ref)

    h = h_ref[0].astype(F32)
    cos = cos_ref[0]
    sin = sin_ref[0]
    nqk = RET_HEADS * RET_QK
    q = h[:, 0:nqk] * cos + h[:, 2 * nqk:3 * nqk] * sin
    k = (h[:, nqk:2 * nqk] * cos + h[:, 3 * nqk:4 * nqk] * sin) * (RET_QK ** -0.5)
    v = h[:, 4 * nqk:4 * nqk + GROUP_WIDTH]
    g = h[:, 4 * nqk + GROUP_WIDTH:]
    kb = k.astype(BF16)
    lane_qk = lax.broadcasted_iota(jnp.int32, (chunk, nqk), 1) // RET_QK
    lane_v = lax.broadcasted_iota(jnp.int32, (chunk, GROUP_WIDTH), 1) // RET_V
    state = st_ref[...]
    o = _bdot(q * qw_ref[...], state)
    for hd in range(RET_HEADS):
        qh = jnp.where(lane_qk == hd, q, 0.0).astype(BF16)
        s = lax.dot_general(qh, kb, (((1,), (1,)), ((), ())), preferred_element_type=F32)
        s = s * intra_ref[hd]
        vh = jnp.where(lane_v == hd, v, 0.0).astype(BF16)
        o = o + jnp.dot(s.astype(BF16), vh, preferred_element_type=F32)
    kv = lax.dot_general((k * kw_ref[...]).astype(BF16), v.astype(BF16),
                         (((0,), (0,)), ((), ())), preferred_element_type=F32)
    dec = dec_ref[...]
    st_ref[...] = state * dec + jnp.where(dec > 0.0, kv, 0.0)
    ms = _split_dot(o * o, ones_ref[...])
    o = o * lax.rsqrt(ms + NORM_EPS)
    o_ref[0] = o * (g * _sigmoid(g))


def retention_tables(chunk):
    log_gamma = np.log1p(-np.exp2(-5.0 - np.arange(RET_HEADS, dtype=np.float64)))
    idx = np.arange(chunk, dtype=np.float64)
    diff = idx[:, None] - idx[None, :]
    intra = np.where(diff >= 0, np.exp(np.maximum(diff, 0.0)[None] * log_gamma[:, None, None]), 0.0)
    q_w = np.repeat(np.exp((idx + 1.0)[:, None] * log_gamma[None, :]), RET_QK, axis=1)
    k_w = np.repeat(np.exp((chunk - 1.0 - idx)[:, None] * log_gamma[None, :]), RET_QK, axis=1)
    head_q = np.arange(RET_HEADS * RET_QK) // RET_QK
    head_v = np.arange(GROUP_WIDTH) // RET_V
    same = head_q[:, None] == head_v[None, :]
    dec = np.where(same, np.exp(chunk * log_gamma)[head_q][:, None], 0.0)
    ones = (head_v[:, None] == head_v[None, :]).astype(np.float64) / RET_V
    f = lambda a: jnp.asarray(a, F32)
    return f(intra), f(q_w), f(k_w), f(dec), jnp.asarray(ones, BF16)


def retention_mixer(h, cos_t, sin_t, chunk=RET_CHUNK):
    bsz, seq, wh = h.shape
    intra, q_w, k_w, dec, ones = retention_tables(chunk)
    nqk = RET_HEADS * RET_QK
    row = lambda b, i: (b, i, 0)
    c2 = lambda b, i: (0, 0)
    return pl.pallas_call(
        functools.partial(_ret_kernel, chunk=chunk),
        grid=(bsz, seq // chunk),
        in_specs=[pl.BlockSpec((1, chunk, wh), row),
                  pl.BlockSpec((1, chunk, LANES), row),
                  pl.BlockSpec((1, chunk, LANES), row),
                  pl.BlockSpec((RET_HEADS, chunk, chunk), lambda b, i: (0, 0, 0)),
                  pl.BlockSpec((chunk, nqk), c2),
                  pl.BlockSpec((chunk, nqk), c2),
                  pl.BlockSpec((nqk, GROUP_WIDTH), c2),
                  pl.BlockSpec((GROUP_WIDTH, GROUP_WIDTH), c2)],
        out_specs=pl.BlockSpec((1, chunk, GROUP_WIDTH), row),
        out_shape=jax.ShapeDtypeStruct((bsz, seq, GROUP_WIDTH), F32),
        scratch_shapes=[pltpu.VMEM((nqk, GROUP_WIDTH), F32)],
        compiler_params=_params("parallel", "arbitrary"),
        name="retention",
    )(h, cos_t, sin_t, intra, q_w, k_w, dec, ones)


def _mla_prep_kernel(h_ref, cos_ref, sin_ref, qg_ref, kvg_ref, wqa_ref, wqb_ref, wk_ref, wv_ref,
                     q_ref, k_ref, v_ref, *, scale):
    h = h_ref[0].astype(F32)
    tm = h.shape[0]
    lane = lax.broadcasted_iota(jnp.int32, (tm, LANES), 1)
    is_nope = lane < MLA_NOPE
    is_rope = jnp.logical_and(lane >= MLA_NOPE, lane < MLA_NOPE + MLA_ROPE)
    cm = jnp.where(is_nope, 1.0, jnp.where(is_rope, cos_ref[0], 0.0))
    sm = jnp.where(is_rope, sin_ref[0], 0.0)

    qc = h[:, 0:MLA_Q_RANK]
    qn = (qc * lax.rsqrt(jnp.mean(qc * qc, axis=-1, keepdims=True) + NORM_EPS) * qg_ref[...]).astype(BF16)
    kvc = h[:, MLA_Q_RANK:MLA_Q_RANK + MLA_KV_RANK]
    kvn = (kvc * lax.rsqrt(jnp.mean(kvc * kvc, axis=-1, keepdims=True) + NORM_EPS) * kvg_ref[...]).astype(BF16)
    off = MLA_Q_RANK + MLA_KV_RANK
    kpe = h[:, off:off + LANES] * cm + h[:, off + LANES:off + 2 * LANES] * sm

    qa = jnp.dot(qn, wqa_ref[...], preferred_element_type=F32)
    qb = jnp.dot(qn, wqb_ref[...], preferred_element_type=F32)
    kn = jnp.dot(kvn, wk_ref[...], preferred_element_type=F32)
    v_ref[0] = jnp.dot(kvn, wv_ref[...], preferred_element_type=F32).astype(BF16)
    for hd in range(MLA_HEADS):
        sl = slice(hd * LANES, (hd + 1) * LANES)
        q_ref[0, hd] = ((qa[:, sl] * cm + qb[:, sl] * sm) * scale).astype(BF16)
        k_ref[0, hd] = (kn[:, sl] + kpe).astype(BF16)


def mla_pack_weights(w_q_up, w_kv_up):
    dq = MLA_NOPE + MLA_ROPE
    zq = w_q_up.shape[1]
    zk = w_kv_up.shape[1]
    z = lambda n, zero: np.full((n,), zero)
    ia, ib, ik, iv = [], [], [], []
    for hd in range(MLA_HEADS):
        nope = np.arange(hd * dq, hd * dq + MLA_NOPE)
        pe = np.arange(hd * dq + MLA_NOPE, (hd + 1) * dq)
        pad = LANES - dq
        ia += [nope, pe, z(pad, zq)]
        ib += [z(MLA_NOPE, zq), _swap_halves(pe, MLA_ROPE), z(pad, zq)]
        kv0 = hd * (MLA_NOPE + MLA_V)
        ik += [np.arange(kv0, kv0 + MLA_NOPE), z(LANES - MLA_NOPE, zk)]
        iv += [np.arange(kv0 + MLA_NOPE, kv0 + MLA_NOPE + MLA_V)]
    ext = lambda w: jnp.concatenate([w, jnp.zeros((w.shape[0], 1), w.dtype)], axis=1)
    take = lambda w, idx: jnp.take(ext(w), jnp.asarray(np.concatenate(idx), jnp.int32), axis=1).astype(BF16)
    return take(w_q_up, ia), take(w_q_up, ib), take(w_kv_up, ik), take(w_kv_up, iv)


def mla_prep(h, cos_t, sin_t, q_norm_g, kv_norm_g, packed, tm=ROW_TILE):
    bsz, seq, wh = h.shape
    wqa, wqb, wk, wv = packed
    row = lambda b, i: (b, i, 0)
    c2 = lambda b, i: (0, 0)
    hrow = lambda b, i: (b, 0, i, 0)
    scale = (MLA_NOPE + MLA_ROPE) ** -0.5
    return pl.pallas_call(
        functools.partial(_mla_prep_kernel, scale=scale),
        grid=(bsz, seq // tm),
        in_specs=[pl.BlockSpec((1, tm, wh), row),
                  pl.BlockSpec((1, tm, LANES), row),
                  pl.BlockSpec((1, tm, LANES), row),
                  pl.BlockSpec((1, MLA_Q_RANK), c2),
                  pl.BlockSpec((1, MLA_KV_RANK), c2),
                  pl.BlockSpec(wqa.shape, c2),
                  pl.BlockSpec(wqb.shape, c2),
                  pl.BlockSpec(wk.shape, c2),
                  pl.BlockSpec(wv.shape, c2)],
        out_specs=[pl.BlockSpec((1, MLA_HEADS, tm, LANES), hrow),
                   pl.BlockSpec((1, MLA_HEADS, tm, LANES), hrow),
                   pl.BlockSpec((1, tm, GROUP_WIDTH), row)],
        out_shape=[jax.ShapeDtypeStruct((bsz, MLA_HEADS, seq, LANES), BF16),
                   jax.ShapeDtypeStruct((bsz, MLA_HEADS, seq, LANES), BF16),
                   jax.ShapeDtypeStruct((bsz, seq, GROUP_WIDTH), BF16)],
        compiler_params=_params("parallel", "parallel"),
        name="mla_prep",
    )(h, cos_t, sin_t, q_norm_g.reshape(1, -1), kv_norm_g.reshape(1, -1), wqa, wqb, wk, wv)


def _attn_kernel(q_ref, k_ref, v_ref, o_ref, m_ref, l_ref, acc_ref, *, blk):
    qi = pl.program_id(2)
    heads = range(2)

    def step(q_rows, key0, size, masked):
        keys = pl.ds(pl.multiple_of(key0, size), size)
        vs = v_ref[0, keys, :]
        s = [lax.dot_general(q_ref[0, hh, q_rows, :], k_ref[0, hh, keys, :], (((1,), (1,)), ((), ())),
                             preferred_element_type=F32) for hh in heads]
        if masked:
            r = lax.broadcasted_iota(jnp.int32, (size, size), 0)
            c = lax.broadcasted_iota(jnp.int32, (size, size), 1)
            s = [jnp.where(c <= r, t, -jnp.inf) for t in s]
        m_prev = [m_ref[hh, q_rows, :] for hh in heads]
        m_new = [jnp.maximum(m_prev[hh], jnp.max(s[hh], axis=-1, keepdims=True)) for hh in heads]
        p = [jnp.exp(s[hh] - jnp.concatenate([m_new[hh]] * (size // LANES), axis=1)) for hh in heads]
        alpha = [jnp.exp(m_prev[hh] - m_new[hh]) for hh in heads]
        for hh in heads:
            l_ref[hh, q_rows, :] = alpha[hh] * l_ref[hh, q_rows, :] + jnp.sum(p[hh], axis=-1, keepdims=True)
            acc_ref[hh, q_rows, :] = (alpha[hh] * acc_ref[hh, q_rows, :]
                                      + jnp.dot(p[hh].astype(BF16), vs, preferred_element_type=F32))
            m_ref[hh, q_rows, :] = m_new[hh]

    for hh in heads:
        m_ref[hh] = jnp.full((blk, LANES), -jnp.inf, F32)
        l_ref[hh] = jnp.zeros((blk, LANES), F32)
        acc_ref[hh] = jnp.zeros((blk, LANES), F32)

    def body(j, _):
        step(slice(0, blk), j * blk, blk, False)
        return 0

    lax.fori_loop(0, qi, body, 0)
    half = blk // 2
    step(slice(0, half), qi * blk, half, True)
    step(slice(half, blk), qi * blk, half, False)
    step(slice(half, blk), qi * blk + half, half, True)

    lane = lax.broadcasted_iota(jnp.int32, (blk, LANES), 1)
    o0 = acc_ref[0] / l_ref[0]
    o1 = acc_ref[1] / l_ref[1]
    o_ref[0] = jnp.where(lane < MLA_V, o0, o1)


def causal_attention(q, k, v, blk=ATTN_BLOCK):
    bsz, nh, seq, dk = q.shape
    return pl.pallas_call(
        functools.partial(_attn_kernel, blk=blk),
        grid=(bsz, nh // 2, seq // blk),
        in_specs=[pl.BlockSpec((1, 2, blk, dk), lambda b, p, i: (b, p, i, 0)),
                  pl.BlockSpec((1, 2, seq, dk), lambda b, p, i: (b, p, 0, 0)),
                  pl.BlockSpec((1, seq, LANES), lambda b, p, i: (b, 0, p))],
        out_specs=pl.BlockSpec((1, blk, LANES), lambda b, p, i: (b, i, p)),
        out_shape=jax.ShapeDtypeStruct((bsz, seq, nh // 2 * LANES), F32),
        scratch_shapes=[pltpu.VMEM((2, blk, LANES), F32)] * 3,
        compiler_params=_params("parallel", "parallel", "arbitrary"),
        name="mla_attention",
    )(q, k, v)


def _rw_prep_kernel(*refs, first):
    if first:
        (h_ref, prev_ref, mu_ref, w0_ref, w2_ref, a0_ref, a2_ref, g2_ref, kk_ref, ka_ref, rk_ref, ones_ref, tri_ref,
         q_ref, y0_ref, phi_ref, dlt_ref, v_ref, bonus_ref, g_ref) = refs
    else:
        (h_ref, prev_ref, mu_ref, w0_ref, w2_ref, a0_ref, a2_ref, g2_ref, kk_ref, ka_ref, rk_ref, ones_ref, tri_ref,
         vf_ref, v0_ref, v1_ref, v2_ref,
         q_ref, y0_ref, phi_ref, dlt_ref, v_ref, bonus_ref, g_ref) = refs
    h = h_ref[0]
    tm = h.shape[0]
    last = prev_ref[0, SUBLANES - 1:SUBLANES, :]
    last = jnp.where(pl.program_id(1) == 0, 0.0, last)
    row = lax.broadcasted_iota(jnp.int32, h.shape, 0)
    h_prev = jnp.where(row == 0, last, pltpu.roll(h, 1, axis=0))
    h = h + (h_prev - h) * mu_ref[...]
    w = GROUP_WIDTH
    r = h[:, 0:w]
    k = h[:, w:2 * w]
    v = h[:, 2 * w:3 * w]
    wa = h[:, 3 * w:3 * w + LANES]
    gd = h[:, 3 * w + LANES:]
    z = w0_ref[...] + _bdot(jnp.tanh(wa), w2_ref[...])
    nz = -z
    softplus = jnp.maximum(nz, 0.0) + jnp.log(1.0 + jnp.exp(-jnp.abs(nz)))
    log_decay = -jnp.exp(-softplus - 0.5)
    a = _sigmoid(a0_ref[...] + _bdot(wa, a2_ref[...]))
    g_ref[0] = _bdot(_sigmoid(gd), g2_ref[...])
    if not first:
        mix = _sigmoid(v0_ref[...] + _bdot(_bdot(v, v1_ref[...]), v2_ref[...]))
        v = v + (vf_ref[0] - v) * mix
    kk = k * kk_ref[...]
    norm = jnp.sqrt(_split_dot(kk * kk, ones_ref[...]))
    kk = kk / jnp.maximum(norm, 1e-12)
    k = k * (1.0 + (a - 1.0) * ka_ref[...])
    v_ref[0] = v
    bonus_ref[0] = _split_dot(r * k * rk_ref[...], ones_ref[...]) * v
    _wkv_chunk_terms(r, log_decay, k, kk, kk * a, v, tri_ref[...], q_ref.at[0], y0_ref.at[0], phi_ref.at[0], dlt_ref.at[0],
                     tm // WKV_CHUNK)


def rw_pack_weights(w2, a2, v1=None, v2=None):
    zeros = lambda n, m: jnp.zeros((n, m), F32)
    half = LANES // 2
    w2p = jnp.concatenate([w2, zeros(half, GROUP_WIDTH)], axis=0).astype(BF16)
    a2p = jnp.concatenate([zeros(half, GROUP_WIDTH), a2], axis=0).astype(BF16)
    if v1 is None:
        return w2p, a2p
    v1p = jnp.concatenate([v1, zeros(GROUP_WIDTH, LANES - v1.shape[1])], axis=1).astype(BF16)
    v2p = jnp.concatenate([v2, zeros(LANES - v2.shape[0], GROUP_WIDTH)], axis=0).astype(BF16)
    return w2p, a2p, v1p, v2p


def head_ones(width, head, scale):
    hd = np.arange(width) // head
    return jnp.asarray((hd[:, None] == hd[None, :]).astype(np.float32) * scale, BF16)


def rw_prep(h, mu, w0, w2p, a0, a2p, g2, k_k, k_a, r_k, vmix=None, tm=ROW_TILE):
    bsz, seq, wh = h.shape
    w = GROUP_WIDTH
    first = vmix is None
    row = lambda b, i: (b, i, 0)
    c2 = lambda b, i: (0, 0)
    vec = lambda t: t.reshape(1, -1)
    prev_map = lambda b, i: (b, jnp.maximum(i * (tm // SUBLANES) - 1, 0), 0)
    c = WKV_CHUNK
    npair = w // LANES
    tri = jnp.asarray(np.tril(np.ones((c, c), np.float32)), BF16)
    args = [h, h, vec(mu), vec(w0), w2p, vec(a0), a2p, g2.astype(BF16), vec(k_k), vec(k_a), vec(r_k),
            head_ones(w, RW_HEAD, 1.0), tri]
    specs = [pl.BlockSpec((1, tm, wh), row), pl.BlockSpec((1, SUBLANES, wh), prev_map),
             pl.BlockSpec((1, wh), c2), pl.BlockSpec((1, w), c2), pl.BlockSpec((LANES, w), c2),
             pl.BlockSpec((1, w), c2), pl.BlockSpec((LANES, w), c2), pl.BlockSpec((LANES, w), c2),
             pl.BlockSpec((1, w), c2), pl.BlockSpec((1, w), c2), pl.BlockSpec((1, w), c2),
             pl.BlockSpec((w, w), c2), pl.BlockSpec((c, c), c2)]
    if not first:
        v_first, v0, v1p, v2p = vmix
        args += [v_first, vec(v0), v1p, v2p]
        specs += [pl.BlockSpec((1, tm, w), row), pl.BlockSpec((1, w), c2),
                  pl.BlockSpec((w, LANES), c2), pl.BlockSpec((LANES, w), c2)]
    act = pl.BlockSpec((1, tm, w), row)
    mat = pl.BlockSpec((1, tm // c, npair, LANES, LANES), lambda b, i: (b, i, 0, 0, 0))
    ashape = jax.ShapeDtypeStruct((bsz, seq, w), F32)
    mshape = jax.ShapeDtypeStruct((bsz, seq // c, npair, LANES, LANES), F32)
    return pl.pallas_call(
        functools.partial(_rw_prep_kernel, first=first),
        grid=(bsz, seq // tm),
        in_specs=specs,
        out_specs=[act, act, mat, mat, act, act, act],
        out_shape=[ashape, ashape, mshape, mshape, ashape, ashape, ashape],
        compiler_params=_params("parallel", "parallel"),
        name="rwkv_prep",
    )(*args)


WKV_CHUNK = 64
_NN = (((1,), (0,)), ((), ()))
_NT = (((1,), (1,)), ((), ()))
_TN = (((0,), (0,)), ((), ()))


def _split(a):
    hi = a.astype(BF16)
    return hi, (a - hi.astype(F32)).astype(BF16)


def _mm(a, b, dims=_NN, passes=1):
    if passes == 1:
        return lax.dot_general(a.astype(BF16), b.astype(BF16), dims, preferred_element_type=F32)
    ah, al = _split(a)
    bh, bl = _split(b)
    d = lambda x, y: lax.dot_general(x, y, dims, preferred_element_type=F32)
    return d(ah, bh) + (d(ah, bl) + d(al, bh))


def _wkv_chunk_terms(r_all, lw_all, k_all, kk_all, b_all, v_all, tri, q_ref, y0_ref, phi_ref, dlt_ref, nchunk):
    c = WKV_CHUNK
    n2 = 2 * c
    lane = lax.broadcasted_iota(jnp.int32, (c, LANES), 1)
    head0 = lane < RW_HEAD
    row = lax.broadcasted_iota(jnp.int32, (n2, n2), 0)
    col = lax.broadcasted_iota(jnp.int32, (n2, n2), 1)
    strict = col < row
    incl = col <= row
    eye = (row == col).astype(F32)
    stack = lambda x: jnp.concatenate([jnp.where(head0, x, 0.0), jnp.where(head0, 0.0, x)], axis=0)
    fold = lambda x: x[:c] + x[c:]
    chains = [(ci, p) for ci in range(nchunk) for p in range(GROUP_WIDTH // LANES)]
    st = []
    for ci, p in chains:
        sl = slice(p * LANES, (p + 1) * LANES)
        rows = slice(ci * c, (ci + 1) * c)
        lw = lw_all[rows, sl]
        lw_hi, lw_lo = _split(lw)
        cum = (jnp.dot(tri, lw_hi, preferred_element_type=F32)
               + jnp.dot(tri, lw_lo, preferred_element_type=F32))
        last = cum[c - 1:c, :]
        e_in = jnp.exp(cum)
        e_ex = jnp.exp(cum - lw)
        e_neg = jnp.exp(-cum)
        e_end = jnp.exp(last - cum)
        kk, r, k, b, v = kk_all[rows, sl], r_all[rows, sl], k_all[rows, sl], b_all[rows, sl], v_all[rows, sl]
        kt2 = stack(kk * e_ex)
        rt2 = stack(r * e_in)
        lhs = jnp.concatenate([kt2, rt2], axis=0)
        gb = _mm(lhs, stack(b * e_neg), _NT)
        gk = _mm(lhs, stack(k * e_neg), _NT)
        x = -jnp.where(strict, gb[:n2], 0.0)
        st.append(dict(sl=sl, rows=rows, kt2=kt2, rt2=rt2, kp2=stack(k * e_end), bp2=stack(b * e_end), v2=stack(v),
                       g2=jnp.where(strict, gk[:n2], 0.0), rb2=jnp.where(incl, gb[n2:], 0.0),
                       rk2=jnp.where(incl, gk[n2:], 0.0), last=last, x=x, t=eye + x))
    for _ in range(int(np.log2(c)) - 1):
        for d in st:
            d["x"] = _mm(d["x"], d["x"])
        for d in st:
            d["t"] = d["t"] + _mm(d["t"], d["x"])
    for d in st:
        d["gv2"] = _mm(d["g2"], d["v2"])
    for d in st:
        d["tku"] = _mm(d["t"], jnp.concatenate([d["kt2"], d["gv2"]], axis=1))
    for d in st:
        d["rbz"] = _mm(d["rb2"], d["tku"])
    for (ci, p), d in zip(chains, st):
        sl, rows, tku, rbz = d["sl"], d["rows"], d["tku"], d["rbz"]
        q_ref[rows, sl] = fold(d["rt2"] - rbz[:, :LANES])
        y0_ref[rows, sl] = fold(_mm(d["rk2"], d["v2"]) - rbz[:, LANES:])
        decay = jnp.where(row == col, jnp.broadcast_to(jnp.exp(d["last"]), (n2, LANES)), 0.0)
        phi_ref[ci, p] = decay - _mm(d["bp2"], tku[:, :LANES], _TN)
        dlt_ref[ci, p] = _mm(d["kp2"], d["v2"], _TN) - _mm(d["bp2"], tku[:, LANES:], _TN)


def _wkv_seq_kernel(q_ref, y0_ref, phi_ref, dlt_ref, y_ref, st_ref, *, nchunk):
    @pl.when(pl.program_id(1) == 0)
    def _():
        st_ref[...] = jnp.zeros_like(st_ref)

    c = WKV_CHUNK
    npair = GROUP_WIDTH // LANES
    m = [st_ref[p] for p in range(npair)]
    for i in range(nchunk):
        rows = slice(i * c, (i + 1) * c)
        for p in range(npair):
            sl = slice(p * LANES, (p + 1) * LANES)
            y_ref[0, rows, sl] = _mm(q_ref[0, rows, sl], m[p]) + y0_ref[0, rows, sl]
        m = [_mm(phi_ref[0, i, p], m[p], _NN, 3) + dlt_ref[0, i, p] for p in range(npair)]
    for p in range(npair):
        st_ref[p] = m[p]


def wkv_sequential(q, y0, phi, dlt, block=ROW_TILE):
    bsz, seq, w = q.shape
    c = WKV_CHUNK
    nchunk = block // c
    npair = w // LANES
    row = lambda bb, i: (bb, i, 0)
    blk = pl.BlockSpec((1, block, w), row)
    mat = pl.BlockSpec((1, nchunk, npair, LANES, LANES), lambda bb, i: (bb, i, 0, 0, 0))
    return pl.pallas_call(
        functools.partial(_wkv_seq_kernel, nchunk=nchunk),
        grid=(bsz, seq // block),
        in_specs=[blk, blk, mat, mat],
        out_specs=blk,
        out_shape=jax.ShapeDtypeStruct((bsz, seq, w), F32),
        scratch_shapes=[pltpu.VMEM((npair, LANES, LANES), F32)],
        compiler_params=_params("parallel", "arbitrary"),
        name="wkv7_chunk_scan",
    )(q, y0, phi, dlt)


def _outproj_kernel(x_ref, ys5_ref, yrw_ref, bonus_ref, grw_ref, ymla_ref, yret_ref, lng_ref, lnb_ref, ones_ref,
                    wout_ref, g1_ref, ng_ref, sc_ref, sh_ref, rhi_ref, rlo_ref, rb_ref,
                    xo_ref, hn_ref, idx_ref, tw_ref, cnt_ref):
    y = yrw_ref[0]
    mean = _split_dot(y, ones_ref[...])
    yc = y - mean
    var = _split_dot(yc * yc, ones_ref[...])
    yrw = (yc * lax.rsqrt(var + RW_GN_EPS) * lng_ref[...] + lnb_ref[...] + bonus_ref[0]) * grw_ref[0]
    w = GROUP_WIDTH
    mixed = (_bdot(ys5_ref[0], wout_ref[0:w, :]) + _bdot(yrw, wout_ref[w:2 * w, :])
             + _bdot(ymla_ref[0], wout_ref[2 * w:3 * w, :]) + _bdot(yret_ref[0], wout_ref[3 * w:4 * w, :]))
    x = x_ref[0] + g1_ref[0] * mixed
    xo_ref[0] = x
    ms = jnp.mean(x * x, axis=-1, keepdims=True)
    hn = x * lax.rsqrt(ms + NORM_EPS) * ng_ref[...]
    hn = hn * (1.0 + sc_ref[0]) + sh_ref[0]
    hi = hn.astype(BF16)
    hn_ref[0] = hi
    lo = (hn - hi.astype(F32)).astype(BF16)
    logits = (jnp.dot(hi, rhi_ref[...], preferred_element_type=F32)
              + jnp.dot(lo, rhi_ref[...], preferred_element_type=F32)
              + jnp.dot(hi, rlo_ref[...], preferred_element_type=F32) + rb_ref[...])
    lane = lax.broadcasted_iota(jnp.int32, logits.shape, 1)
    lane_f = lane.astype(F32)
    cur = jnp.where(lane < N_EXPERTS, logits, -jnp.inf)
    idx_out = jnp.zeros(logits.shape, F32)
    val_out = jnp.zeros(logits.shape, F32)
    picked = jnp.zeros(logits.shape, F32)
    top = None
    denom = None
    for j in range(TOP_K):
        m = jnp.max(cur, axis=-1, keepdims=True)
        sel = jnp.min(jnp.where(cur == m, lane_f, float(LANES)), axis=-1, keepdims=True)
        hit = lane_f == sel
        cur = jnp.where(hit, -jnp.inf, cur)
        picked = picked + jnp.where(hit, 1.0, 0.0)
        top = m if top is None else top
        e = jnp.exp(m - top)
        denom = e if denom is None else denom + e
        idx_out = jnp.where(lane == j, sel, idx_out)
        val_out = jnp.where(lane == j, e, val_out)
    idx_ref[0] = idx_out.astype(jnp.int32)
    tw_ref[0] = val_out / denom
    cnt_ref[0, 0] = jnp.broadcast_to(jnp.sum(picked, axis=0, keepdims=True), (SUBLANES, LANES))


def out_proj(x, y_s5, y_rw, bonus, g_rw, y_mla, y_ret, ln_g, ln_b, w_out, g1, norm_g, sc2, sh2,
             router_w, router_b, tm=ROW_TILE):
    bsz, seq, d = x.shape
    w = GROUP_WIDTH
    row = lambda b, i: (b, i, 0)
    per_b = lambda b, i: (b, 0, 0)
    c2 = lambda b, i: (0, 0)
    vec = lambda t: t.reshape(1, -1)
    pad = LANES - router_w.shape[1]
    rw_pad = jnp.concatenate([router_w, jnp.zeros((d, pad), F32)], axis=1)
    r_hi = rw_pad.astype(BF16)
    r_lo = (rw_pad - r_hi.astype(F32)).astype(BF16)
    rb = jnp.concatenate([router_b, jnp.zeros((pad,), F32)]).reshape(1, LANES)
    mixer = pl.BlockSpec((1, tm, w), row)
    return pl.pallas_call(
        _outproj_kernel,
        grid=(bsz, seq // tm),
        in_specs=[pl.BlockSpec((1, tm, d), row), mixer, mixer, mixer, mixer, mixer, mixer,
                  pl.BlockSpec((1, w), c2), pl.BlockSpec((1, w), c2), pl.BlockSpec((w, w), c2),
                  pl.BlockSpec((4 * w, d), c2),
                  pl.BlockSpec((1, 1, d), per_b), pl.BlockSpec((1, d), c2),
                  pl.BlockSpec((1, 1, d), per_b), pl.BlockSpec((1, 1, d), per_b),
                  pl.BlockSpec((d, LANES), c2), pl.BlockSpec((d, LANES), c2), pl.BlockSpec((1, LANES), c2)],
        out_specs=[pl.BlockSpec((1, tm, d), row), pl.BlockSpec((1, tm, d), row), pl.BlockSpec((1, tm, LANES), row),
                   pl.BlockSpec((1, tm, LANES), row), pl.BlockSpec((1, 1, SUBLANES, LANES), lambda b, i: (b, i, 0, 0))],
        out_shape=[jax.ShapeDtypeStruct((bsz, seq, d), F32), jax.ShapeDtypeStruct((bsz, seq, d), BF16),
                   jax.ShapeDtypeStruct((bsz, seq, LANES), jnp.int32), jax.ShapeDtypeStruct((bsz, seq, LANES), F32),
                   jax.ShapeDtypeStruct((bsz, seq // tm, SUBLANES, LANES), F32)],
        compiler_params=_params("parallel", "parallel"),
        name="out_proj",
    )(x, y_s5, y_rw, bonus, g_rw, y_mla, y_ret, vec(ln_g), vec(ln_b), head_ones(w, RW_HEAD, 1.0 / RW_HEAD),
      w_out.astype(BF16), g1, vec(norm_g), sc2, sh2, r_hi, r_lo, rb)


MOE_ROWS = 512


def _moe_kernel(blk_e_ref, blk_on_ref, x_ref, wg_ref, bg_ref, wu_ref, bu_ref, wd_ref, bd_ref, o_ref, wb_ref):
    i = pl.program_id(0)
    changed = jnp.logical_or(i == 0, blk_e_ref[i] != blk_e_ref[jnp.maximum(i - 1, 0)])

    @pl.when(changed)
    def _():
        wb_ref[0] = wg_ref[0, 0].astype(BF16)
        wb_ref[1] = wu_ref[0, 0].astype(BF16)
        wb_ref[2] = wd_ref[0, 0].astype(BF16)

    @pl.when(blk_on_ref[i] == 0)
    def _():
        o_ref[...] = jnp.zeros_like(o_ref)

    @pl.when(blk_on_ref[i] > 0)
    def _():
        x = x_ref[...]
        gt = jnp.minimum(jnp.dot(x, wb_ref[0], preferred_element_type=F32) + bg_ref[0, 0], SWIGLU_LIMIT)
        up = jnp.clip(jnp.dot(x, wb_ref[1], preferred_element_type=F32) + bu_ref[0, 0], -SWIGLU_LIMIT, SWIGLU_LIMIT)
        act = gt * _sigmoid(SWIGLU_ALPHA * gt) * (up + 1.0)
        o_ref[...] = (jnp.dot(act.astype(BF16), wb_ref[2], preferred_element_type=F32) + bd_ref[0, 0]).astype(BF16)


def moe_experts(xb, blk_e, blk_on, layer, w_gate, b_gate, w_up, b_up, w_down, b_down):
    p_rows, d = xb.shape
    depth, n_e, _, de = w_gate.shape
    wmap = lambda i, e, on: (layer, e[i], 0, 0)
    rows = lambda i, e, on: (i, 0)
    return pl.pallas_call(
        _moe_kernel,
        grid_spec=pltpu.PrefetchScalarGridSpec(
            num_scalar_prefetch=2,
            grid=(p_rows // MOE_ROWS,),
            in_specs=[pl.BlockSpec((MOE_ROWS, d), rows),
                      pl.BlockSpec((1, 1, d, de), wmap), pl.BlockSpec((1, 1, 1, de), wmap),
                      pl.BlockSpec((1, 1, d, de), wmap), pl.BlockSpec((1, 1, 1, de), wmap),
                      pl.BlockSpec((1, 1, de, d), wmap), pl.BlockSpec((1, 1, 1, d), wmap)],
            out_specs=pl.BlockSpec((MOE_ROWS, d), rows),
            scratch_shapes=[pltpu.VMEM((3, d, de), BF16)]),
        out_shape=jax.ShapeDtypeStruct((p_rows, d), BF16),
        compiler_params=_params("arbitrary"),
        name="moe_experts",
    )(blk_e, blk_on, xb, w_gate, b_gate.reshape(depth, n_e, 1, de), w_up, b_up.reshape(depth, n_e, 1, de),
      w_down, b_down.reshape(depth, n_e, 1, d))


def moe_route(top_idx, counts):
    t = top_idx.shape[0]
    n_assign = t * TOP_K
    flat_e = top_idx.reshape(-1)
    iota = jnp.arange(n_assign, dtype=jnp.int32)
    _, order = lax.sort((flat_e, iota), num_keys=1)
    _, rank = lax.sort((order, iota), num_keys=1)
    start = jnp.cumsum(counts) - counts
    padded = (counts + MOE_ROWS - 1) // MOE_ROWS * MOE_ROWS
    pad_end = jnp.cumsum(padded)
    pad_start = pad_end - padded
    pos = rank + (pad_start - start)[flat_e]
    p_rows = n_assign + N_EXPERTS * MOE_ROWS
    n_blocks = p_rows // MOE_ROWS
    blk_first = jnp.arange(n_blocks, dtype=jnp.int32) * MOE_ROWS
    blk_e = jnp.minimum(jnp.sum(pad_end[None, :] <= blk_first[:, None], axis=1, dtype=jnp.int32), N_EXPERTS - 1)
    blk_within = blk_first - pad_start[blk_e]
    blk_left = counts[blk_e] - blk_within
    blk_on = (blk_left > 0).astype(jnp.int32)
    r = jnp.arange(MOE_ROWS, dtype=jnp.int32)[None, :]
    src = jnp.clip((start[blk_e] + blk_within)[:, None] + r, 0, n_assign - 1)
    row = blk_first[:, None] + r
    buf_tok = jnp.where(r < blk_left[:, None], order[src.reshape(-1)].reshape(n_blocks, MOE_ROWS) // TOP_K, row % t)
    return buf_tok.reshape(-1), pos.reshape(t, TOP_K), blk_e, blk_on


def moe_ffn(hn_bf16, top_idx, counts, layer, w_gate, b_gate, w_up, b_up, w_down, b_down):
    bsz, seq, d = hn_bf16.shape
    t = bsz * seq
    buf_tok, pos, blk_e, blk_on = moe_route(top_idx.reshape(t, LANES)[:, :TOP_K], counts)
    xb = hn_bf16.reshape(t, d).at[buf_tok].get(mode='promise_in_bounds')
    yb = moe_experts(xb, blk_e, blk_on, layer, w_gate, b_gate, w_up, b_up, w_down, b_down)
    yg = yb.at[pos.T.reshape(-1)].get(mode='promise_in_bounds')
    return yg.reshape(TOP_K, bsz, seq, d)


def _final_kernel(x_ref, y_ref, w_ref, g_ref, ng_ref, o_ref):
    x = x_ref[0] + g_ref[0] * _routed_sum(y_ref, w_ref[0])
    o_ref[0] = x * lax.rsqrt(jnp.mean(x * x, axis=-1, keepdims=True) + NORM_EPS) * ng_ref[...]


def final_combine_norm(x, yg, top_w, gate, final_g, tm=ROW_TILE):
    bsz, seq, d = x.shape
    row = lambda b, i: (b, i, 0)
    return pl.pallas_call(
        _final_kernel,
        grid=(bsz, seq // tm),
        in_specs=[pl.BlockSpec((1, tm, d), row),
                  pl.BlockSpec((TOP_K, 1, tm, d), lambda b, i: (0, b, i, 0)),
                  pl.BlockSpec((1, tm, LANES), row),
                  pl.BlockSpec((1, 1, d), lambda b, i: (b, 0, 0)),
                  pl.BlockSpec((1, d), lambda b, i: (0, 0))],
        out_specs=pl.BlockSpec((1, tm, d), row),
        out_shape=jax.ShapeDtypeStruct((bsz, seq, d), F32),
        compiler_params=_params("parallel", "parallel"),
        name="final_combine_norm",
    )(x, yg, top_w, gate, final_g.reshape(1, d))


def kernel(x, c, positions, ada_w, ada_b, norm_mix_g, norm_ffn_g, w_in, w_out,
           s5_lambda_re, s5_lambda_im, s5_log_step, s5_b_re, s5_b_im, s5_c_re, s5_c_im,
           s5_d, s5_glu_w, s5_glu_b,
           rw_mu, rw_w0, rw_w2, rw_a0, rw_a2, rw_g2, rw_k_k, rw_k_a, rw_r_k, rw_ln_g, rw_ln_b,
           rw_v0, rw_v1, rw_v2,
           mla_q_norm_g, mla_kv_norm_g, mla_w_q_up, mla_w_kv_up,
           router_w, router_b, ex_w_gate, ex_b_gate, ex_w_up, ex_b_up, ex_w_down, ex_b_down,
           final_norm_g):
    depth = w_in.shape[0]
    bsz, seq, _ = x.shape
    assert seq % max(ROW_TILE, ROPE_TILE, ATTN_BLOCK) == 0 and (bsz * seq * TOP_K) % MOE_ROWS == 0
    mod = adaln_mod(c, ada_w, ada_b)
    cos_t, sin_t = rope_tables(positions)
    v_first = None
    ffn = None
    for l in range(depth):
        sh1, sc1, g1, sh2, sc2, g2 = [m[:, None, :] for m in jnp.split(mod[l], N_MOD, axis=-1)]
        proj = in_proj(x, sc1, sh1, norm_mix_g[l], pack_w_in(w_in[l]), ffn)
        if ffn is not None:
            x = proj[0]
        s5_u, rw_in, mla_in, ret_in = proj[-4:]

        prep = s5_prepare(s5_lambda_re[l], s5_lambda_im[l], s5_log_step[l], s5_b_re[l], s5_b_im[l],
                          s5_c_re[l], s5_c_im[l], ROW_TILE // SUBLANES)
        y_s5 = s5_mixer(s5_u, prep, s5_d[l], s5_glu_w[l], s5_glu_b[l])

        if l == 0:
            w2p, a2p = rw_pack_weights(rw_w2[l], rw_a2[l])
            vmix = None
        else:
            w2p, a2p, v1p, v2p = rw_pack_weights(rw_w2[l], rw_a2[l], rw_v1[l - 1], rw_v2[l - 1])
            vmix = (v_first, rw_v0[l - 1], v1p, v2p)
        q_rw, y0_rw, phi_rw, dlt_rw, v_rw, bonus, g_rw = rw_prep(
            rw_in, rw_mu[l], rw_w0[l], w2p, rw_a0[l], a2p, rw_g2[l], rw_k_k[l], rw_k_a[l],
            rw_r_k[l].reshape(-1), vmix)
        if l == 0:
            v_first = v_rw
        y_rw = wkv_sequential(q_rw, y0_rw, phi_rw, dlt_rw)

        q, k, v = mla_prep(mla_in, cos_t, sin_t, mla_q_norm_g[l], mla_kv_norm_g[l],
                           mla_pack_weights(mla_w_q_up[l], mla_w_kv_up[l]))
        y_mla = causal_attention(q, k, v)

        y_ret = retention_mixer(ret_in, cos_t, sin_t)

        x, hn, top_idx, top_w, cnt = out_proj(x, y_s5, y_rw, bonus, g_rw, y_mla, y_ret, rw_ln_g[l], rw_ln_b[l],
                                              w_out[l], g1, norm_ffn_g[l], sc2, sh2, router_w[l], router_b[l])
        counts = jnp.sum(cnt[:, :, 0, :N_EXPERTS], axis=(0, 1)).astype(jnp.int32)
        yg = moe_ffn(hn, top_idx, counts, l, ex_w_gate, ex_b_gate, ex_w_up, ex_b_up, ex_w_down, ex_b_down)
        ffn = (yg, top_w, g2)
    return final_combine_norm(x, *ffn, final_norm_g)
```
